```python
import math
import jax, jax.numpy as jnp
from jax import lax
import numpy as np

D_MODEL = 2048
BATCH = 2
SEQ = 4096
DEPTH = 2
DEC_BATCH = 128
DEC_SEQ = 1
PAST_LEN = 2048
PAGE_SIZE = 128

MIX_WIDTH = D_MODEL
GROUP_WIDTH = MIX_WIDTH // 4
HEAD_DIM = 64
H_FOX = GROUP_WIDTH // HEAD_DIM
LRU_WIDTH = GROUP_WIDTH
LRU_BLOCKS = 8
LRU_BLOCK_DIM = LRU_WIDTH // LRU_BLOCKS
LRU_C = 8.0
CONV_W = 4
H_NSA = GROUP_WIDTH // HEAD_DIM
KVH_NSA = 2
NSA_GROUP = H_NSA // KVH_NSA
KV_W = KVH_NSA * HEAD_DIM
NSA_BLOCK = 64
NSA_TOPN = 16
NSA_WINDOW = 512
FORCED_SCORE = 1e4
H_GDN = 4
GDN_DK = GROUP_WIDTH // H_GDN
GDN_DV = GDN_DK
GDN_CHUNK = 64
D_FF = 4 * D_MODEL
QBLOCK = 128
RMS_EPS = 1e-6
NEG = -1e30
IN_SPLITS = (GROUP_WIDTH, GROUP_WIDTH, GROUP_WIDTH, H_FOX,
             LRU_WIDTH, LRU_WIDTH,
             GROUP_WIDTH, KV_W, KV_W, KV_W, KV_W, KV_W, KV_W, 3 * H_NSA,
             3 * GROUP_WIDTH, GROUP_WIDTH, H_GDN, H_GDN)
N_IN = sum(IN_SPLITS)

kernel_name = 'hybrid_fox_rglru_nsa_gdn_decode_step'


def rmsnorm(x, g):
    xf = x.astype(jnp.float32)
    y = xf * lax.rsqrt(jnp.mean(xf * xf, axis=-1, keepdims=True) + RMS_EPS)
    return (y * g.astype(jnp.float32)).astype(x.dtype)


def l2norm(x):
    return x * lax.rsqrt(jnp.sum(x * x, axis=-1, keepdims=True) + 1e-6)


def alibi_slopes(n):
    return jnp.exp2(-8.0 * jnp.arange(1, n + 1, dtype=jnp.float32) / n)


def split_cols(a, sizes):
    offs = np.cumsum((0,) + tuple(sizes))
    return [a[..., int(offs[i]):int(offs[i + 1])] for i in range(len(sizes))]


def gather_pages(pool, page_table):
    g = pool[page_table]
    return g.reshape(g.shape[0], g.shape[1] * g.shape[2], *g.shape[3:])


def causal_conv(x, w, buf):
    T = x.shape[1]
    xp = jnp.concatenate([buf.astype(x.dtype), x], axis=1)
    y = sum(w[i] * xp[:, i:i + T] for i in range(CONV_W))
    return y, xp[:, T:]


def map_query_blocks(fn, per_query, q_pos):
    T = q_pos.shape[0]
    nb = T // QBLOCK

    def to_blocks(a):
        return jnp.moveaxis(a.reshape(a.shape[0], nb, QBLOCK, *a.shape[2:]), 1, 0)

    xs = tuple(to_blocks(a) for a in per_query) + (q_pos.reshape(nb, QBLOCK),)
    out = lax.map(lambda args: fn(*args), xs)
    out = jnp.moveaxis(out, 0, 1)
    return out.reshape(out.shape[0], T, *out.shape[3:])


def fox_attend(q, k, v, cq, ck, q_pos, k_pos):
    s = jnp.einsum('bqhd,bkhd->bhqk', q, k, preferred_element_type=jnp.float32) * (HEAD_DIM ** -0.5)
    s = s + (jnp.swapaxes(cq, 1, 2)[..., :, None] - jnp.swapaxes(ck, 1, 2)[..., None, :])
    mask = k_pos[None, :] <= q_pos[:, None]
    s = jnp.where(mask[None, None], s, NEG)
    p = jax.nn.softmax(s, axis=-1)
    return jnp.einsum('bhqk,bkhd->bqhd', p.astype(v.dtype), v)


def fox_mixer(q, k, v, f_logit, b_f, past_kv, past_logf):
    B, T = q.shape[:2]
    logf = jax.nn.log_sigmoid((f_logit + b_f).astype(jnp.float32))
    if past_kv is None:
        p0, k_all, v_all, logf_all = 0, k, v, logf
    else:
        p0 = past_kv.shape[1]
        k_all = jnp.concatenate([past_kv[:, :, 0], k], axis=1)
        v_all = jnp.concatenate([past_kv[:, :, 1], v], axis=1)
        logf_all = jnp.concatenate([past_logf.astype(jnp.float32), logf], axis=1)
    c = jnp.cumsum(logf_all, axis=1)
    k_pos = jnp.arange(k_all.shape[1])
    q_pos = p0 + jnp.arange(T)
    cq = c[:, p0:]
    if past_kv is None:
        o = map_query_blocks(lambda qb, cqb, qp: fox_attend(qb, k_all, v_all, cqb, c, qp, k_pos), (q, cq), q_pos)
    else:
        o = fox_attend(q, k_all, v_all, cq, c, q_pos, k_pos)
    new_kv = jnp.stack([k, v], axis=2)
    return o.reshape(B, T, GROUP_WIDTH), new_kv, logf.astype(q.dtype)


def rglru_mixer(xb, gate, conv_w, conv_b, w_r, b_r, w_i, b_i, lam, conv_buf, h0):
    B, T, _ = xb.shape
    dt = xb.dtype
    xc, new_buf = causal_conv(xb, conv_w, conv_buf)
    xc = xc + conv_b
    blocks = xc.reshape(B, T, LRU_BLOCKS, LRU_BLOCK_DIM)
    r = jax.nn.sigmoid(jnp.einsum('btnd,nde->btne', blocks, w_r).reshape(B, T, LRU_WIDTH) + b_r)
    i_gate = jax.nn.sigmoid(jnp.einsum('btnd,nde->btne', blocks, w_i).reshape(B, T, LRU_WIDTH) + b_i)
    log_a = -LRU_C * r.astype(jnp.float32) * jax.nn.softplus(-lam.astype(jnp.float32))
    a = jnp.exp(log_a)
    u = jnp.sqrt(-jnp.expm1(2.0 * log_a)) * (i_gate * xc).astype(jnp.float32)

    def step(h, au):
        h = au[0] * h + au[1]
        return h, h

    hT, hs = lax.scan(step, h0.astype(jnp.float32), (jnp.swapaxes(a, 0, 1), jnp.swapaxes(u, 0, 1)))
    y = jnp.swapaxes(hs, 0, 1) * jax.nn.gelu(gate.astype(jnp.float32))
    return y.astype(dt), new_buf, hT.astype(dt)


def nsa_mixer(q, kc, vc, ks, vs, kw, vw, gate_logit, w_cmp, past_kv, past_win):
    B, T = q.shape[:2]
    dt = q.dtype
    scale = HEAD_DIM ** -0.5
    slopes = alibi_slopes(H_NSA).reshape(KVH_NSA, NSA_GROUP)
    new_kv = jnp.stack([kc, vc, ks, vs], axis=2)
    new_win = jnp.stack([kw, vw], axis=2)
    if past_kv is None:
        p0, win_p0, rows, win_rows = 0, 0, new_kv, new_win
    else:
        p0 = past_kv.shape[1]
        win_p0 = p0 - past_win.shape[1]
        rows = jnp.concatenate([past_kv, new_kv], axis=1)
        win_rows = jnp.concatenate([past_win, new_win], axis=1)
    q_pos = p0 + jnp.arange(T)
    n_rows = rows.shape[1]
    nblk = -(-n_rows // NSA_BLOCK)
    rows = jnp.pad(rows, ((0, 0), (0, nblk * NSA_BLOCK - n_rows), (0, 0), (0, 0), (0, 0)))
    blocks = rows.reshape(B, nblk, NSA_BLOCK, 4, KVH_NSA, HEAD_DIM)
    qg = q.reshape(B, T, KVH_NSA, NSA_GROUP, HEAD_DIM)
    gates = jax.nn.sigmoid(gate_logit.astype(jnp.float32)).reshape(B, T, KVH_NSA, NSA_GROUP, 3)

    kv_cmp = jnp.einsum('bnicgd,ci->bncgd', blocks[:, :, :, :2], w_cmp)
    blk_ids = jnp.arange(nblk)
    dist_c = q_pos[:, None] - ((blk_ids + 1) * NSA_BLOCK - 1)[None, :]
    valid_c = (dist_c >= 0)[None, :, None, None, :]
    s_c = jnp.einsum('btgqd,bngd->btgqn', qg, kv_cmp[:, :, 0], preferred_element_type=jnp.float32) * scale
    s_c = s_c - slopes[None, None, :, :, None] * dist_c.astype(jnp.float32)[None, :, None, None, :]
    s_c = jnp.where(valid_c, s_c, NEG)
    p_c = jax.nn.softmax(s_c, axis=-1) * valid_c
    o_c = jnp.einsum('btgqn,bngd->btgqd', p_c.astype(dt), kv_cmp[:, :, 1])

    imp = p_c.sum(axis=3)
    cur = q_pos // NSA_BLOCK
    allowed = blk_ids[None, :] <= cur[:, None]
    forced = (blk_ids[None, :] == 0) | (blk_ids[None, :] >= cur[:, None] - 1)
    score = jnp.where(allowed[None, :, None, :], jnp.where(forced[None, :, None, :], FORCED_SCORE, imp), -1.0)
    n_sel = min(NSA_TOPN, nblk)
    sel_score, sel_idx = lax.top_k(score, n_sel)
    sel_valid = sel_score >= 0.0
    k_sel = jnp.moveaxis(blocks[:, :, :, 2], 3, 1)
    v_sel = jnp.moveaxis(blocks[:, :, :, 3], 3, 1)
    wk, wv = win_rows[:, :, 0], win_rows[:, :, 1]

    def sel_and_window(qg_b, g_b, idx_b, valid_b, qp_b, wk_b, wv_b, wpos_b):
        Tq = qg_b.shape[1]
        bi = jnp.arange(B)[:, None, None, None]
        gi = jnp.arange(KVH_NSA)[None, None, :, None]
        kb = k_sel[bi, gi, idx_b]
        vb = v_sel[bi, gi, idx_b]
        kpos = idx_b[..., None] * NSA_BLOCK + jnp.arange(NSA_BLOCK)
        dist_s = qp_b[None, :, None, None, None] - kpos
        mask_s = (dist_s >= 0) & valid_b[..., None]
        s_s = jnp.einsum('btgqd,btgnkd->btgqnk', qg_b, kb, preferred_element_type=jnp.float32) * scale
        s_s = s_s - slopes[None, None, :, :, None, None] * dist_s.astype(jnp.float32)[:, :, :, None]
        s_s = jnp.where(mask_s[:, :, :, None], s_s, NEG)
        p_s = jax.nn.softmax(s_s.reshape(B, Tq, KVH_NSA, NSA_GROUP, n_sel * NSA_BLOCK), axis=-1).reshape(s_s.shape)
        o_s = jnp.einsum('btgqnk,btgnkd->btgqd', p_s.astype(dt), vb)
        dist_w = qp_b[:, None] - wpos_b[None, :]
        mask_w = (dist_w >= 0) & (dist_w < NSA_WINDOW) & (wpos_b >= 0)[None, :]
        s_w = jnp.einsum('btgqd,bkgd->btgqk', qg_b, wk_b, preferred_element_type=jnp.float32) * scale
        s_w = s_w - slopes[None, None, :, :, None] * dist_w.astype(jnp.float32)[None, :, None, None, :]
        s_w = jnp.where(mask_w[None, :, None, None, :], s_w, NEG)
        p_w = jax.nn.softmax(s_w, axis=-1)
        o_w = jnp.einsum('btgqk,bkgd->btgqd', p_w.astype(dt), wv_b)
        return g_b[..., 1:2] * o_s + g_b[..., 2:3] * o_w

    if past_kv is None:
        span = NSA_WINDOW + QBLOCK
        pad_w = ((0, 0), (NSA_WINDOW, 0), (0, 0), (0, 0))
        wk_pad, wv_pad = jnp.pad(wk, pad_w), jnp.pad(wv, pad_w)

        def blk_fn(qg_b, g_b, idx_b, valid_b, qp_b):
            start = qp_b[0]
            wk_b = lax.dynamic_slice_in_dim(wk_pad, start, span, axis=1)
            wv_b = lax.dynamic_slice_in_dim(wv_pad, start, span, axis=1)
            wpos_b = start - NSA_WINDOW + jnp.arange(span)
            return sel_and_window(qg_b, g_b, idx_b, valid_b, qp_b, wk_b, wv_b, wpos_b)

        o_sw = map_query_blocks(blk_fn, (qg, gates, sel_idx, sel_valid), q_pos)
    else:
        wpos = win_p0 + jnp.arange(win_rows.shape[1])
        o_sw = sel_and_window(qg, gates, sel_idx, sel_valid, q_pos, wk, wv, wpos)
    o = gates[..., 0:1] * o_c + o_sw
    return o.reshape(B, T, GROUP_WIDTH).astype(dt), new_kv, win_rows[:, -NSA_WINDOW:]


def gated_delta_rule(q, k, v, beta, log_alpha, S0):
    B, T, H, DK = q.shape
    DV = v.shape[-1]
    C = GDN_CHUNK
    Tp = -(-T // C) * C
    pad = Tp - T

    def prep(a):
        a = jnp.pad(a, ((0, 0), (0, pad)) + ((0, 0),) * (a.ndim - 2))
        a = a.reshape(B, Tp // C, C, *a.shape[2:])
        return jnp.moveaxis(jnp.moveaxis(a, 3, 2), 1, 0)

    qc, kc, vc, bc, lac = (prep(a) for a in (q, k, v, beta, log_alpha))
    g = jnp.cumsum(lac, axis=-1)
    idx = jnp.arange(C)
    incl = idx[:, None] >= idx[None, :]
    strict = idx[:, None] > idx[None, :]
    decay = jnp.exp(jnp.where(incl, g[..., :, None] - g[..., None, :], -jnp.inf))
    kb = kc * bc[..., None]
    A = jnp.where(strict, jnp.einsum('nbhid,nbhjd->nbhij', kb, kc) * decay, 0.0)
    eye = jnp.eye(C, dtype=A.dtype)
    Tm = lax.linalg.triangular_solve(eye + A, jnp.broadcast_to(eye, A.shape), left_side=True, lower=True)
    u_v = Tm @ (vc * bc[..., None])
    w_k = Tm @ (kb * jnp.exp(g)[..., None])
    qk = jnp.einsum('nbhid,nbhjd->nbhij', qc, kc) * decay

    def step(S, xs):
        q_n, k_n, u_n, w_n, qk_n, g_n = xs
        u_new = u_n - w_n @ S
        o_n = (q_n * jnp.exp(g_n)[..., None]) @ S + qk_n @ u_new
        g_last = g_n[..., -1]
        S = S * jnp.exp(g_last)[..., None, None] + jnp.einsum(
            'bhcd,bhce->bhde', k_n * jnp.exp(g_last[..., None] - g_n)[..., None], u_new)
        return S, o_n

    S_T, o = lax.scan(step, S0, (qc, kc, u_v, w_k, qk, g))
    o = jnp.moveaxis(jnp.moveaxis(o, 0, 1), 2, 3).reshape(B, Tp, H, DV)[:, :T]
    return o, S_T


def gdn_mixer(qkv_pre, g_out, a_logit, b_logit, conv_w, A_log, dt_bias, norm_w, conv_buf, S0):
    B, T, _ = qkv_pre.shape
    dt = qkv_pre.dtype
    qkv, new_buf = causal_conv(qkv_pre, conv_w, conv_buf)
    qkv = jax.nn.silu(qkv.astype(jnp.float32))
    q, k, v = (a.reshape(B, T, H_GDN, GDN_DK) for a in jnp.split(qkv, 3, axis=-1))
    q = l2norm(q) * (GDN_DK ** -0.5)
    k = l2norm(k)
    beta = jax.nn.sigmoid(b_logit.astype(jnp.float32))
    log_alpha = -jnp.exp(A_log.astype(jnp.float32)) * jax.nn.softplus(a_logit.astype(jnp.float32) + dt_bias.astype(jnp.float32))
    o, S = gated_delta_rule(q, k, v, beta, log_alpha, S0.astype(jnp.float32))
    o = rmsnorm(o, norm_w) * jax.nn.silu(g_out.astype(jnp.float32).reshape(B, T, H_GDN, GDN_DV))
    return o.reshape(B, T, GROUP_WIDTH).astype(dt), new_buf, S.astype(dt)


def trunk_layer(x, prm, past):
    B, T, _ = x.shape
    h = rmsnorm(x, prm['norm_pre_mix'])
    proj = jnp.einsum('btd,dn->btn', h, prm['w_in'])
    (f_q, f_k, f_v, f_f, l_x, l_g, n_q, n_kc, n_vc, n_ks, n_vs, n_kw, n_vw, n_gate,
     d_qkv, d_g, d_a, d_b) = split_cols(proj, IN_SPLITS)

    def heads(a, n):
        return a.reshape(B, T, n, a.shape[-1] // n)

    if past is None:
        fox_kv0, fox_logf0, nsa_kv0, nsa_win0 = None, None, None, None
        lru_conv0 = jnp.zeros((B, CONV_W - 1, LRU_WIDTH), x.dtype)
        lru_h0 = jnp.zeros((B, LRU_WIDTH), x.dtype)
        gdn_conv0 = jnp.zeros((B, CONV_W - 1, 3 * GROUP_WIDTH), x.dtype)
        gdn_S0 = jnp.zeros((B, H_GDN, GDN_DK, GDN_DV), x.dtype)
    else:
        fox_kv0, fox_logf0 = past['fox_kv'], past['fox_logf']
        nsa_kv0, nsa_win0 = past['nsa_kv'], past['nsa_win']
        lru_conv0, lru_h0 = past['lru_conv'], past['lru_h']
        gdn_conv0, gdn_S0 = past['gdn_conv'], past['gdn_S']

    o_a, fox_kv_new, fox_logf_new = fox_mixer(heads(f_q, H_FOX), heads(f_k, H_FOX), heads(f_v, H_FOX), f_f,
                                              prm['fox_b_f'], fox_kv0, fox_logf0)
    o_b, lru_conv_new, lru_h_new = rglru_mixer(l_x, l_g, prm['lru_conv_w'], prm['lru_conv_b'], prm['lru_w_r'],
                                               prm['lru_b_r'], prm['lru_w_i'], prm['lru_b_i'], prm['lru_lambda'],
                                               lru_conv0, lru_h0)
    o_c, nsa_kv_new, nsa_win_new = nsa_mixer(heads(n_q, H_NSA), heads(n_kc, KVH_NSA), heads(n_vc, KVH_NSA),
                                             heads(n_ks, KVH_NSA), heads(n_vs, KVH_NSA), heads(n_kw, KVH_NSA),
                                             heads(n_vw, KVH_NSA), n_gate, prm['nsa_w_cmp'], nsa_kv0, nsa_win0)
    o_d, gdn_conv_new, gdn_S_new = gdn_mixer(d_qkv, d_g, d_a, d_b, prm['gdn_conv_w'], prm['gdn_A_log'],
                                             prm['gdn_dt_bias'], prm['gdn_norm_w'], gdn_conv0, gdn_S0)
    o = jnp.concatenate([o_a, o_b, o_c, o_d], axis=-1)
    y = jnp.einsum('btm,md->btd', o, prm['w_out'])
    x = x + rmsnorm(y, prm['norm_post_mix'])
    h = rmsnorm(x, prm['norm_pre_mlp'])
    u = jnp.square(jax.nn.relu(jnp.einsum('btd,df->btf', h, prm['w_up'])))
    x = x + rmsnorm(jnp.einsum('btf,fd->btd', u, prm['w_down']), prm['norm_post_mlp'])
    return x, (fox_kv_new, fox_logf_new, nsa_kv_new, nsa_win_new, lru_conv_new, lru_h_new, gdn_conv_new, gdn_S_new)


def setup_inputs(seed: int = 0) -> dict:
    key = jax.random.key(seed)
    ks = jax.random.split(key, 40)
    n_pages = PAST_LEN // PAGE_SIZE
    n_used = DEC_BATCH * n_pages
    n_pool = (5 * n_used) // 4
    win_buf = min(NSA_WINDOW, PAST_LEN)

    def nrm(k, shape, s):
        return s * jax.random.normal(k, shape, jnp.float32)

    def gain(k, n):
        return 1.0 + nrm(k, (DEPTH, n), 0.1)

    a0 = jax.random.uniform(ks[24], (DEPTH, LRU_WIDTH), jnp.float32, 0.9, 0.999)
    s0 = a0 ** (1.0 / LRU_C)
    dt0 = jnp.exp(jax.random.uniform(ks[27], (DEPTH, H_GDN), jnp.float32, math.log(1e-3), math.log(1e-1)))
    return {
        'x_prompt': nrm(ks[0], (BATCH, SEQ, D_MODEL), 1.0),
        'x_sample': nrm(ks[1], (DEC_BATCH, DEC_SEQ, D_MODEL), 1.0),
        'cache_fox_kv': nrm(ks[2], (DEPTH, n_pool, PAGE_SIZE, 2, H_FOX, HEAD_DIM), 1.0),
        'cache_fox_logf': jax.nn.log_sigmoid(3.0 + nrm(ks[3], (DEPTH, n_pool, PAGE_SIZE, H_FOX), 1.0)),
        'cache_nsa_kv': nrm(ks[4], (DEPTH, n_pool, PAGE_SIZE, 4, KVH_NSA, HEAD_DIM), 1.0),
        'cache_nsa_win': nrm(ks[5], (DEPTH, DEC_BATCH, win_buf, 2, KVH_NSA, HEAD_DIM), 1.0),
        'state_rglru_conv': nrm(ks[6], (DEPTH, DEC_BATCH, CONV_W - 1, LRU_WIDTH), 1.0),
        'state_rglru_h': nrm(ks[7], (DEPTH, DEC_BATCH, LRU_WIDTH), 0.5),
        'state_gdn_conv': nrm(ks[8], (DEPTH, DEC_BATCH, CONV_W - 1, 3 * GROUP_WIDTH), 1.0),
        'state_gdn_S': nrm(ks[9], (DEPTH, DEC_BATCH, H_GDN, GDN_DK, GDN_DV), 0.3),
        'page_table': jax.random.permutation(ks[10], n_pool)[:n_used].reshape(DEC_BATCH, n_pages).astype(jnp.int32),
        'norm_pre_mix': gain(ks[11], D_MODEL),
        'norm_post_mix': gain(ks[12], D_MODEL),
        'norm_pre_mlp': gain(ks[13], D_MODEL),
        'norm_post_mlp': gain(ks[14], D_MODEL),
        'w_in': nrm(ks[15], (DEPTH, D_MODEL, N_IN), D_MODEL ** -0.5),
        'w_out': nrm(ks[16], (DEPTH, MIX_WIDTH, D_MODEL), MIX_WIDTH ** -0.5),
        'w_up': nrm(ks[17], (DEPTH, D_MODEL, D_FF), D_MODEL ** -0.5),
        'w_down': nrm(ks[18], (DEPTH, D_FF, D_MODEL), D_FF ** -0.5),
        'fox_b_f': 3.0 + nrm(ks[19], (DEPTH, H_FOX), 0.5),
        'lru_conv_w': nrm(ks[20], (DEPTH, CONV_W, LRU_WIDTH), CONV_W ** -0.5),
        'lru_conv_b': nrm(ks[21], (DEPTH, LRU_WIDTH), 0.01),
        'lru_w_r': nrm(ks[22], (DEPTH, LRU_BLOCKS, LRU_BLOCK_DIM, LRU_BLOCK_DIM), LRU_BLOCK_DIM ** -0.5),
        'lru_b_r': nrm(ks[23], (DEPTH, LRU_WIDTH), 0.01),
        'lru_w_i': nrm(ks[25], (DEPTH, LRU_BLOCKS, LRU_BLOCK_DIM, LRU_BLOCK_DIM), LRU_BLOCK_DIM ** -0.5),
        'lru_b_i': nrm(ks[26], (DEPTH, LRU_WIDTH), 0.01),
        'lru_lambda': jnp.log(s0) - jnp.log1p(-s0),
        'nsa_w_cmp': (1.0 + nrm(ks[28], (DEPTH, 2, NSA_BLOCK), 0.1)) / NSA_BLOCK,
        'gdn_conv_w': nrm(ks[29], (DEPTH, CONV_W, 3 * GROUP_WIDTH), 0.5),
        'gdn_A_log': jnp.log(jax.random.uniform(ks[30], (DEPTH, H_GDN), jnp.float32, 1.0, 16.0)),
        'gdn_dt_bias': dt0 + jnp.log(-jnp.expm1(-dt0)),
        'gdn_norm_w': gain(ks[31], GDN_DV),
    }


def reference(x_prompt, x_sample, cache_fox_kv, cache_fox_logf, cache_nsa_kv, cache_nsa_win,
              state_rglru_conv, state_rglru_h, state_gdn_conv, state_gdn_S, page_table,
              norm_pre_mix, norm_post_mix, norm_pre_mlp, norm_post_mlp, w_in, w_out, w_up, w_down,
              fox_b_f, lru_conv_w, lru_conv_b, lru_w_r, lru_b_r, lru_w_i, lru_b_i, lru_lambda,
              nsa_w_cmp, gdn_conv_w, gdn_A_log, gdn_dt_bias, gdn_norm_w):
    y_prompt, y_sample = x_prompt, x_sample
    prompt_states, sample_states = [], []
    for l in range(DEPTH):
        prm = dict(norm_pre_mix=norm_pre_mix[l], norm_post_mix=norm_post_mix[l],
                   norm_pre_mlp=norm_pre_mlp[l], norm_post_mlp=norm_post_mlp[l],
                   w_in=w_in[l], w_out=w_out[l], w_up=w_up[l], w_down=w_down[l],
                   fox_b_f=fox_b_f[l], lru_conv_w=lru_conv_w[l], lru_conv_b=lru_conv_b[l],
                   lru_w_r=lru_w_r[l], lru_b_r=lru_b_r[l], lru_w_i=lru_w_i[l], lru_b_i=lru_b_i[l],
                   lru_lambda=lru_lambda[l], nsa_w_cmp=nsa_w_cmp[l], gdn_conv_w=gdn_conv_w[l],
                   gdn_A_log=gdn_A_log[l], gdn_dt_bias=gdn_dt_bias[l], gdn_norm_w=gdn_norm_w[l])
        y_prompt, st_p = trunk_layer(y_prompt, prm, None)
        past = dict(fox_kv=gather_pages(cache_fox_kv[l], page_table),
                    fox_logf=gather_pages(cache_fox_logf[l], page_table),
                    nsa_kv=gather_pages(cache_nsa_kv[l], page_table),
                    nsa_win=cache_nsa_win[l],
                    lru_conv=state_rglru_conv[l], lru_h=state_rglru_h[l],
                    gdn_conv=state_gdn_conv[l], gdn_S=state_gdn_S[l])
        y_sample, st_s = trunk_layer(y_sample, prm, past)
        prompt_states.append(st_p)
        sample_states.append(st_s)
    (fox_kv_p, fox_logf_p, nsa_kv_p, nsa_win_p,
     lru_conv_p, lru_h_p, gdn_conv_p, gdn_S_p) = [jnp.stack([st[i] for st in prompt_states]) for i in range(8)]
    (fox_kv_s, fox_logf_s, nsa_kv_s, nsa_win_s,
     lru_conv_s, lru_h_s, gdn_conv_s, gdn_S_s) = [jnp.stack([st[i] for st in sample_states]) for i in range(8)]
    return (y_prompt, y_sample, fox_kv_p, fox_kv_s, fox_logf_p, fox_logf_s, nsa_kv_p, nsa_kv_s,
            nsa_win_p, nsa_win_s, lru_conv_p, lru_conv_s, lru_h_p, lru_h_s, gdn_conv_p, gdn_conv_s,
            gdn_S_p, gdn_S_s)
```

```python
import functools

import jax
import jax.numpy as jnp
from jax import lax
from jax.experimental import pallas as pl
from jax.experimental.pallas import tpu as pltpu

F32 = jnp.float32
BF16 = jnp.bfloat16
HI = lax.Precision.HIGHEST

D_MODEL = 2048
GROUP_WIDTH = 512
HEAD_DIM = 64
N_HEADS = 8
KVH_NSA = 2
NSA_GROUP = 4
NSA_BLOCK = 64
NSA_TOPN = 16
NSA_WINDOW = 512
FORCED_SCORE = 1e4
H_GDN = 4
GDN_DK = 128
GDN_CHUNK = 64
LRU_C = 8.0
CONV_W = 4
D_FF = 4 * D_MODEL
RMS_EPS = 1e-6
NEG = -1e30
ATT_SCALE = HEAD_DIM ** -0.5
PAGE_SIZE = 128

C_FQ, C_FK, C_FV = 0, 512, 1024
C_LX, C_LG = 1536, 2048
C_NQ = 2560
C_NKV = 3072
C_NWIN = 3584
C_SMALL = 3840
C_DQKV = 4096
C_DG = 5632
N_PROJ = 6144
L_FF, L_NG, L_DA, L_DB = 0, 8, 32, 36

VMEM_LIMIT = 56 * 1024 * 1024


def _cparams(sem):
    return pltpu.CompilerParams(dimension_semantics=sem, vmem_limit_bytes=VMEM_LIMIT)


def _pick(n, prefs):
    for p in prefs:
        if n % p == 0:
            return p
    return n


def _rms(xf, g):
    return xf * lax.rsqrt(jnp.mean(xf * xf, axis=-1, keepdims=True) + RMS_EPS) * g


def _softplus(x):
    return jnp.maximum(x, 0.0) + jnp.log1p(jnp.exp(-jnp.abs(x)))


def _log_sigmoid(x):
    return -_softplus(-x)


def _iota(shape, dim):
    return lax.broadcasted_iota(jnp.int32, shape, dim)


def _dot(a, b, **kw):
    return jnp.dot(a, b, preferred_element_type=F32, **kw)


def _dot_nt(a, b):
    return lax.dot_general(a, b, (((1,), (1,)), ((), ())), preferred_element_type=F32)


def _dot_tn(a, b):
    return lax.dot_general(a, b, (((0,), (0,)), ((), ())), preferred_element_type=F32)


def _in_proj_kernel(x_ref, g_ref, w_ref, o_ref, h_ref):
    @pl.when(pl.program_id(1) == 0)
    def _():
        h_ref[...] = _rms(x_ref[...], g_ref[...]).astype(BF16)

    o_ref[...] = _dot(h_ref[...], w_ref[...])


def _in_proj(x, g, w):
    m, d = x.shape
    n = w.shape[1]
    tm = _pick(m, (640, 512, 256, 128, 64, 32, 16, 8))
    tn = _pick(n, (1536, 1024, 768, 512, 256, 128))
    return pl.pallas_call(
        _in_proj_kernel,
        grid=(m // tm, n // tn),
        in_specs=[pl.BlockSpec((tm, d), lambda i, j: (i, 0)),
                  pl.BlockSpec((1, d), lambda i, j: (0, 0)),
                  pl.BlockSpec((d, tn), lambda i, j: (0, j))],
        out_specs=pl.BlockSpec((tm, tn), lambda i, j: (i, j)),
        out_shape=jax.ShapeDtypeStruct((m, n), F32),
        scratch_shapes=[pltpu.VMEM((tm, d), BF16)],
        compiler_params=_cparams(("parallel", "arbitrary")),
        name="in_proj",
    )(x, g, w)


def _out_proj_kernel(oa_ref, ob_ref, oc_ref, od_ref, x_ref, w_ref, g_ref, o_ref):
    y = _dot(oa_ref[...].astype(BF16), w_ref[0:512, :])
    y = y + _dot(ob_ref[...].astype(BF16), w_ref[512:1024, :])
    y = y + _dot(oc_ref[...].astype(BF16), w_ref[1024:1536, :])
    y = y + _dot(od_ref[...].astype(BF16), w_ref[1536:2048, :])
    o_ref[...] = x_ref[...] + _rms(y, g_ref[...])


def _out_proj(oa, ob, oc, od, x, w, g):
    m, d = x.shape
    tm = _pick(m, (640, 512, 256, 128, 64, 32, 16, 8))
    gw = GROUP_WIDTH
    return pl.pallas_call(
        _out_proj_kernel,
        grid=(m // tm,),
        in_specs=[pl.BlockSpec((tm, gw), lambda i: (i, 0))] * 4 + [
            pl.BlockSpec((tm, d), lambda i: (i, 0)),
            pl.BlockSpec((d, d), lambda i: (0, 0)),
            pl.BlockSpec((1, d), lambda i: (0, 0))],
        out_specs=pl.BlockSpec((tm, d), lambda i: (i, 0)),
        out_shape=jax.ShapeDtypeStruct((m, d), F32),
        compiler_params=_cparams(("parallel",)),
        name="out_proj",
    )(oa, ob, oc, od, x, w, g)


def _mlp_kernel(x_ref, g1_ref, wu_ref, wd_ref, g2_ref, o_ref, h_ref, acc_ref):
    f = pl.program_id(1)

    @pl.when(f == 0)
    def _():
        h_ref[...] = _rms(x_ref[...], g1_ref[...]).astype(BF16)
        acc_ref[...] = jnp.zeros_like(acc_ref)

    u = jnp.maximum(_dot(h_ref[...], wu_ref[...]), 0.0)
    acc_ref[...] += _dot((u * u).astype(BF16), wd_ref[...])

    @pl.when(f == pl.num_programs(1) - 1)
    def _():
        o_ref[...] = x_ref[...] + _rms(acc_ref[...], g2_ref[...])


def _mlp(x, g1, wu, wd, g2):
    m, d = x.shape
    ff = wu.shape[1]
    tm = _pick(m, (640, 512, 256, 128, 64, 32, 16, 8))
    tf = _pick(ff, (512, 256, 128))
    return pl.pallas_call(
        _mlp_kernel,
        grid=(m // tm, ff // tf),
        in_specs=[pl.BlockSpec((tm, d), lambda i, f: (i, 0)),
                  pl.BlockSpec((1, d), lambda i, f: (0, 0)),
                  pl.BlockSpec((d, tf), lambda i, f: (0, f)),
                  pl.BlockSpec((tf, d), lambda i, f: (f, 0)),
                  pl.BlockSpec((1, d), lambda i, f: (0, 0))],
        out_specs=pl.BlockSpec((tm, d), lambda i, f: (i, 0)),
        out_shape=jax.ShapeDtypeStruct((m, d), F32),
        scratch_shapes=[pltpu.VMEM((tm, d), BF16), pltpu.VMEM((tm, d), F32)],
        compiler_params=_cparams(("parallel", "arbitrary")),
        name="mlp",
    )(x, g1, wu, wd, g2)


def _fox_gate_kernel(s_ref, bf_ref, logf_ref, c_ref, carry_ref, *, tb):
    @pl.when(pl.program_id(1) == 0)
    def _():
        carry_ref[...] = jnp.zeros_like(carry_ref)

    logf = _log_sigmoid(s_ref[...] + bf_ref[...])
    tri = (_iota((tb, tb), 1) <= _iota((tb, tb), 0)).astype(F32)
    c = _dot(tri, logf, precision=HI) + carry_ref[...]
    logf_ref[...] = logf
    c_ref[...] = c
    carry_ref[...] = c[tb - 1:tb, :]


def _fox_gate(proj, bf_row, b, t):
    tb = _pick(t, (512, 256, 128, 64))
    nt = t // tb
    out = jax.ShapeDtypeStruct((b * t, 128), F32)
    return pl.pallas_call(
        functools.partial(_fox_gate_kernel, tb=tb),
        grid=(b, nt),
        in_specs=[pl.BlockSpec((tb, 128), lambda i, j: (i * nt + j, C_SMALL // 128)),
                  pl.BlockSpec((1, 128), lambda i, j: (0, 0))],
        out_specs=[pl.BlockSpec((tb, 128), lambda i, j: (i * nt + j, 0))] * 2,
        out_shape=[out, out],
        scratch_shapes=[pltpu.VMEM((1, 128), F32)],
        compiler_params=_cparams(("parallel", "arbitrary")),
        name="fox_gate",
    )(proj, bf_row)


def _fox_flash_kernel(q_ref, k_ref, v_ref, cq_ref, ck_ref, o_ref, m_ref, l_ref, acc_ref, *, nt):
    qi = pl.program_id(1)
    ki = pl.program_id(2)

    @pl.when(ki == 0)
    def _():
        m_ref[...] = jnp.full_like(m_ref, NEG)
        l_ref[...] = jnp.zeros_like(l_ref)
        acc_ref[...] = jnp.zeros_like(acc_ref)

    def step(diag):
        q = q_ref[...]
        k = k_ref[...]
        v = v_ref[...]
        cq = cq_ref[...]
        ck = ck_ref[...]
        if diag:
            keep = _iota((nt, nt), 1) <= _iota((nt, nt), 0)
        for h in range(N_HEADS):
            sl = slice(h * HEAD_DIM, (h + 1) * HEAD_DIM)
            s = _dot_nt(q[:, sl].astype(BF16), k[:, sl].astype(BF16)) * ATT_SCALE
            s = s + (cq[:, h:h + 1] - ck[h:h + 1, :])
            if diag:
                s = jnp.where(keep, s, NEG)
            m_prev = m_ref[h]
            m_new = jnp.maximum(m_prev, jnp.max(s, axis=-1, keepdims=True))
            alpha = jnp.exp(m_prev - m_new)
            p = jnp.exp(s - m_new)
            l_ref[h] = alpha * l_ref[h] + jnp.sum(p, axis=-1, keepdims=True)
            acc_ref[h] = alpha * acc_ref[h] + _dot(p.astype(BF16), v[:, sl].astype(BF16))
            m_ref[h] = m_new

    @pl.when(ki < qi)
    def _():
        step(False)

    @pl.when(ki == qi)
    def _():
        step(True)
        o_ref[...] = jnp.concatenate([acc_ref[h] / l_ref[h] for h in range(N_HEADS)], axis=1)


def _fox_flash(proj, c, c_row, b, t):
    nt = _pick(t, (512, 256, 128))
    n = t // nt
    return pl.pallas_call(
        functools.partial(_fox_flash_kernel, nt=nt),
        grid=(b, n, n),
        in_specs=[pl.BlockSpec((nt, 512), lambda i, q, k: (i * n + q, C_FQ // 512)),
                  pl.BlockSpec((nt, 512), lambda i, q, k: (i * n + jnp.minimum(k, q), C_FK // 512)),
                  pl.BlockSpec((nt, 512), lambda i, q, k: (i * n + jnp.minimum(k, q), C_FV // 512)),
                  pl.BlockSpec((nt, 128), lambda i, q, k: (i * n + q, 0)),
                  pl.BlockSpec((None, 8, nt), lambda i, q, k: (i, 0, jnp.minimum(k, q)))],
        out_specs=pl.BlockSpec((nt, 512), lambda i, q, k: (i * n + q, 0)),
        out_shape=jax.ShapeDtypeStruct((b * t, GROUP_WIDTH), F32),
        scratch_shapes=[pltpu.VMEM((N_HEADS, nt, 1), F32), pltpu.VMEM((N_HEADS, nt, 1), F32),
                        pltpu.VMEM((N_HEADS, nt, HEAD_DIM), F32)],
        compiler_params=_cparams(("parallel", "parallel", "arbitrary")),
        name="fox_flash",
    )(proj, proj, proj, c, c_row)


def _lru_gates(xc, wr, br, wi, bi, lam):
    xb = xc.astype(BF16)
    r = jax.nn.sigmoid(_dot(xb, wr) + br)
    ig = jax.nn.sigmoid(_dot(xb, wi) + bi)
    log_a = -LRU_C * r * _softplus(-lam)
    a = jnp.exp(log_a)
    th = jnp.tanh(log_a)
    u = jnp.sqrt(-2.0 * th / (1.0 - th)) * (ig * xc)
    return a, u


def _conv4(w_ref, x0, x1, x2, x3, cols=slice(None)):
    y = 0.0 + w_ref[0:1, cols] * x0
    y = y + w_ref[1:2, cols] * x1
    y = y + w_ref[2:3, cols] * x2
    return y + w_ref[3:4, cols] * x3


def _lru_prompt_kernel(x_ref, g_ref, cw_ref, cb_ref, wr_ref, br_ref, wi_ref, bi_ref, lam_ref,
                       y_ref, hfin_ref, xbuf, a_s, u_s, hs, h_s, *, tb):
    @pl.when(pl.program_id(1) == 0)
    def _():
        xbuf[0:8, :] = jnp.zeros((8, GROUP_WIDTH), F32)
        h_s[...] = jnp.zeros_like(h_s)

    xbuf[8:8 + tb, :] = x_ref[...]
    xc = _conv4(cw_ref, xbuf[5:5 + tb, :], xbuf[6:6 + tb, :], xbuf[7:7 + tb, :], xbuf[8:8 + tb, :])
    xc = xc + cb_ref[...]
    xbuf[0:8, :] = xbuf[tb:tb + 8, :]
    a, u = _lru_gates(xc, wr_ref[...], br_ref[...], wi_ref[...], bi_ref[...], lam_ref[...])
    a_s[...] = a
    u_s[...] = u

    def body(t, h):
        h = a_s[pl.ds(t, 1), :] * h + u_s[pl.ds(t, 1), :]
        hs[pl.ds(t, 1), :] = h
        return h

    h = lax.fori_loop(0, tb, body, h_s[...], unroll=8)
    h_s[...] = h
    y_ref[...] = hs[...] * jax.nn.gelu(g_ref[...])
    hfin_ref[...] = jnp.broadcast_to(h, (8, GROUP_WIDTH))


def _lru_prompt(proj, prm, b, t):
    tb = _pick(t, (512, 256, 128, 64))
    nt = t // tb
    w = GROUP_WIDTH
    full = lambda shape: pl.BlockSpec(shape, lambda i, j: (0,) * len(shape))
    return pl.pallas_call(
        functools.partial(_lru_prompt_kernel, tb=tb),
        grid=(b, nt),
        in_specs=[pl.BlockSpec((tb, w), lambda i, j: (i * nt + j, C_LX // w)),
                  pl.BlockSpec((tb, w), lambda i, j: (i * nt + j, C_LG // w)),
                  full((CONV_W, w)), full((1, w)), full((w, w)), full((1, w)), full((w, w)),
                  full((1, w)), full((1, w))],
        out_specs=[pl.BlockSpec((tb, w), lambda i, j: (i * nt + j, 0)),
                   pl.BlockSpec((None, 8, w), lambda i, j: (i, 0, 0))],
        out_shape=[jax.ShapeDtypeStruct((b * t, w), F32), jax.ShapeDtypeStruct((b, 8, w), F32)],
        scratch_shapes=[pltpu.VMEM((tb + 8, w), F32), pltpu.VMEM((tb, w), F32), pltpu.VMEM((tb, w), F32),
                        pltpu.VMEM((tb, w), F32), pltpu.VMEM((1, w), F32)],
        compiler_params=_cparams(("parallel", "arbitrary")),
        name="lru_prompt",
    )(proj, proj, prm['lru_conv_w'], prm['lru_conv_b'], prm['lru_wr'], prm['lru_b_r'], prm['lru_wi'],
      prm['lru_b_i'], prm['lru_lambda'])


def _lru_step_kernel(x_ref, g_ref, buf_ref, h0_ref, cw_ref, cb_ref, wr_ref, br_ref, wi_ref, bi_ref,
                     lam_ref, y_ref, h_ref):
    xc = _conv4(cw_ref, buf_ref[0], buf_ref[1], buf_ref[2], x_ref[...]) + cb_ref[...]
    a, u = _lru_gates(xc, wr_ref[...], br_ref[...], wi_ref[...], bi_ref[...], lam_ref[...])
    h = a * h0_ref[...] + u
    h_ref[...] = h
    y_ref[...] = h * jax.nn.gelu(g_ref[...])


def _lru_step(x, g, buf, h0, prm):
    n, w = x.shape
    out = jax.ShapeDtypeStruct((n, w), F32)
    return pl.pallas_call(
        _lru_step_kernel, out_shape=[out, out], name="lru_step",
        compiler_params=pltpu.CompilerParams(vmem_limit_bytes=VMEM_LIMIT),
    )(x, g, buf, h0, prm['lru_conv_w'], prm['lru_conv_b'], prm['lru_wr'], prm['lru_b_r'], prm['lru_wi'],
      prm['lru_b_i'], prm['lru_lambda'])


def _nsa_cmp_kernel(kv_ref, w_ref, o_ref, *, tb):
    x = kv_ref[...].reshape(tb // NSA_BLOCK, NSA_BLOCK, 256)
    o_ref[...] = jnp.sum(x * w_ref[...][None], axis=1)


def _nsa_cmp(proj, wcmp, b, t):
    tb = _pick(t, (512,))
    nt = t // tb
    nb = tb // NSA_BLOCK
    return pl.pallas_call(
        functools.partial(_nsa_cmp_kernel, tb=tb),
        grid=(b, nt),
        in_specs=[pl.BlockSpec((tb, 256), lambda i, j: (i * nt + j, C_NKV // 256)),
                  pl.BlockSpec((NSA_BLOCK, 256), lambda i, j: (0, 0))],
        out_specs=pl.BlockSpec((None, nb, 256), lambda i, j: (i, j, 0)),
        out_shape=jax.ShapeDtypeStruct((b, t // NSA_BLOCK, 256), F32),
        compiler_params=_cparams(("parallel", "parallel")),
        name="nsa_cmp",
    )(proj, wcmp)


def _slope(h):
    return float(2.0 ** (-8.0 * (h + 1) / N_HEADS))


def _online_update(s, v_bf, m_ref, l_ref, acc_ref, g, rows):
    m_prev = m_ref[g, rows]
    m_new = jnp.maximum(m_prev, jnp.max(s, axis=-1, keepdims=True))
    alpha = jnp.exp(m_prev - m_new)
    p = jnp.exp(s - m_new)
    l_ref[g, rows] = alpha * l_ref[g, rows] + jnp.sum(p, axis=-1, keepdims=True)
    acc_ref[g, rows] = alpha * acc_ref[g, rows] + _dot(p.astype(BF16), v_bf)
    m_ref[g, rows] = m_new


def _nsa_prompt_kernel(q_ref, ks_ref, vs_ref, kw_ref, vw_ref, cmp_ref, sm_ref, o_ref,
                       qs, oc, sel, m_s, l_s, acc_s, m_w, l_w, acc_w, *, nt, nb, wt):
    qi = pl.program_id(1)
    ki = pl.program_id(2)
    q0 = qi * nt
    k0 = ki * nt
    n_sel = min(NSA_TOPN, nb)

    @pl.when(ki == 0)
    def _():
        q = q_ref[...]
        cmp = cmp_ref[...]
        tpos = q0 + _iota((nt, nb), 0)
        blk = _iota((nt, nb), 1)
        dist_c = tpos - ((blk + 1) * NSA_BLOCK - 1)
        valid1 = dist_c >= 0
        dist_cf = dist_c.astype(F32)
        cur = tpos // NSA_BLOCK
        allowed = blk <= cur
        forced = (blk == 0) | (blk >= cur - 1)
        for g in range(KVH_NSA):
            qg = jnp.concatenate(
                [q[:, (NSA_GROUP * g + i) * HEAD_DIM:(NSA_GROUP * g + i + 1) * HEAD_DIM]
                 for i in range(NSA_GROUP)], axis=0).astype(BF16)
            qs[g] = qg
            kc = cmp[:, g * HEAD_DIM:(g + 1) * HEAD_DIM].astype(BF16)
            vc = cmp[:, 128 + g * HEAD_DIM:128 + (g + 1) * HEAD_DIM].astype(BF16)
            s_all = _dot_nt(qg, kc) * ATT_SCALE
            imp = None
            for i in range(NSA_GROUP):
                rows = slice(i * nt, (i + 1) * nt)
                s = s_all[rows] - _slope(NSA_GROUP * g + i) * dist_cf
                s = jnp.where(valid1, s, NEG)
                m = jnp.max(s, axis=-1, keepdims=True)
                e = jnp.where(valid1, jnp.exp(s - m), 0.0)
                l = jnp.sum(e, axis=-1, keepdims=True)
                p = e / jnp.where(l > 0.0, l, 1.0)
                oc[g, rows] = _dot(p.astype(BF16), vc)
                imp = p if imp is None else imp + p
            score = jnp.where(allowed, jnp.where(forced, FORCED_SCORE, imp), -1.0)
            cnt = jnp.zeros((nt, nb), F32)
            for mm in range(nb):
                col = score[:, mm:mm + 1]
                beats = (col > score) | ((col == score) & (blk > mm))
                cnt = cnt + jnp.where(beats, 1.0, 0.0)
            sel[g] = jnp.where((cnt < n_sel) & (score >= 0.0), 1.0, 0.0)
        for ref in (m_s, m_w):
            ref[...] = jnp.full_like(ref, NEG)
        for ref in (l_s, acc_s, l_w, acc_w):
            ref[...] = jnp.zeros_like(ref)

    def dist_tile():
        return (q0 + _iota((nt, nt), 0)) - (k0 + _iota((nt, nt), 1))

    @pl.when(ki <= qi)
    def _():
        dist = dist_tile()
        distf = dist.astype(F32)
        onehot = ((k0 + _iota((nb, nt), 1)) // NSA_BLOCK == _iota((nb, nt), 0)).astype(BF16)
        ks = ks_ref[...]
        vs = vs_ref[...]
        for g in range(KVH_NSA):
            picked = _dot(sel[g].astype(BF16), onehot) > 0.5
            ok = picked & (dist >= 0)
            s_all = _dot_nt(qs[g], ks[:, g * HEAD_DIM:(g + 1) * HEAD_DIM].astype(BF16)) * ATT_SCALE
            v_bf = vs[:, g * HEAD_DIM:(g + 1) * HEAD_DIM].astype(BF16)
            for i in range(NSA_GROUP):
                rows = slice(i * nt, (i + 1) * nt)
                s = jnp.where(ok, s_all[rows] - _slope(NSA_GROUP * g + i) * distf, NEG)
                _online_update(s, v_bf, m_s, l_s, acc_s, g, rows)

    @pl.when((ki <= qi) & (ki >= qi - wt))
    def _():
        dist = dist_tile()
        distf = dist.astype(F32)
        ok = (dist >= 0) & (dist < NSA_WINDOW)
        kw = kw_ref[...]
        vw = vw_ref[...]
        for g in range(KVH_NSA):
            s_all = _dot_nt(qs[g], kw[:, g * HEAD_DIM:(g + 1) * HEAD_DIM].astype(BF16)) * ATT_SCALE
            v_bf = vw[:, g * HEAD_DIM:(g + 1) * HEAD_DIM].astype(BF16)
            for i in range(NSA_GROUP):
                rows = slice(i * nt, (i + 1) * nt)
                s = jnp.where(ok, s_all[rows] - _slope(NSA_GROUP * g + i) * distf, NEG)
                _online_update(s, v_bf, m_w, l_w, acc_w, g, rows)

    @pl.when(ki == qi)
    def _():
        gates = jax.nn.sigmoid(sm_ref[...])
        outs = []
        for g in range(KVH_NSA):
            for i in range(NSA_GROUP):
                rows = slice(i * nt, (i + 1) * nt)
                c = L_NG + 3 * (NSA_GROUP * g + i)
                o = gates[:, c:c + 1] * oc[g, rows]
                o = o + (gates[:, c + 1:c + 2] * (acc_s[g, rows] / l_s[g, rows])
                         + gates[:, c + 2:c + 3] * (acc_w[g, rows] / l_w[g, rows]))
                outs.append(o)
        o_ref[...] = jnp.concatenate(outs, axis=1)


def _nsa_prompt(proj, cmp, b, t):
    nt = _pick(t, (256, 128))
    n = t // nt
    nb = t // NSA_BLOCK
    wt = NSA_WINDOW // nt
    kidx = lambda i, q, k: i * n + jnp.minimum(k, q)
    widx = lambda i, q, k: i * n + jnp.clip(k, jnp.maximum(q - wt, 0), q)
    rows4 = NSA_GROUP * nt
    return pl.pallas_call(
        functools.partial(_nsa_prompt_kernel, nt=nt, nb=nb, wt=wt),
        grid=(b, n, n),
        in_specs=[pl.BlockSpec((nt, 512), lambda i, q, k: (i * n + q, C_NQ // 512)),
                  pl.BlockSpec((nt, 128), lambda i, q, k: (kidx(i, q, k), C_NKV // 128 + 2)),
                  pl.BlockSpec((nt, 128), lambda i, q, k: (kidx(i, q, k), C_NKV // 128 + 3)),
                  pl.BlockSpec((nt, 128), lambda i, q, k: (widx(i, q, k), C_NWIN // 128)),
                  pl.BlockSpec((nt, 128), lambda i, q, k: (widx(i, q, k), C_NWIN // 128 + 1)),
                  pl.BlockSpec((None, nb, 256), lambda i, q, k: (i, 0, 0)),
                  pl.BlockSpec((nt, 128), lambda i, q, k: (i * n + q, C_SMALL // 128))],
        out_specs=pl.BlockSpec((nt, 512), lambda i, q, k: (i * n + q, 0)),
        out_shape=jax.ShapeDtypeStruct((b * t, GROUP_WIDTH), F32),
        scratch_shapes=[pltpu.VMEM((KVH_NSA, rows4, HEAD_DIM), BF16),
                        pltpu.VMEM((KVH_NSA, rows4, HEAD_DIM), F32),
                        pltpu.VMEM((KVH_NSA, nt, nb), F32),
                        pltpu.VMEM((KVH_NSA, rows4, 1), F32), pltpu.VMEM((KVH_NSA, rows4, 1), F32),
                        pltpu.VMEM((KVH_NSA, rows4, HEAD_DIM), F32),
                        pltpu.VMEM((KVH_NSA, rows4, 1), F32), pltpu.VMEM((KVH_NSA, rows4, 1), F32),
                        pltpu.VMEM((KVH_NSA, rows4, HEAD_DIM), F32)],
        compiler_params=_cparams(("parallel", "parallel", "arbitrary")),
        name="nsa_prompt",
    )(proj, proj, proj, proj, proj, cmp, proj)


def _col_to_row(col, eye):
    return jnp.sum(jnp.where(eye, col, 0.0), axis=0, keepdims=True)


def _row_to_col(row, eye):
    return jnp.sum(jnp.where(eye, row, 0.0), axis=1, keepdims=True)


def _gdn_qkv(act_q, act_k, h):
    sl = slice(h * GDN_DK, (h + 1) * GDN_DK)
    q = act_q[:, sl]
    k = act_k[:, sl]
    q = q * lax.rsqrt(jnp.sum(q * q, axis=-1, keepdims=True) + 1e-6) * (GDN_DK ** -0.5)
    k = k * lax.rsqrt(jnp.sum(k * k, axis=-1, keepdims=True) + 1e-6)
    return q, k


def _gdn_out(o, nw, gate):
    return _rms(o, nw) * (gate * jax.nn.sigmoid(gate))


def _silu(x):
    return x * jax.nn.sigmoid(x)


def _gdn_prompt_kernel(q_ref, k_ref, v_ref, g_ref, sm_ref, cw_ref, al_ref, dt_ref, nw_ref,
                       o_ref, s_ref, xbuf):
    c = GDN_CHUNK

    @pl.when(pl.program_id(1) == 0)
    def _():
        xbuf[:, 0:8, :] = jnp.zeros((3, 8, GROUP_WIDTH), F32)
        s_ref[...] = jnp.zeros_like(s_ref)

    acts = []
    for seg, ref in enumerate((q_ref, k_ref, v_ref)):
        xbuf[seg, 8:8 + c, :] = ref[...]
        cols = slice(seg * GROUP_WIDTH, (seg + 1) * GROUP_WIDTH)
        y = _conv4(cw_ref, xbuf[seg, 5:5 + c, :], xbuf[seg, 6:6 + c, :], xbuf[seg, 7:7 + c, :],
                   xbuf[seg, 8:8 + c, :], cols)
        xbuf[seg, 0:8, :] = xbuf[seg, c:c + 8, :]
        acts.append(_silu(y))
    sm = sm_ref[...]
    beta_all = jax.nn.sigmoid(sm)
    la_all = -jnp.exp(al_ref[...]) * _softplus(sm + dt_ref[...])
    ri = _iota((c, c), 0)
    ci = _iota((c, c), 1)
    incl = ri >= ci
    strict = ri > ci
    eye = ri == ci
    eye_f = eye.astype(F32)
    g_all = _dot(incl.astype(F32), la_all, precision=HI)
    gate = g_ref[...]
    nw = nw_ref[...]
    for h in range(H_GDN):
        sl = slice(h * GDN_DK, (h + 1) * GDN_DK)
        q, k = _gdn_qkv(acts[0], acts[1], h)
        v = acts[2][:, sl]
        beta = beta_all[:, L_DB + h:L_DB + h + 1]
        g = g_all[:, L_DA + h:L_DA + h + 1]
        g_row = _col_to_row(g, eye)
        decay = jnp.exp(jnp.where(incl, g - g_row, -jnp.inf))
        kb = k * beta
        a = jnp.where(strict, _dot_nt(kb, k) * decay, 0.0)
        tm = eye_f - a
        pw = _dot(a, a, precision=HI)
        for it in range(5):
            tm = tm + _dot(tm, pw, precision=HI)
            if it < 4:
                pw = _dot(pw, pw, precision=HI)
        eg = jnp.exp(g)
        u = _dot(tm, v * beta)
        w = _dot(tm, kb * eg)
        qk = _dot_nt(q, k) * decay
        s0 = s_ref[h]
        u_new = u - _dot(w, s0)
        o = _dot(q * eg, s0) + _dot(qk, u_new)
        g_last = g[c - 1:c, :]
        s_ref[h] = s0 * jnp.exp(g_last) + _dot_tn(k * jnp.exp(g_last - g), u_new)
        o_ref[:, sl] = _gdn_out(o, nw, gate[:, sl])


def _gdn_prompt(proj, prm, b, t):
    c = GDN_CHUNK
    nc = t // c
    w = GROUP_WIDTH
    full = lambda shape: pl.BlockSpec(shape, lambda i, j: (0,) * len(shape))
    seg = lambda col: pl.BlockSpec((c, w), lambda i, j: (i * nc + j, col // w))
    return pl.pallas_call(
        _gdn_prompt_kernel,
        grid=(b, nc),
        in_specs=[seg(C_DQKV), seg(C_DQKV + w), seg(C_DQKV + 2 * w), seg(C_DG),
                  pl.BlockSpec((c, 128), lambda i, j: (i * nc + j, C_SMALL // 128)),
                  full((CONV_W, 3 * w)), full((1, 128)), full((1, 128)), full((1, GDN_DK))],
        out_specs=[pl.BlockSpec((c, w), lambda i, j: (i * nc + j, 0)),
                   pl.BlockSpec((None, H_GDN, GDN_DK, GDN_DK), lambda i, j: (i, 0, 0, 0))],
        out_shape=[jax.ShapeDtypeStruct((b * t, w), F32),
                   jax.ShapeDtypeStruct((b, H_GDN, GDN_DK, GDN_DK), F32)],
        scratch_shapes=[pltpu.VMEM((3, c + 8, w), F32)],
        compiler_params=_cparams(("parallel", "arbitrary")),
        name="gdn_prompt",
    )(proj, proj, proj, proj, proj, prm['gdn_conv_w'], prm['gdn_alog_row'], prm['gdn_dt_row'],
      prm['gdn_norm_w'])


def _gdn_step_kernel(x_ref, g_ref, sm_ref, buf_ref, s0_ref, cw_ref, al_ref, dt_ref, nw_ref,
                     o_ref, s_ref):
    buf = buf_ref[...]
    act = _silu(_conv4(cw_ref, buf[0:1], buf[1:2], buf[2:3], x_ref[...]))
    sm = sm_ref[...]
    beta_all = jax.nn.sigmoid(sm)
    alpha_all = jnp.exp(-jnp.exp(al_ref[...]) * _softplus(sm + dt_ref[...]))
    eye = _iota((GDN_DK, GDN_DK), 0) == _iota((GDN_DK, GDN_DK), 1)
    w = GROUP_WIDTH
    gate = g_ref[...]
    outs = []
    for h in range(H_GDN):
        sl = slice(h * GDN_DK, (h + 1) * GDN_DK)
        q, k = _gdn_qkv(act[:, 0:w], act[:, w:2 * w], h)
        v = act[:, 2 * w + h * GDN_DK:2 * w + (h + 1) * GDN_DK]
        beta = beta_all[:, L_DB + h:L_DB + h + 1]
        alpha = alpha_all[:, L_DA + h:L_DA + h + 1]
        k_col = _row_to_col(k, eye)
        q_col = _row_to_col(q, eye)
        s0 = s0_ref[h]
        ks = jnp.sum(s0 * k_col, axis=0, keepdims=True)
        u = beta * v - (beta * alpha) * ks
        s1 = alpha * s0 + k_col * u
        s_ref[h] = s1
        o = jnp.sum(s1 * q_col, axis=0, keepdims=True)
        outs.append(_gdn_out(o, nw_ref[...], gate[:, sl]))
    o_ref[...] = jnp.concatenate(outs, axis=1)


def _gdn_step(x, gate, small, conv_buf, s0, prm):
    n = x.shape[0]
    w = GROUP_WIDTH
    full = lambda shape: pl.BlockSpec(shape, lambda i: (0,) * len(shape))
    return pl.pallas_call(
        _gdn_step_kernel,
        grid=(n,),
        in_specs=[pl.BlockSpec((None, 1, 3 * w), lambda i: (i, 0, 0)),
                  pl.BlockSpec((None, 1, w), lambda i: (i, 0, 0)),
                  pl.BlockSpec((None, 1, 128), lambda i: (i, 0, 0)),
                  pl.BlockSpec((None, CONV_W - 1, 3 * w), lambda i: (i, 0, 0)),
                  pl.BlockSpec((None, H_GDN, GDN_DK, GDN_DK), lambda i: (i, 0, 0, 0)),
                  full((CONV_W, 3 * w)), full((1, 128)), full((1, 128)), full((1, GDN_DK))],
        out_specs=[pl.BlockSpec((None, 1, w), lambda i: (i, 0, 0)),
                   pl.BlockSpec((None, H_GDN, GDN_DK, GDN_DK), lambda i: (i, 0, 0, 0))],
        out_shape=[jax.ShapeDtypeStruct((n, 1, w), F32),
                   jax.ShapeDtypeStruct((n, H_GDN, GDN_DK, GDN_DK), F32)],
        compiler_params=_cparams(("parallel",)),
        name="gdn_step",
    )(x, gate, small, conv_buf, s0, prm['gdn_conv_w'], prm['gdn_alog_row'], prm['gdn_dt_row'],
      prm['gdn_norm_w'])


def _dot_row(row, mat, **kw):
    return _dot(jnp.broadcast_to(row, (8, row.shape[1])), mat, **kw)[0:1]


def _fox_step_kernel(pt_ref, *refs, n_pages):
    kv = refs[:n_pages]
    lf = refs[n_pages:2 * n_pages]
    x_ref, sm_ref, bf_ref, o_ref, lf_out_ref = refs[2 * n_pages:]
    w = GROUP_WIDTH
    pg = PAGE_SIZE
    x = x_ref[...]
    q, kn, vn = x[:, 0:w], x[:, w:2 * w], x[:, 2 * w:3 * w]
    logf_new = _log_sigmoid(sm_ref[...] + bf_ref[...])
    lf_out_ref[...] = logf_new
    head_cols = (_iota((w, N_HEADS), 0) // HEAD_DIM == _iota((w, N_HEADS), 1)).astype(BF16)
    head_rows = _iota((N_HEADS, w), 1) // HEAD_DIM == _iota((N_HEADS, w), 0)
    eye = _iota((w, w), 0) == _iota((w, w), 1)
    qblk = _dot(jnp.where(eye, q, 0.0).astype(BF16), head_cols).astype(BF16)
    tri = (_iota((pg, pg), 1) <= _iota((pg, pg), 0)).astype(F32)
    carry = jnp.zeros((1, N_HEADS), F32)
    cs = []
    for p in range(n_pages):
        c = _dot(tri, lf[p][...], precision=HI) + carry
        carry = c[pg - 1:pg, :]
        cs.append(c)
    ctot = carry + logf_new[:, 0:N_HEADS]
    s_n = _dot_row(kn.astype(BF16), qblk) * ATT_SCALE
    m = s_n
    ss = []
    for p in range(n_pages):
        s = _dot(kv[p][:, 0:w].astype(BF16), qblk) * ATT_SCALE + (ctot - cs[p])
        ss.append(s)
        m = jnp.maximum(m, jnp.max(s, axis=0, keepdims=True))
    e_n = jnp.exp(s_n - m)
    l = e_n
    acc = jnp.zeros((8, w), F32)
    hr_bf = head_rows.astype(BF16)
    for p in range(n_pages):
        e = jnp.exp(ss[p] - m)
        l = l + jnp.sum(e, axis=0, keepdims=True)
        pe = _dot(e.astype(BF16), hr_bf)
        acc = acc + jnp.sum((pe * kv[p][:, w:2 * w]).reshape(pg // 8, 8, w), axis=0)
    tot = jnp.sum(acc, axis=0, keepdims=True) + _dot_row(e_n.astype(BF16), hr_bf) * vn
    o_ref[...] = tot * _dot_row(1.0 / l, head_rows.astype(F32), precision=HI)


def _fox_step(pt, kv_pool, lf_pool, x, small, bf_row):
    n, n_pages = pt.shape
    w = GROUP_WIDTH
    page = lambda i, width: pl.BlockSpec((None, PAGE_SIZE, width), lambda b, t: (t[b, i], 0, 0))
    in_specs = ([page(i, 2 * w) for i in range(n_pages)] + [page(i, N_HEADS) for i in range(n_pages)] +
                [pl.BlockSpec((None, 1, 3 * w), lambda b, t: (b, 0, 0)),
                 pl.BlockSpec((None, 1, 128), lambda b, t: (b, 0, 0)),
                 pl.BlockSpec((1, 128), lambda b, t: (0, 0))])
    return pl.pallas_call(
        functools.partial(_fox_step_kernel, n_pages=n_pages),
        grid_spec=pltpu.PrefetchScalarGridSpec(
            num_scalar_prefetch=1, grid=(n,), in_specs=in_specs,
            out_specs=[pl.BlockSpec((None, 1, w), lambda b, t: (b, 0, 0)),
                       pl.BlockSpec((None, 1, 128), lambda b, t: (b, 0, 0))]),
        out_shape=[jax.ShapeDtypeStruct((n, 1, w), F32), jax.ShapeDtypeStruct((n, 1, 128), F32)],
        compiler_params=_cparams(("parallel",)),
        name="fox_step",
    )(pt, *([kv_pool] * n_pages), *([lf_pool] * n_pages), x, small, bf_row)


def _softmax_cols(scores):
    m = None
    for s in scores:
        mx = jnp.max(s, axis=0, keepdims=True)
        m = mx if m is None else jnp.maximum(m, mx)
    es = [jnp.exp(s - m) for s in scores]
    l = None
    for e in es:
        sm = jnp.sum(e, axis=0, keepdims=True)
        l = sm if l is None else l + sm
    return es, l


def _nsa_step_kernel(pt_ref, *refs, n_pages):
    pgs = refs[:n_pages]
    win_ref, q_ref, kv_ref, wn_ref, sm_ref, wc_ref, o_ref = refs[n_pages:]
    pg = PAGE_SIZE
    w = GROUP_WIDTH
    kvw = KVH_NSA * HEAD_DIM
    past = n_pages * pg
    nbp = past // NSA_BLOCK
    q = q_ref[...]
    r = _iota((kvw, w), 0)
    c = _iota((kvw, w), 1)
    fold = (c % HEAD_DIM == r % HEAD_DIM) & (c // (NSA_GROUP * HEAD_DIM) == r // HEAD_DIM)
    head_cols = (_iota((w, N_HEADS), 0) // HEAD_DIM == _iota((w, N_HEADS), 1)).astype(BF16)
    qn = _dot(jnp.where(fold, q, 0.0).astype(BF16), head_cols).astype(BF16)
    hl = _iota((1, N_HEADS), 1)
    slopes = jnp.exp2(-(hl + 1).astype(F32) * (8.0 / N_HEADS))
    spread = [_iota((N_HEADS, kvw), 0) == NSA_GROUP * (_iota((N_HEADS, kvw), 1) // HEAD_DIM) + i
              for i in range(NSA_GROUP)]
    spread_bf = [s.astype(BF16) for s in spread]
    spread_f = [s.astype(F32) for s in spread]

    def weighted_rows(es, vals):
        out = []
        for i in range(NSA_GROUP):
            tot = None
            for e, v in zip(es, vals):
                if e.shape[0] == 1:
                    part = _dot_row(e.astype(BF16), spread_bf[i]) * v
                else:
                    part = jnp.sum(_dot(e.astype(BF16), spread_bf[i]) * v, axis=0, keepdims=True)
                tot = part if tot is None else tot + part
            out.append(tot)
        return out

    wc = wc_ref[...]
    kc_rows, vc_rows = [], []
    for p in range(n_pages):
        rows = pgs[p][:, 0:2 * kvw]
        for half in range(pg // NSA_BLOCK):
            cm = jnp.sum(rows[half * NSA_BLOCK:(half + 1) * NSA_BLOCK] * wc, axis=0, keepdims=True)
            kc_rows.append(cm[:, 0:kvw])
            vc_rows.append(cm[:, kvw:2 * kvw])
    kcmp = jnp.concatenate(kc_rows, axis=0)
    vcmp = jnp.concatenate(vc_rows, axis=0)
    blk = _iota((nbp, 1), 0)
    dist_c = (past - ((blk + 1) * NSA_BLOCK - 1)).astype(F32)
    s_c = _dot(kcmp.astype(BF16), qn) * ATT_SCALE - slopes * dist_c
    (e_c,), l_c = _softmax_cols([s_c])
    p_c = e_c / l_c
    acc_c = weighted_rows([p_c], [vcmp])

    cur = past // NSA_BLOCK
    forced = (blk == 0) | (blk >= cur - 1)
    eye_b = _iota((nbp, nbp), 0) == _iota((nbp, nbp), 1)
    lower = _iota((nbp, nbp), 1) < _iota((nbp, nbp), 0)
    n_sel = min(NSA_TOPN, nbp + 1)
    sels = []
    for g in range(KVH_NSA):
        imp = p_c[:, NSA_GROUP * g:NSA_GROUP * g + 1]
        for i in range(1, NSA_GROUP):
            imp = imp + p_c[:, NSA_GROUP * g + i:NSA_GROUP * g + i + 1]
        score = jnp.where(forced, FORCED_SCORE, imp)
        score_row = _col_to_row(score, eye_b)
        beats = (score_row > score) | ((score_row == score) & lower)
        cnt = jnp.sum(jnp.where(beats, 1.0, 0.0), axis=1, keepdims=True)
        cnt = cnt + jnp.where(score < FORCED_SCORE, 1.0, 0.0)
        sels.append(jnp.where(cnt < n_sel, 1.0, 0.0))
    sel8 = jnp.where(hl < NSA_GROUP, sels[0], sels[1])

    kv_new = kv_ref[...]
    first_half = _iota((pg, 1), 0) < NSA_BLOCK
    scores, vals = [], []
    for p in range(n_pages):
        dist = (past - (p * pg + _iota((pg, 1), 0))).astype(F32)
        s = _dot(pgs[p][:, 2 * kvw:3 * kvw].astype(BF16), qn) * ATT_SCALE - slopes * dist
        chosen = jnp.where(first_half, sel8[2 * p:2 * p + 1, :], sel8[2 * p + 1:2 * p + 2, :])
        scores.append(jnp.where(chosen > 0.5, s, NEG))
        vals.append(pgs[p][:, 3 * kvw:4 * kvw])
    scores.append(_dot_row(kv_new[:, 2 * kvw:3 * kvw].astype(BF16), qn) * ATT_SCALE)
    vals.append(kv_new[:, 3 * kvw:4 * kvw])
    e_s, l_s = _softmax_cols(scores)
    acc_s = weighted_rows(e_s, vals)

    nw = win_ref.shape[0]
    wrow = _iota((nw, 1), 0)
    dist_w = nw - wrow
    s_w = _dot(win_ref[:, 0:kvw].astype(BF16), qn) * ATT_SCALE - slopes * dist_w.astype(F32)
    s_w = jnp.where((dist_w < NSA_WINDOW) & (past - dist_w >= 0), s_w, NEG)
    wn = wn_ref[...]
    s_wn = _dot_row(wn[:, 0:kvw].astype(BF16), qn) * ATT_SCALE
    e_w, l_w = _softmax_cols([s_w, s_wn])
    acc_w = weighted_rows(e_w, [win_ref[:, kvw:2 * kvw], wn[:, kvw:2 * kvw]])

    gates = jax.nn.sigmoid(sm_ref[...])
    lane = _iota((128, N_HEADS), 0)
    head = _iota((128, N_HEADS), 1)
    coef = []
    for k in range(3):
        pick = (lane == L_NG + 3 * head + k).astype(F32)
        coef.append(_dot_row(gates, pick, precision=HI))
    coef[1] = coef[1] / l_s
    coef[2] = coef[2] / l_w
    tots = []
    for i in range(NSA_GROUP):
        t = _dot_row(coef[0], spread_f[i], precision=HI) * acc_c[i]
        t = t + (_dot_row(coef[1], spread_f[i], precision=HI) * acc_s[i]
                 + _dot_row(coef[2], spread_f[i], precision=HI) * acc_w[i])
        tots.append(t)
    o_ref[...] = jnp.concatenate([t[:, 0:HEAD_DIM] for t in tots] + [t[:, HEAD_DIM:kvw] for t in tots], axis=1)


def _nsa_step(pt, kv_pool, win, q, kv_new, win_new, small, wcmp):
    n, n_pages = pt.shape
    w = GROUP_WIDTH
    nw = win.shape[1]
    row = lambda width: pl.BlockSpec((None, 1, width), lambda b, t: (b, 0, 0))
    in_specs = ([pl.BlockSpec((None, PAGE_SIZE, w), (lambda i: (lambda b, t: (t[b, i], 0, 0)))(i))
                 for i in range(n_pages)] +
                [pl.BlockSpec((None, nw, 256), lambda b, t: (b, 0, 0)),
                 row(w), row(w), row(256), row(128),
                 pl.BlockSpec((NSA_BLOCK, 256), lambda b, t: (0, 0))])
    return pl.pallas_call(
        functools.partial(_nsa_step_kernel, n_pages=n_pages),
        grid_spec=pltpu.PrefetchScalarGridSpec(
            num_scalar_prefetch=1, grid=(n,), in_specs=in_specs,
            out_specs=pl.BlockSpec((None, 1, w), lambda b, t: (b, 0, 0))),
        out_shape=jax.ShapeDtypeStruct((n, 1, w), F32),
        compiler_params=_cparams(("parallel",)),
        name="nsa_step",
    )(pt, *([kv_pool] * n_pages), win, q, kv_new, win_new, small, wcmp)


_O_FF, _O_LX, _O_NQ, _O_NKC, _O_NG, _O_DQKV, _O_DG, _O_DA, _O_END = (
    1536, 1544, 2568, 3080, 3848, 3872, 5408, 5920, 5928)


def _pack_w_in(w):
    d = w.shape[0]
    zeros = lambda n: jnp.zeros((d, n), w.dtype)
    parts = [w[:, 0:_O_FF], w[:, _O_LX:_O_NQ], w[:, _O_NQ:_O_NKC], w[:, _O_NKC:_O_NG],
             w[:, _O_FF:_O_LX], w[:, _O_NG:_O_DQKV], w[:, _O_DA:_O_END], zeros(128 - 40), zeros(128),
             w[:, _O_DQKV:_O_DG], w[:, _O_DG:_O_DA]]
    out = jnp.concatenate(parts, axis=1)
    assert out.shape[1] == N_PROJ
    return out.astype(BF16)


def _block_diag(w):
    n, d, e = w.shape
    eye = jnp.eye(n, dtype=w.dtype)
    return (eye[:, None, :, None] * w[:, :, None, :]).reshape(n * d, n * e)


def _lane_row(vals, start):
    return jnp.zeros((1, 128), F32).at[0, start:start + vals.shape[0]].set(vals)


def _layer_params(l, p):
    row = lambda a: a[l][None, :]
    return dict(
        norm_pre_mix=row(p['norm_pre_mix']), norm_post_mix=row(p['norm_post_mix']),
        norm_pre_mlp=row(p['norm_pre_mlp']), norm_post_mlp=row(p['norm_post_mlp']),
        w_in=_pack_w_in(p['w_in'][l]), w_out=p['w_out'][l].astype(BF16),
        w_up=p['w_up'][l].astype(BF16), w_down=p['w_down'][l].astype(BF16),
        fox_bf_row=_lane_row(p['fox_b_f'][l], L_FF),
        lru_conv_w=p['lru_conv_w'][l], lru_conv_b=row(p['lru_conv_b']),
        lru_wr=_block_diag(p['lru_w_r'][l]).astype(BF16), lru_b_r=row(p['lru_b_r']),
        lru_wi=_block_diag(p['lru_w_i'][l]).astype(BF16), lru_b_i=row(p['lru_b_i']),
        lru_lambda=row(p['lru_lambda']),
        nsa_wcmp=jnp.repeat(p['nsa_w_cmp'][l].T, 128, axis=1),
        gdn_conv_w=p['gdn_conv_w'][l],
        gdn_alog_row=_lane_row(p['gdn_A_log'][l], L_DA), gdn_dt_row=_lane_row(p['gdn_dt_bias'][l], L_DA),
        gdn_norm_w=row(p['gdn_norm_w']))


def kernel(x_prompt, x_sample, cache_fox_kv, cache_fox_logf, cache_nsa_kv, cache_nsa_win,
           state_rglru_conv, state_rglru_h, state_gdn_conv, state_gdn_S, page_table,
           norm_pre_mix, norm_post_mix, norm_pre_mlp, norm_post_mlp, w_in, w_out, w_up, w_down,
           fox_b_f, lru_conv_w, lru_conv_b, lru_w_r, lru_b_r, lru_w_i, lru_b_i, lru_lambda,
           nsa_w_cmp, gdn_conv_w, gdn_A_log, gdn_dt_bias, gdn_norm_w):
    params = dict(norm_pre_mix=norm_pre_mix, norm_post_mix=norm_post_mix, norm_pre_mlp=norm_pre_mlp,
                  norm_post_mlp=norm_post_mlp, w_in=w_in, w_out=w_out, w_up=w_up, w_down=w_down,
                  fox_b_f=fox_b_f, lru_conv_w=lru_conv_w, lru_conv_b=lru_conv_b, lru_w_r=lru_w_r,
                  lru_b_r=lru_b_r, lru_w_i=lru_w_i, lru_b_i=lru_b_i, lru_lambda=lru_lambda,
                  nsa_w_cmp=nsa_w_cmp, gdn_conv_w=gdn_conv_w, gdn_A_log=gdn_A_log,
                  gdn_dt_bias=gdn_dt_bias, gdn_norm_w=gdn_norm_w)
    b, t, d = x_prompt.shape
    ns = x_sample.shape[0]
    depth, n_pool = cache_fox_kv.shape[:2]
    bt = b * t
    w = GROUP_WIDTH
    x = jnp.concatenate([x_prompt.reshape(bt, d), x_sample.reshape(ns, d)], axis=0)
    fox_pool = cache_fox_kv.reshape(depth * n_pool, PAGE_SIZE, 2 * w)
    logf_pool = cache_fox_logf.reshape(depth * n_pool, PAGE_SIZE, N_HEADS)
    nsa_pool = cache_nsa_kv.reshape(depth * n_pool, PAGE_SIZE, w)
    n_win = cache_nsa_win.shape[2]
    outs = [[] for _ in range(16)]
    for l in range(depth):
        prm = _layer_params(l, params)
        proj = _in_proj(x, prm['norm_pre_mix'], prm['w_in'])
        pp = proj[:bt].reshape(b, t, N_PROJ)
        ps = proj[bt:]
        row3 = lambda c0, c1: ps[:, None, c0:c1]

        logf_p, c_p = _fox_gate(proj, prm['fox_bf_row'], b, t)
        c_row = jnp.swapaxes(c_p[:, :N_HEADS].reshape(b, t, N_HEADS), 1, 2)
        oa_p = _fox_flash(proj, c_p, c_row, b, t)
        pt = page_table + l * n_pool
        oa_s, logf_s = _fox_step(pt, fox_pool, logf_pool, row3(C_FQ, C_FQ + 3 * w),
                                 row3(C_SMALL, C_SMALL + 128), prm['fox_bf_row'])

        ob_p, h_p = _lru_prompt(proj, prm, b, t)
        ob_s, h_s = _lru_step(ps[:, C_LX:C_LX + w], ps[:, C_LG:C_LG + w],
                              jnp.moveaxis(state_rglru_conv[l], 1, 0), state_rglru_h[l], prm)

        cmp = _nsa_cmp(proj, prm['nsa_wcmp'], b, t)
        oc_p = _nsa_prompt(proj, cmp, b, t)
        win_l = cache_nsa_win[l].reshape(ns, n_win, 256)
        oc_s = _nsa_step(pt, nsa_pool, win_l, row3(C_NQ, C_NQ + w), row3(C_NKV, C_NKV + w),
                         row3(C_NWIN, C_NWIN + 256), row3(C_SMALL, C_SMALL + 128), prm['nsa_wcmp'])

        od_p, s_p = _gdn_prompt(proj, prm, b, t)
        od_s, s_s = _gdn_step(row3(C_DQKV, C_DQKV + 3 * w), row3(C_DG, C_DG + w),
                              row3(C_SMALL, C_SMALL + 128), state_gdn_conv[l], state_gdn_S[l], prm)

        cat = lambda p_, s_: jnp.concatenate([p_, s_.reshape(ns, w)], axis=0)
        x = _out_proj(cat(oa_p, oa_s), cat(ob_p, ob_s), cat(oc_p, oc_s), cat(od_p, od_s), x,
                      prm['w_out'], prm['norm_post_mix'])
        x = _mlp(x, prm['norm_pre_mlp'], prm['w_up'], prm['w_down'], prm['norm_post_mlp'])

        new_win_s = ps[:, C_NWIN:C_NWIN + 256].reshape(ns, 1, 2, KVH_NSA, HEAD_DIM)
        win_all = jnp.concatenate([cache_nsa_win[l], new_win_s], axis=1)
        layer_out = [
            pp[:, :, C_FK:C_FK + 2 * w].reshape(b, t, 2, N_HEADS, HEAD_DIM),
            ps[:, C_FK:C_FK + 2 * w].reshape(ns, 1, 2, N_HEADS, HEAD_DIM),
            logf_p[:, :N_HEADS].reshape(b, t, N_HEADS),
            logf_s[:, :, :N_HEADS],
            pp[:, :, C_NKV:C_NKV + w].reshape(b, t, 4, KVH_NSA, HEAD_DIM),
            ps[:, C_NKV:C_NKV + w].reshape(ns, 1, 4, KVH_NSA, HEAD_DIM),
            pp[:, t - min(NSA_WINDOW, t):, C_NWIN:C_NWIN + 256].reshape(b, min(NSA_WINDOW, t), 2, KVH_NSA, HEAD_DIM),
            win_all[:, -NSA_WINDOW:],
            pp[:, t - (CONV_W - 1):, C_LX:C_LX + w],
            jnp.concatenate([state_rglru_conv[l][:, 1:], ps[:, None, C_LX:C_LX + w]], axis=1),
            h_p[:, 0],
            h_s,
            pp[:, t - (CONV_W - 1):, C_DQKV:C_DQKV + 3 * w],
            jnp.concatenate([state_gdn_conv[l][:, 1:], ps[:, None, C_DQKV:C_DQKV + 3 * w]], axis=1),
            s_p,
            s_s,
        ]
        for i, a in enumerate(layer_out):
            outs[i].append(a)
    y_prompt = x[:bt].reshape(b, t, d)
    y_sample = x[bt:].reshape(ns, 1, d)
    return (y_prompt, y_sample) + tuple(jnp.stack(o) for o in outs)
```

```python
import functools

import jax
import jax.numpy as jnp
from jax import lax
from jax.experimental import pallas as pl
from jax.experimental.pallas import tpu as pltpu

F32 = jnp.float32
BF16 = jnp.bfloat16
HI = lax.Precision.HIGHEST

D_MODEL = 2048
GROUP_WIDTH = 512
HEAD_DIM = 64
N_HEADS = 8
KVH_NSA = 2
NSA_GROUP = 4
NSA_BLOCK = 64
NSA_TOPN = 16
NSA_WINDOW = 512
FORCED_SCORE = 1e4
H_GDN = 4
GDN_DK = 128
GDN_CHUNK = 64
LRU_C = 8.0
CONV_W = 4
D_FF = 4 * D_MODEL
RMS_EPS = 1e-6
NEG = -1e30
ATT_SCALE = HEAD_DIM ** -0.5
PAGE_SIZE = 128

C_FQ, C_FK, C_FV = 0, 512, 1024
C_LX, C_LG = 1536, 2048
C_NQ = 2560
C_NKV = 3072
C_NWIN = 3584
C_SMALL = 3840
C_DQKV = 4096
C_DG = 5632
N_PROJ = 6144
L_FF, L_NG, L_DA, L_DB = 0, 8, 32, 36

VMEM_LIMIT = 56 * 1024 * 1024


def _cparams(sem):
    return pltpu.CompilerParams(dimension_semantics=sem, vmem_limit_bytes=VMEM_LIMIT)


def _pick(n, prefs):
    for p in prefs:
        if n % p == 0:
            return p
    return n


def _rms(xf, g):
    return xf * lax.rsqrt(jnp.mean(xf * xf, axis=-1, keepdims=True) + RMS_EPS) * g


def _softplus(x):
    return jnp.maximum(x, 0.0) + jnp.log1p(jnp.exp(-jnp.abs(x)))


def _log_sigmoid(x):
    return -_softplus(-x)


def _iota(shape, dim):
    return lax.broadcasted_iota(jnp.int32, shape, dim)


def _dot(a, b, **kw):
    return jnp.dot(a, b, preferred_element_type=F32, **kw)


def _dot3(a, b):
    a_hi = a.astype(BF16)
    b_hi = b.astype(BF16)
    a_lo = (a - a_hi.astype(F32)).astype(BF16)
    b_lo = (b - b_hi.astype(F32)).astype(BF16)
    return _dot(a_hi, b_hi) + (_dot(a_hi, b_lo) + _dot(a_lo, b_hi))


def _dot_nt(a, b):
    return lax.dot_general(a, b, (((1,), (1,)), ((), ())), preferred_element_type=F32)


def _dot_tn(a, b):
    return lax.dot_general(a, b, (((0,), (0,)), ((), ())), preferred_element_type=F32)


def _in_proj_kernel(x_ref, g_ref, w_ref, o_ref, h_ref):
    @pl.when(pl.program_id(1) == 0)
    def _():
        h_ref[...] = _rms(x_ref[...], g_ref[...]).astype(BF16)

    o_ref[...] = _dot(h_ref[...], w_ref[...])


def _in_proj(x, g, w):
    m, d = x.shape
    n = w.shape[1]
    tm = _pick(m, (1024, 512, 256, 128, 64, 32, 16, 8))
    tn = _pick(n, (1536, 1024, 768, 512, 256, 128))
    return pl.pallas_call(
        _in_proj_kernel,
        grid=(m // tm, n // tn),
        in_specs=[pl.BlockSpec((tm, d), lambda i, j: (i, 0)),
                  pl.BlockSpec((1, d), lambda i, j: (0, 0)),
                  pl.BlockSpec((d, tn), lambda i, j: (0, j))],
        out_specs=pl.BlockSpec((tm, tn), lambda i, j: (i, j)),
        out_shape=jax.ShapeDtypeStruct((m, n), F32),
        scratch_shapes=[pltpu.VMEM((tm, d), BF16)],
        compiler_params=_cparams(("parallel", "arbitrary")),
        name="in_proj",
    )(x, g, w)


def _out_proj_kernel(oa_ref, ob_ref, oc_ref, od_ref, x_ref, w_ref, g_ref, o_ref):
    y = _dot(oa_ref[...].astype(BF16), w_ref[0:512, :])
    y = y + _dot(ob_ref[...].astype(BF16), w_ref[512:1024, :])
    y = y + _dot(oc_ref[...].astype(BF16), w_ref[1024:1536, :])
    y = y + _dot(od_ref[...].astype(BF16), w_ref[1536:2048, :])
    o_ref[...] = x_ref[...] + _rms(y, g_ref[...])


def _out_proj(oa, ob, oc, od, x, w, g):
    m, d = x.shape
    tm = _pick(m, (640, 512, 256, 128, 64, 32, 16, 8))
    gw = GROUP_WIDTH
    return pl.pallas_call(
        _out_proj_kernel,
        grid=(m // tm,),
        in_specs=[pl.BlockSpec((tm, gw), lambda i: (i, 0))] * 4 + [
            pl.BlockSpec((tm, d), lambda i: (i, 0)),
            pl.BlockSpec((d, d), lambda i: (0, 0)),
            pl.BlockSpec((1, d), lambda i: (0, 0))],
        out_specs=pl.BlockSpec((tm, d), lambda i: (i, 0)),
        out_shape=jax.ShapeDtypeStruct((m, d), F32),
        compiler_params=_cparams(("parallel",)),
        name="out_proj",
    )(oa, ob, oc, od, x, w, g)


def _mlp_kernel(x_ref, g1_ref, wu_ref, wd_ref, g2_ref, o_ref, h_ref, acc_ref):
    f = pl.program_id(1)

    @pl.when(f == 0)
    def _():
        h_ref[...] = _rms(x_ref[...], g1_ref[...]).astype(BF16)
        acc_ref[...] = jnp.zeros_like(acc_ref)

    u = jnp.maximum(_dot(h_ref[...], wu_ref[...]), 0.0)
    acc_ref[...] += _dot((u * u).astype(BF16), wd_ref[...])

    @pl.when(f == pl.num_programs(1) - 1)
    def _():
        o_ref[...] = x_ref[...] + _rms(acc_ref[...], g2_ref[...])


def _mlp(x, g1, wu, wd, g2):
    m, d = x.shape
    ff = wu.shape[1]
    tm = _pick(m, (640, 512, 256, 128, 64, 32, 16, 8))
    tf = _pick(ff, (512, 256, 128))
    return pl.pallas_call(
        _mlp_kernel,
        grid=(m // tm, ff // tf),
        in_specs=[pl.BlockSpec((tm, d), lambda i, f: (i, 0)),
                  pl.BlockSpec((1, d), lambda i, f: (0, 0)),
                  pl.BlockSpec((d, tf), lambda i, f: (0, f)),
                  pl.BlockSpec((tf, d), lambda i, f: (f, 0)),
                  pl.BlockSpec((1, d), lambda i, f: (0, 0))],
        out_specs=pl.BlockSpec((tm, d), lambda i, f: (i, 0)),
        out_shape=jax.ShapeDtypeStruct((m, d), F32),
        scratch_shapes=[pltpu.VMEM((tm, d), BF16), pltpu.VMEM((tm, d), F32)],
        compiler_params=_cparams(("parallel", "arbitrary")),
        name="mlp",
    )(x, g1, wu, wd, g2)


def _fox_prep_kernel(x_ref, s_ref, bf_ref, qa_ref, ka_ref, v_ref, logf_ref, carry_ref, *, tb):
    @pl.when(pl.program_id(1) == 0)
    def _():
        carry_ref[...] = jnp.zeros_like(carry_ref)

    logf = _log_sigmoid(s_ref[...] + bf_ref[...])
    logf_ref[...] = logf
    tri = (_iota((tb, tb), 1) <= _iota((tb, tb), 0)).astype(F32)
    c = _dot(tri, logf, precision=HI) + carry_ref[...]
    carry_ref[...] = c[tb - 1:tb, :]
    hi = c.astype(BF16).astype(F32)
    r1 = c - hi
    mid = r1.astype(BF16).astype(F32)
    lo = r1 - mid
    lane = _iota((tb, HEAD_DIM), 1)
    q_tail = jnp.where(lane < 3, 1.0, 0.0).astype(BF16)
    x = x_ref[...]
    w = GROUP_WIDTH
    for h in range(N_HEADS):
        sl = slice(h * HEAD_DIM, (h + 1) * HEAD_DIM)
        qa_ref[h] = jnp.concatenate([(x[:, sl] * ATT_SCALE).astype(BF16), q_tail], axis=1)
        k_tail = jnp.where(lane == 0, -hi[:, h:h + 1],
                           jnp.where(lane == 1, -mid[:, h:h + 1],
                                     jnp.where(lane == 2, -lo[:, h:h + 1], 0.0)))
        ka_ref[h] = jnp.concatenate([x[:, w + h * HEAD_DIM:w + (h + 1) * HEAD_DIM].astype(BF16),
                                     k_tail.astype(BF16)], axis=1)
        v_ref[h] = x[:, 2 * w + h * HEAD_DIM:2 * w + (h + 1) * HEAD_DIM].astype(BF16)


def _fox_prep(proj, bf_row, b, t):
    tb = _pick(t, (512, 256, 128, 64))
    nt = t // tb
    w = GROUP_WIDTH
    aug = jax.ShapeDtypeStruct((b, N_HEADS, t, 128), BF16)
    return pl.pallas_call(
        functools.partial(_fox_prep_kernel, tb=tb),
        grid=(b, nt),
        in_specs=[pl.BlockSpec((tb, 3 * w), lambda i, j: (i * nt + j, C_FQ // (3 * w))),
                  pl.BlockSpec((tb, 128), lambda i, j: (i * nt + j, C_SMALL // 128)),
                  pl.BlockSpec((1, 128), lambda i, j: (0, 0))],
        out_specs=[pl.BlockSpec((None, N_HEADS, tb, 128), lambda i, j: (i, 0, j, 0)),
                   pl.BlockSpec((None, N_HEADS, tb, 128), lambda i, j: (i, 0, j, 0)),
                   pl.BlockSpec((None, N_HEADS, tb, HEAD_DIM), lambda i, j: (i, 0, j, 0)),
                   pl.BlockSpec((tb, 128), lambda i, j: (i * nt + j, 0))],
        out_shape=[aug, aug, jax.ShapeDtypeStruct((b, N_HEADS, t, HEAD_DIM), BF16),
                   jax.ShapeDtypeStruct((b * t, 128), F32)],
        scratch_shapes=[pltpu.VMEM((1, 128), F32)],
        compiler_params=_cparams(("parallel", "arbitrary")),
        name="fox_prep",
    )(proj, proj, bf_row)


def _fox_flash_kernel(qi_ref, ki_ref, q_ref, k_ref, v_ref, o_ref, m_ref, l_ref, acc_ref, *, nt):
    step_id = pl.program_id(2)
    qi = qi_ref[step_id]
    ki = ki_ref[step_id]

    @pl.when(ki == 0)
    def _():
        m_ref[...] = jnp.full_like(m_ref, NEG)
        l_ref[...] = jnp.zeros_like(l_ref)
        acc_ref[...] = jnp.zeros_like(acc_ref)

    def step(diag):
        for j in range(2):
            s = _dot_nt(q_ref[j], k_ref[j])
            if diag:
                s = jnp.where(_iota((nt, nt), 1) <= _iota((nt, nt), 0), s, NEG)
            m_prev = m_ref[j]
            m_new = jnp.maximum(m_prev, jnp.max(s, axis=-1, keepdims=True))
            alpha = jnp.exp(m_prev - m_new)
            p = jnp.exp(s - m_new)
            l_ref[j] = alpha * l_ref[j] + jnp.sum(p, axis=-1, keepdims=True)
            acc_ref[j] = alpha * acc_ref[j] + _dot(p.astype(BF16), v_ref[j])
            m_ref[j] = m_new

    @pl.when(ki < qi)
    def _():
        step(False)

    @pl.when(ki == qi)
    def _():
        step(True)
        o_ref[...] = jnp.concatenate([acc_ref[j] / l_ref[j] for j in range(2)], axis=1)


def _causal_pairs(n):
    qs = [q for q in range(n) for _ in range(q + 1)]
    ks = [k for q in range(n) for k in range(q + 1)]
    return jnp.asarray(qs, jnp.int32), jnp.asarray(ks, jnp.int32)


def _fox_flash(qa, ka, v, b, t):
    nt = _pick(t, (1024, 512, 256, 128))
    n = t // nt
    qi, ki = _causal_pairs(n)
    hp = N_HEADS // 2
    return pl.pallas_call(
        functools.partial(_fox_flash_kernel, nt=nt),
        grid_spec=pltpu.PrefetchScalarGridSpec(
            num_scalar_prefetch=2, grid=(b, hp, qi.shape[0]),
            in_specs=[pl.BlockSpec((None, 2, nt, 128), lambda i, h, s, qr, kr: (i, h, qr[s], 0)),
                      pl.BlockSpec((None, 2, nt, 128), lambda i, h, s, qr, kr: (i, h, kr[s], 0)),
                      pl.BlockSpec((None, 2, nt, HEAD_DIM), lambda i, h, s, qr, kr: (i, h, kr[s], 0))],
            out_specs=pl.BlockSpec((nt, 128), lambda i, h, s, qr, kr: (i * n + qr[s], h)),
            scratch_shapes=[pltpu.VMEM((2, nt, 1), F32), pltpu.VMEM((2, nt, 1), F32),
                            pltpu.VMEM((2, nt, HEAD_DIM), F32)]),
        out_shape=jax.ShapeDtypeStruct((b * t, GROUP_WIDTH), F32),
        compiler_params=_cparams(("parallel", "parallel", "arbitrary")),
        name="fox_flash",
    )(qi, ki, qa, ka, v)


def _lru_gates(xc, wr, br, wi, bi, lam):
    xb = xc.astype(BF16)
    r = jax.nn.sigmoid(_dot(xb, wr) + br)
    ig = jax.nn.sigmoid(_dot(xb, wi) + bi)
    log_a = -LRU_C * r * _softplus(-lam)
    a = jnp.exp(log_a)
    th = jnp.tanh(log_a)
    u = jnp.sqrt(-2.0 * th / (1.0 - th)) * (ig * xc)
    return a, u


def _conv4(w_ref, x0, x1, x2, x3, cols=slice(None)):
    y = 0.0 + w_ref[0:1, cols] * x0
    y = y + w_ref[1:2, cols] * x1
    y = y + w_ref[2:3, cols] * x2
    return y + w_ref[3:4, cols] * x3


def _lru_prompt_kernel(x_ref, g_ref, cw_ref, cb_ref, wr_ref, br_ref, wi_ref, bi_ref, lam_ref,
                       y_ref, hfin_ref, xbuf, a_s, u_s, hs, h_s, *, tb):
    @pl.when(pl.program_id(1) == 0)
    def _():
        xbuf[0:8, :] = jnp.zeros((8, GROUP_WIDTH), F32)
        h_s[...] = jnp.zeros_like(h_s)

    xbuf[8:8 + tb, :] = x_ref[...]
    xc = _conv4(cw_ref, xbuf[5:5 + tb, :], xbuf[6:6 + tb, :], xbuf[7:7 + tb, :], xbuf[8:8 + tb, :])
    xc = xc + cb_ref[...]
    xbuf[0:8, :] = xbuf[tb:tb + 8, :]
    a, u = _lru_gates(xc, wr_ref[...], br_ref[...], wi_ref[...], bi_ref[...], lam_ref[...])
    a_s[...] = a
    u_s[...] = u

    def body(t, h):
        h = a_s[pl.ds(t, 1), :] * h + u_s[pl.ds(t, 1), :]
        hs[pl.ds(t, 1), :] = h
        return h

    h = lax.fori_loop(0, tb, body, h_s[...], unroll=8)
    h_s[...] = h
    y_ref[...] = hs[...] * jax.nn.gelu(g_ref[...])
    hfin_ref[...] = jnp.broadcast_to(h, (8, GROUP_WIDTH))


def _lru_prompt(proj, prm, b, t):
    tb = _pick(t, (512, 256, 128, 64))
    nt = t // tb
    w = GROUP_WIDTH
    full = lambda shape: pl.BlockSpec(shape, lambda i, j: (0,) * len(shape))
    return pl.pallas_call(
        functools.partial(_lru_prompt_kernel, tb=tb),
        grid=(b, nt),
        in_specs=[pl.BlockSpec((tb, w), lambda i, j: (i * nt + j, C_LX // w)),
                  pl.BlockSpec((tb, w), lambda i, j: (i * nt + j, C_LG // w)),
                  full((CONV_W, w)), full((1, w)), full((w, w)), full((1, w)), full((w, w)),
                  full((1, w)), full((1, w))],
        out_specs=[pl.BlockSpec((tb, w), lambda i, j: (i * nt + j, 0)),
                   pl.BlockSpec((None, 8, w), lambda i, j: (i, 0, 0))],
        out_shape=[jax.ShapeDtypeStruct((b * t, w), F32), jax.ShapeDtypeStruct((b, 8, w), F32)],
        scratch_shapes=[pltpu.VMEM((tb + 8, w), F32), pltpu.VMEM((tb, w), F32), pltpu.VMEM((tb, w), F32),
                        pltpu.VMEM((tb, w), F32), pltpu.VMEM((1, w), F32)],
        compiler_params=_cparams(("parallel", "arbitrary")),
        name="lru_prompt",
    )(proj, proj, prm['lru_conv_w'], prm['lru_conv_b'], prm['lru_wr'], prm['lru_b_r'], prm['lru_wi'],
      prm['lru_b_i'], prm['lru_lambda'])


def _lru_step_kernel(x_ref, g_ref, buf_ref, h0_ref, cw_ref, cb_ref, wr_ref, br_ref, wi_ref, bi_ref,
                     lam_ref, y_ref, h_ref):
    xc = _conv4(cw_ref, buf_ref[0], buf_ref[1], buf_ref[2], x_ref[...]) + cb_ref[...]
    a, u = _lru_gates(xc, wr_ref[...], br_ref[...], wi_ref[...], bi_ref[...], lam_ref[...])
    h = a * h0_ref[...] + u
    h_ref[...] = h
    y_ref[...] = h * jax.nn.gelu(g_ref[...])


def _lru_step(x, g, buf, h0, prm):
    n, w = x.shape
    out = jax.ShapeDtypeStruct((n, w), F32)
    return pl.pallas_call(
        _lru_step_kernel, out_shape=[out, out], name="lru_step",
        compiler_params=pltpu.CompilerParams(vmem_limit_bytes=VMEM_LIMIT),
    )(x, g, buf, h0, prm['lru_conv_w'], prm['lru_conv_b'], prm['lru_wr'], prm['lru_b_r'], prm['lru_wi'],
      prm['lru_b_i'], prm['lru_lambda'])


def _nsa_cmp_kernel(kv_ref, w_ref, o_ref, *, tb):
    x = kv_ref[...].reshape(tb // NSA_BLOCK, NSA_BLOCK, 256)
    o_ref[...] = jnp.sum(x * w_ref[...][None], axis=1)


def _nsa_cmp(proj, wcmp, b, t):
    tb = _pick(t, (512,))
    nt = t // tb
    nb = tb // NSA_BLOCK
    return pl.pallas_call(
        functools.partial(_nsa_cmp_kernel, tb=tb),
        grid=(b, nt),
        in_specs=[pl.BlockSpec((tb, 256), lambda i, j: (i * nt + j, C_NKV // 256)),
                  pl.BlockSpec((NSA_BLOCK, 256), lambda i, j: (0, 0))],
        out_specs=pl.BlockSpec((None, nb, 256), lambda i, j: (i, j, 0)),
        out_shape=jax.ShapeDtypeStruct((b, t // NSA_BLOCK, 256), F32),
        compiler_params=_cparams(("parallel", "parallel")),
        name="nsa_cmp",
    )(proj, wcmp)


def _slope(h):
    return float(2.0 ** (-8.0 * (h + 1) / N_HEADS))


def _nsa_prompt_kernel(qi_ref, ki_ref, q_ref, ks_ref, vs_ref, kw_ref, vw_ref, cmp_ref, sm_ref, o_ref,
                       qs, oc, sel, m_s, l_s, acc_s, m_w, l_w, acc_w, *, nq, nk, nb):
    step_id = pl.program_id(1)
    qi = qi_ref[step_id]
    ki = ki_ref[step_id]
    q0 = qi * nq
    k0 = ki * nk
    k_last = (q0 + nq - 1) // nk
    n_sel = min(NSA_TOPN, nb)
    rows4 = NSA_GROUP * nq
    hd = HEAD_DIM
    head_in_group = _iota((rows4, 1), 0) // nq

    def slope_col(g):
        return jnp.exp2(-(NSA_GROUP * g + head_in_group + 1).astype(F32) * (8.0 / N_HEADS))

    @pl.when(ki == 0)
    def _():
        q = q_ref[...] * ATT_SCALE
        cmp = cmp_ref[...]
        tpos = q0 + (_iota((rows4, nb), 0) % nq)
        blk = _iota((rows4, nb), 1)
        dist_c = tpos - ((blk + 1) * NSA_BLOCK - 1)
        valid = dist_c >= 0
        dist_cf = dist_c.astype(F32)
        lane = _iota((rows4, hd), 1)
        scores = []
        for g in range(KVH_NSA):
            qg = jnp.concatenate([q[:, (NSA_GROUP * g + i) * hd:(NSA_GROUP * g + i + 1) * hd]
                                  for i in range(NSA_GROUP)], axis=0)
            tail = jnp.where(lane < 2, slope_col(g), 0.0)
            qs[g] = jnp.concatenate([qg, tail] if g == 0 else [tail, qg], axis=1).astype(BF16)
            kc = cmp[:, g * hd:(g + 1) * hd].astype(BF16)
            vc = cmp[:, 128 + g * hd:128 + (g + 1) * hd].astype(BF16)
            s = _dot_nt(qg.astype(BF16), kc) - slope_col(g) * dist_cf
            s = jnp.where(valid, s, NEG)
            m = jnp.max(s, axis=-1, keepdims=True)
            e = jnp.where(valid, jnp.exp(s - m), 0.0)
            l = jnp.sum(e, axis=-1, keepdims=True)
            p = e / jnp.where(l > 0.0, l, 1.0)
            oc[g] = _dot(p.astype(BF16), vc)
            imp = p[0:nq]
            for i in range(1, NSA_GROUP):
                imp = imp + p[i * nq:(i + 1) * nq]
            scores.append(imp)
        imp2 = jnp.concatenate(scores, axis=1)
        lane2 = _iota((nq, KVH_NSA * nb), 1)
        blk2 = lane2 % nb
        cur = (q0 + _iota((nq, KVH_NSA * nb), 0)) // NSA_BLOCK
        forced = (blk2 == 0) | (blk2 >= cur - 1)
        score = jnp.where(blk2 <= cur, jnp.where(forced, FORCED_SCORE, imp2), -1.0)
        first = lane2 < nb
        cnt = jnp.zeros((nq, KVH_NSA * nb), F32)
        for mm in range(nb):
            col = jnp.where(first, score[:, mm:mm + 1], score[:, nb + mm:nb + mm + 1])
            tie = jnp.where(blk2 > mm, 1.0, 0.0)
            cnt = cnt + jnp.where(col > score, 1.0, jnp.where(col == score, tie, 0.0))
        sel[...] = jnp.where(cnt < n_sel, jnp.where(score >= 0.0, 1.0, 0.0), 0.0)
        for ref in (m_s, m_w):
            ref[...] = jnp.full_like(ref, NEG)
        for ref in (l_s, acc_s, l_w, acc_w):
            ref[...] = jnp.zeros_like(ref)

    def aug_keys(k_ref):
        k = k_ref[...]
        c = _iota((nk, 2 * hd), 0)
        lane = _iota((nk, 2 * hd), 1)
        c_hi = ((c // 256) * 256).astype(F32)
        c_lo = (c % 256).astype(F32)
        k_g0 = jnp.where(lane < hd, k, jnp.where(lane == hd, c_hi, jnp.where(lane == hd + 1, c_lo, 0.0)))
        k_g1 = jnp.where(lane >= hd, k, jnp.where(lane == 0, c_hi, jnp.where(lane == 1, c_lo, 0.0)))
        return k_g0.astype(BF16), k_g1.astype(BF16)

    def attend(g, k_aug, v_bf, bias, m_ref, l_ref, acc_ref):
        shift = slope_col(g) * (k0 - q0).astype(F32)
        s = _dot_nt(qs[g], k_aug)
        s = (s.reshape(NSA_GROUP, nq, nk) + bias[None]).reshape(rows4, nk)
        m_prev = m_ref[g]
        m_new = jnp.maximum(m_prev, jnp.max(s, axis=-1, keepdims=True) + shift)
        alpha = jnp.exp(m_prev - m_new)
        p = jnp.exp(s - (m_new - shift))
        l_ref[g] = alpha * l_ref[g] + jnp.sum(p, axis=-1, keepdims=True)
        acc_ref[g] = alpha * acc_ref[g] + _dot(p.astype(BF16), v_bf)
        m_ref[g] = m_new

    def dist_tile():
        return (q0 + _iota((nq, nk), 0)) - (k0 + _iota((nq, nk), 1))

    @pl.when(ki <= k_last)
    def _():
        causal = dist_tile() >= 0
        k_aug = aug_keys(ks_ref)
        v_bf = vs_ref[...].astype(BF16)
        sel_bf = sel[...].astype(BF16)
        key_blk = (k0 + _iota((KVH_NSA * nb, nk), 1)) // NSA_BLOCK
        row = _iota((KVH_NSA * nb, nk), 0)
        for g in range(KVH_NSA):
            onehot = (row == g * nb + key_blk).astype(BF16)
            picked = _dot(sel_bf, onehot) > 0.5
            bias = jnp.where(causal, jnp.where(picked, 0.0, NEG), NEG)
            attend(g, k_aug[g], v_bf, bias, m_s, l_s, acc_s)

    @pl.when((ki <= k_last) & (ki >= k_last - (NSA_WINDOW + nk - 1) // nk))
    def _():
        dist = dist_tile()
        bias = jnp.where(dist >= 0, jnp.where(dist < NSA_WINDOW, 0.0, NEG), NEG)
        k_aug = aug_keys(kw_ref)
        v_bf = vw_ref[...].astype(BF16)
        for g in range(KVH_NSA):
            attend(g, k_aug[g], v_bf, bias, m_w, l_w, acc_w)

    @pl.when(ki == k_last)
    def _():
        gates = jax.nn.sigmoid(sm_ref[...])
        outs = []
        for g in range(KVH_NSA):
            o_s = acc_s[g][:, g * hd:(g + 1) * hd] / l_s[g]
            o_w = acc_w[g][:, g * hd:(g + 1) * hd] / l_w[g]
            o_c = oc[g]
            for i in range(NSA_GROUP):
                rows = slice(i * nq, (i + 1) * nq)
                c = L_NG + 3 * (NSA_GROUP * g + i)
                o = gates[:, c:c + 1] * o_c[rows]
                outs.append(o + (gates[:, c + 1:c + 2] * o_s[rows] + gates[:, c + 2:c + 3] * o_w[rows]))
        o_ref[...] = jnp.concatenate(outs, axis=1)


def _nsa_prompt(proj, cmp, b, t):
    nq = _pick(t, (256, 128))
    nk = _pick(t, (512, 256, 128))
    n_q = t // nq
    n_k = t // nk
    nb = t // NSA_BLOCK
    last = lambda q: (q * nq + nq - 1) // nk
    win_tiles = (NSA_WINDOW + nk - 1) // nk
    pairs = [(q, k) for q in range(n_q) for k in range(last(q) + 1)]
    qi = jnp.asarray([p[0] for p in pairs], jnp.int32)
    ki = jnp.asarray([p[1] for p in pairs], jnp.int32)
    rows4 = NSA_GROUP * nq

    def widx(q, k):
        kl = (q * nq + nq - 1) // nk
        return jnp.clip(k, jnp.maximum(kl - win_tiles, 0), kl)

    return pl.pallas_call(
        functools.partial(_nsa_prompt_kernel, nq=nq, nk=nk, nb=nb),
        grid_spec=pltpu.PrefetchScalarGridSpec(
            num_scalar_prefetch=2, grid=(b, len(pairs)),
            in_specs=[pl.BlockSpec((nq, 512), lambda i, s, qr, kr: (i * n_q + qr[s], C_NQ // 512)),
                      pl.BlockSpec((nk, 128), lambda i, s, qr, kr: (i * n_k + kr[s], C_NKV // 128 + 2)),
                      pl.BlockSpec((nk, 128), lambda i, s, qr, kr: (i * n_k + kr[s], C_NKV // 128 + 3)),
                      pl.BlockSpec((nk, 128), lambda i, s, qr, kr: (i * n_k + widx(qr[s], kr[s]), C_NWIN // 128)),
                      pl.BlockSpec((nk, 128), lambda i, s, qr, kr: (i * n_k + widx(qr[s], kr[s]), C_NWIN // 128 + 1)),
                      pl.BlockSpec((None, nb, 256), lambda i, s, qr, kr: (i, 0, 0)),
                      pl.BlockSpec((nq, 128), lambda i, s, qr, kr: (i * n_q + qr[s], C_SMALL // 128))],
            out_specs=pl.BlockSpec((nq, 512), lambda i, s, qr, kr: (i * n_q + qr[s], 0)),
            scratch_shapes=[pltpu.VMEM((KVH_NSA, rows4, 2 * HEAD_DIM), BF16),
                            pltpu.VMEM((KVH_NSA, rows4, HEAD_DIM), F32),
                            pltpu.VMEM((nq, KVH_NSA * nb), F32),
                            pltpu.VMEM((KVH_NSA, rows4, 1), F32), pltpu.VMEM((KVH_NSA, rows4, 1), F32),
                            pltpu.VMEM((KVH_NSA, rows4, 2 * HEAD_DIM), F32),
                            pltpu.VMEM((KVH_NSA, rows4, 1), F32), pltpu.VMEM((KVH_NSA, rows4, 1), F32),
                            pltpu.VMEM((KVH_NSA, rows4, 2 * HEAD_DIM), F32)]),
        out_shape=jax.ShapeDtypeStruct((b * t, GROUP_WIDTH), F32),
        compiler_params=_cparams(("parallel", "arbitrary")),
        name="nsa_prompt",
    )(qi, ki, proj, proj, proj, proj, proj, cmp, proj)


def _col_to_row(col, eye):
    return jnp.sum(jnp.where(eye, col, 0.0), axis=0, keepdims=True)


def _row_to_col(row, eye):
    return jnp.sum(jnp.where(eye, row, 0.0), axis=1, keepdims=True)


def _gdn_qkv(act_q, act_k, h):
    sl = slice(h * GDN_DK, (h + 1) * GDN_DK)
    q = act_q[:, sl]
    k = act_k[:, sl]
    q = q * lax.rsqrt(jnp.sum(q * q, axis=-1, keepdims=True) + 1e-6) * (GDN_DK ** -0.5)
    k = k * lax.rsqrt(jnp.sum(k * k, axis=-1, keepdims=True) + 1e-6)
    return q, k


def _gdn_out(o, nw, gate):
    return _rms(o, nw) * (gate * jax.nn.sigmoid(gate))


def _silu(x):
    return x * jax.nn.sigmoid(x)


def _gdn_pre_kernel(q_ref, k_ref, v_ref, hq_ref, hk_ref, hv_ref, sm_ref, cw_ref, al_ref, dt_ref,
                    u_ref, w_ref, qe_ref, kd_ref, qk_ref, egl_ref, xbuf, act, *, rows):
    c = GDN_CHUNK
    w5 = GROUP_WIDTH
    first = pl.program_id(1) == 0
    for seg, (ref, halo) in enumerate(((q_ref, hq_ref), (k_ref, hk_ref), (v_ref, hv_ref))):
        xbuf[seg, 0:8, :] = jnp.where(first, 0.0, halo[...])
        xbuf[seg, 8:8 + rows, :] = ref[...]
        cols = slice(seg * w5, (seg + 1) * w5)
        act[seg] = _silu(_conv4(cw_ref, xbuf[seg, 5:5 + rows, :], xbuf[seg, 6:6 + rows, :],
                                xbuf[seg, 7:7 + rows, :], xbuf[seg, 8:8 + rows, :], cols))
    ri = _iota((c, c), 0)
    ci = _iota((c, c), 1)
    incl = ri >= ci
    strict = ri > ci
    eye = ri == ci
    eye_f = eye.astype(F32)
    incl_f = incl.astype(F32)

    def chunk(idx, carry):
        r0 = pl.multiple_of(idx * c, c)
        rs = pl.ds(r0, c)
        sm = sm_ref[rs, :]
        beta_all = jax.nn.sigmoid(sm)
        la_all = -jnp.exp(al_ref[...]) * _softplus(sm + dt_ref[...])
        g_all = _dot(incl_f, la_all, precision=HI)
        egl_ref[rs, :] = jnp.broadcast_to(jnp.exp(g_all[c - 1:c, :]), (c, 128))
        act_q = act[0, rs, :]
        act_k = act[1, rs, :]
        act_v = act[2, rs, :]
        qks = []
        for h in range(H_GDN):
            sl = slice(h * GDN_DK, (h + 1) * GDN_DK)
            q, k = _gdn_qkv(act_q, act_k, h)
            v = act_v[:, sl]
            beta = beta_all[:, L_DB + h:L_DB + h + 1]
            g = g_all[:, L_DA + h:L_DA + h + 1]
            g_row = _col_to_row(g, eye)
            decay = jnp.exp(jnp.where(incl, g - g_row, -jnp.inf))
            kb = k * beta
            a = jnp.where(strict, _dot_nt(kb, k) * decay, 0.0)
            tm = eye_f - a
            pw = _dot3(a, a)
            for it in range(5):
                tm = tm + _dot3(tm, pw)
                if it < 4:
                    pw = _dot3(pw, pw)
            eg = jnp.exp(g)
            u_ref[rs, sl] = _dot(tm, v * beta)
            w_ref[rs, sl] = _dot(tm, kb * eg)
            qe_ref[rs, sl] = q * eg
            kd_ref[rs, sl] = k * jnp.exp(g[c - 1:c, :] - g)
            qks.append(_dot_nt(q, k) * decay)
        qk_ref[rs, :] = jnp.concatenate(qks, axis=1)
        return carry

    lax.fori_loop(0, rows // c, chunk, 0, unroll=2)


def _gdn_seq_kernel(u_ref, w_ref, qe_ref, kd_ref, qk_ref, egl_ref, g_ref, nw_ref, o_ref, s_ref, *, rows):
    c = GDN_CHUNK

    @pl.when(pl.program_id(1) == 0)
    def _():
        s_ref[...] = jnp.zeros_like(s_ref)

    nw = nw_ref[...]

    def chunk(idx, carry):
        r0 = pl.multiple_of(idx * c, c)
        rs = pl.ds(r0, c)
        egl = egl_ref[pl.ds(r0, 1), :]
        for h in range(H_GDN):
            sl = slice(h * GDN_DK, (h + 1) * GDN_DK)
            s0 = s_ref[h]
            u_new = u_ref[rs, sl] - _dot(w_ref[rs, sl], s0)
            o = _dot(qe_ref[rs, sl], s0) + _dot(qk_ref[rs, h * c:(h + 1) * c], u_new)
            s_ref[h] = s0 * egl[:, L_DA + h:L_DA + h + 1] + _dot_tn(kd_ref[rs, sl], u_new)
            o_ref[rs, sl] = _gdn_out(o, nw, g_ref[rs, sl])
        return carry

    lax.fori_loop(0, rows // c, chunk, 0)


def _gdn_prompt(proj, prm, b, t):
    c = GDN_CHUNK
    w = GROUP_WIDTH
    rows = _pick(t, (512, 256, 128, 64))
    nr = t // rows
    full = lambda shape: pl.BlockSpec(shape, lambda i, j: (0,) * len(shape))
    seg = lambda col: pl.BlockSpec((rows, w), lambda i, j: (i * nr + j, col // w))
    halo = lambda col: pl.BlockSpec(
        (8, w), lambda i, j: (jnp.maximum((i * nr + j) * (rows // 8) - 1, 0), col // w))
    blk = lambda width: pl.BlockSpec((rows, width), lambda i, j: (i * nr + j, 0))
    f32 = lambda width: jax.ShapeDtypeStruct((b * t, width), F32)
    u, wk, qe, kd, qk, egl = pl.pallas_call(
        functools.partial(_gdn_pre_kernel, rows=rows),
        grid=(b, nr),
        in_specs=[seg(C_DQKV), seg(C_DQKV + w), seg(C_DQKV + 2 * w),
                  halo(C_DQKV), halo(C_DQKV + w), halo(C_DQKV + 2 * w),
                  pl.BlockSpec((rows, 128), lambda i, j: (i * nr + j, C_SMALL // 128)),
                  full((CONV_W, 3 * w)), full((1, 128)), full((1, 128))],
        out_specs=[blk(w), blk(w), blk(w), blk(w), blk(H_GDN * c), blk(128)],
        out_shape=[f32(w), f32(w), f32(w), f32(w), f32(H_GDN * c), f32(128)],
        scratch_shapes=[pltpu.VMEM((3, rows + 8, w), F32), pltpu.VMEM((3, rows, w), F32)],
        compiler_params=_cparams(("parallel", "parallel")),
        name="gdn_pre",
    )(proj, proj, proj, proj, proj, proj, proj, prm['gdn_conv_w'], prm['gdn_alog_row'], prm['gdn_dt_row'])
    return pl.pallas_call(
        functools.partial(_gdn_seq_kernel, rows=rows),
        grid=(b, nr),
        in_specs=[blk(w), blk(w), blk(w), blk(w), blk(H_GDN * c), blk(128), seg(C_DG), full((1, GDN_DK))],
        out_specs=[blk(w), pl.BlockSpec((None, H_GDN, GDN_DK, GDN_DK), lambda i, j: (i, 0, 0, 0))],
        out_shape=[f32(w), jax.ShapeDtypeStruct((b, H_GDN, GDN_DK, GDN_DK), F32)],
        compiler_params=_cparams(("parallel", "arbitrary")),
        name="gdn_seq",
    )(u, wk, qe, kd, qk, egl, proj, prm['gdn_norm_w'])


def _gdn_step_kernel(x_ref, g_ref, sm_ref, buf_ref, s0_ref, cw_ref, al_ref, dt_ref, nw_ref,
                     o_ref, s_ref):
    buf = buf_ref[...]
    act = _silu(_conv4(cw_ref, buf[0:1], buf[1:2], buf[2:3], x_ref[...]))
    sm = sm_ref[...]
    beta_all = jax.nn.sigmoid(sm)
    alpha_all = jnp.exp(-jnp.exp(al_ref[...]) * _softplus(sm + dt_ref[...]))
    eye = _iota((GDN_DK, GDN_DK), 0) == _iota((GDN_DK, GDN_DK), 1)
    w = GROUP_WIDTH
    gate = g_ref[...]
    outs = []
    for h in range(H_GDN):
        sl = slice(h * GDN_DK, (h + 1) * GDN_DK)
        q, k = _gdn_qkv(act[:, 0:w], act[:, w:2 * w], h)
        v = act[:, 2 * w + h * GDN_DK:2 * w + (h + 1) * GDN_DK]
        beta = beta_all[:, L_DB + h:L_DB + h + 1]
        alpha = alpha_all[:, L_DA + h:L_DA + h + 1]
        k_col = _row_to_col(k, eye)
        q_col = _row_to_col(q, eye)
        s0 = s0_ref[h]
        ks = jnp.sum(s0 * k_col, axis=0, keepdims=True)
        u = beta * v - (beta * alpha) * ks
        s1 = alpha * s0 + k_col * u
        s_ref[h] = s1
        o = jnp.sum(s1 * q_col, axis=0, keepdims=True)
        outs.append(_gdn_out(o, nw_ref[...], gate[:, sl]))
    o_ref[...] = jnp.concatenate(outs, axis=1)


def _gdn_step(x, gate, small, conv_buf, s0, prm):
    n = x.shape[0]
    w = GROUP_WIDTH
    full = lambda shape: pl.BlockSpec(shape, lambda i: (0,) * len(shape))
    return pl.pallas_call(
        _gdn_step_kernel,
        grid=(n,),
        in_specs=[pl.BlockSpec((None, 1, 3 * w), lambda i: (i, 0, 0)),
                  pl.BlockSpec((None, 1, w), lambda i: (i, 0, 0)),
                  pl.BlockSpec((None, 1, 128), lambda i: (i, 0, 0)),
                  pl.BlockSpec((None, CONV_W - 1, 3 * w), lambda i: (i, 0, 0)),
                  pl.BlockSpec((None, H_GDN, GDN_DK, GDN_DK), lambda i: (i, 0, 0, 0)),
                  full((CONV_W, 3 * w)), full((1, 128)), full((1, 128)), full((1, GDN_DK))],
        out_specs=[pl.BlockSpec((None, 1, w), lambda i: (i, 0, 0)),
                   pl.BlockSpec((None, H_GDN, GDN_DK, GDN_DK), lambda i: (i, 0, 0, 0))],
        out_shape=[jax.ShapeDtypeStruct((n, 1, w), F32),
                   jax.ShapeDtypeStruct((n, H_GDN, GDN_DK, GDN_DK), F32)],
        compiler_params=_cparams(("parallel",)),
        name="gdn_step",
    )(x, gate, small, conv_buf, s0, prm['gdn_conv_w'], prm['gdn_alog_row'], prm['gdn_dt_row'],
      prm['gdn_norm_w'])


def _dot_row(row, mat, **kw):
    return _dot(jnp.broadcast_to(row, (8, row.shape[1])), mat, **kw)[0:1]


def _fox_step_kernel(pt_ref, *refs, n_pages):
    kv = refs[:n_pages]
    lf = refs[n_pages:2 * n_pages]
    x_ref, sm_ref, bf_ref, o_ref, lf_out_ref = refs[2 * n_pages:]
    w = GROUP_WIDTH
    pg = PAGE_SIZE
    x = x_ref[...]
    q, kn, vn = x[:, 0:w], x[:, w:2 * w], x[:, 2 * w:3 * w]
    logf_new = _log_sigmoid(sm_ref[...] + bf_ref[...])
    lf_out_ref[...] = logf_new
    head_cols = (_iota((w, N_HEADS), 0) // HEAD_DIM == _iota((w, N_HEADS), 1)).astype(BF16)
    head_rows = _iota((N_HEADS, w), 1) // HEAD_DIM == _iota((N_HEADS, w), 0)
    eye = _iota((w, w), 0) == _iota((w, w), 1)
    qblk = _dot(jnp.where(eye, q, 0.0).astype(BF16), head_cols).astype(BF16)
    tri = (_iota((pg, pg), 1) <= _iota((pg, pg), 0)).astype(F32)
    carry = jnp.zeros((1, N_HEADS), F32)
    cs = []
    for p in range(n_pages):
        c = _dot(tri, lf[p][...], precision=HI) + carry
        carry = c[pg - 1:pg, :]
        cs.append(c)
    ctot = carry + logf_new[:, 0:N_HEADS]
    s_n = _dot_row(kn.astype(BF16), qblk) * ATT_SCALE
    m = s_n
    ss = []
    stride = 2 * N_HEADS
    for p in range(n_pages):
        s = None
        for h in range(N_HEADS):
            k_h = kv[p][pl.ds(h, pg, stride=stride), :].astype(BF16)
            part = _dot(k_h, qblk[h * HEAD_DIM:(h + 1) * HEAD_DIM, :])
            s = part if s is None else s + part
        s = s * ATT_SCALE + (ctot - cs[p])
        ss.append(s)
        m = jnp.maximum(m, jnp.max(s, axis=0, keepdims=True))
    e_n = jnp.exp(s_n - m)
    l = e_n
    acc = [jnp.zeros((8, HEAD_DIM), F32) for _ in range(N_HEADS)]
    for p in range(n_pages):
        e = jnp.exp(ss[p] - m)
        l = l + jnp.sum(e, axis=0, keepdims=True)
        for h in range(N_HEADS):
            v_h = kv[p][pl.ds(N_HEADS + h, pg, stride=stride), :]
            acc[h] = acc[h] + jnp.sum((e[:, h:h + 1] * v_h).reshape(pg // 8, 8, HEAD_DIM), axis=0)
    tot = jnp.concatenate([jnp.sum(a, axis=0, keepdims=True) for a in acc], axis=1)
    tot = tot + _dot_row(e_n.astype(BF16), head_rows.astype(BF16)) * vn
    o_ref[...] = tot * _dot_row(1.0 / l, head_rows.astype(F32), precision=HI)


def _fox_step(pt, kv_pool, lf_pool, x, small, bf_row):
    n, n_pages = pt.shape
    w = GROUP_WIDTH
    page = lambda i, rows, width: pl.BlockSpec((None, rows, width), lambda b, t: (t[b, i], 0, 0))
    in_specs = ([page(i, PAGE_SIZE * 2 * N_HEADS, HEAD_DIM) for i in range(n_pages)] +
                [page(i, PAGE_SIZE, N_HEADS) for i in range(n_pages)] +
                [pl.BlockSpec((None, 1, 3 * w), lambda b, t: (b, 0, 0)),
                 pl.BlockSpec((None, 1, 128), lambda b, t: (b, 0, 0)),
                 pl.BlockSpec((1, 128), lambda b, t: (0, 0))])
    return pl.pallas_call(
        functools.partial(_fox_step_kernel, n_pages=n_pages),
        grid_spec=pltpu.PrefetchScalarGridSpec(
            num_scalar_prefetch=1, grid=(n,), in_specs=in_specs,
            out_specs=[pl.BlockSpec((None, 1, w), lambda b, t: (b, 0, 0)),
                       pl.BlockSpec((None, 1, 128), lambda b, t: (b, 0, 0))]),
        out_shape=[jax.ShapeDtypeStruct((n, 1, w), F32), jax.ShapeDtypeStruct((n, 1, 128), F32)],
        compiler_params=_cparams(("parallel",)),
        name="fox_step",
    )(pt, *([kv_pool] * n_pages), *([lf_pool] * n_pages), x, small, bf_row)


def _softmax_cols(scores):
    m = None
    for s in scores:
        mx = jnp.max(s, axis=0, keepdims=True)
        m = mx if m is None else jnp.maximum(m, mx)
    es = [jnp.exp(s - m) for s in scores]
    l = None
    for e in es:
        sm = jnp.sum(e, axis=0, keepdims=True)
        l = sm if l is None else l + sm
    return es, l


def _nsa_step_kernel(pt_ref, *refs, n_pages):
    pgs = refs[:n_pages]
    win_ref, q_ref, kv_ref, wn_ref, sm_ref, wc_ref, o_ref = refs[n_pages:]
    pg = PAGE_SIZE
    w = GROUP_WIDTH
    kvw = KVH_NSA * HEAD_DIM
    past = n_pages * pg
    nbp = past // NSA_BLOCK
    q = q_ref[...]
    r = _iota((kvw, w), 0)
    c = _iota((kvw, w), 1)
    fold = (c % HEAD_DIM == r % HEAD_DIM) & (c // (NSA_GROUP * HEAD_DIM) == r // HEAD_DIM)
    head_cols = (_iota((w, N_HEADS), 0) // HEAD_DIM == _iota((w, N_HEADS), 1)).astype(BF16)
    qn = _dot(jnp.where(fold, q, 0.0).astype(BF16), head_cols).astype(BF16)
    hl = _iota((1, N_HEADS), 1)
    slopes = jnp.exp2(-(hl + 1).astype(F32) * (8.0 / N_HEADS))
    spread = [_iota((N_HEADS, kvw), 0) == NSA_GROUP * (_iota((N_HEADS, kvw), 1) // HEAD_DIM) + i
              for i in range(NSA_GROUP)]
    spread_bf = [s.astype(BF16) for s in spread]
    spread_f = [s.astype(F32) for s in spread]

    def weighted_rows(es, vals):
        out = []
        for i in range(NSA_GROUP):
            tot = None
            for e, v in zip(es, vals):
                if e.shape[0] == 1:
                    part = _dot_row(e.astype(BF16), spread_bf[i]) * v
                else:
                    part = jnp.sum(_dot(e.astype(BF16), spread_bf[i]) * v, axis=0, keepdims=True)
                tot = part if tot is None else tot + part
            out.append(tot)
        return out

    wc = wc_ref[...]
    kc_rows, vc_rows = [], []
    for p in range(n_pages):
        rows = pgs[p][:, 0:2 * kvw]
        for half in range(pg // NSA_BLOCK):
            cm = jnp.sum(rows[half * NSA_BLOCK:(half + 1) * NSA_BLOCK] * wc, axis=0, keepdims=True)
            kc_rows.append(cm[:, 0:kvw])
            vc_rows.append(cm[:, kvw:2 * kvw])
    kcmp = jnp.concatenate(kc_rows, axis=0)
    vcmp = jnp.concatenate(vc_rows, axis=0)
    blk = _iota((nbp, 1), 0)
    dist_c = (past - ((blk + 1) * NSA_BLOCK - 1)).astype(F32)
    s_c = _dot(kcmp.astype(BF16), qn) * ATT_SCALE - slopes * dist_c
    (e_c,), l_c = _softmax_cols([s_c])
    p_c = e_c / l_c
    acc_c = weighted_rows([p_c], [vcmp])

    cur = past // NSA_BLOCK
    forced = (blk == 0) | (blk >= cur - 1)
    eye_b = _iota((nbp, nbp), 0) == _iota((nbp, nbp), 1)
    lower = _iota((nbp, nbp), 1) < _iota((nbp, nbp), 0)
    n_sel = min(NSA_TOPN, nbp + 1)
    sels = []
    for g in range(KVH_NSA):
        imp = p_c[:, NSA_GROUP * g:NSA_GROUP * g + 1]
        for i in range(1, NSA_GROUP):
            imp = imp + p_c[:, NSA_GROUP * g + i:NSA_GROUP * g + i + 1]
        score = jnp.where(forced, FORCED_SCORE, imp)
        score_row = _col_to_row(score, eye_b)
        beats = (score_row > score) | ((score_row == score) & lower)
        cnt = jnp.sum(jnp.where(beats, 1.0, 0.0), axis=1, keepdims=True)
        cnt = cnt + jnp.where(score < FORCED_SCORE, 1.0, 0.0)
        sels.append(jnp.where(cnt < n_sel, 1.0, 0.0))
    sel8 = jnp.where(hl < NSA_GROUP, sels[0], sels[1])

    kv_new = kv_ref[...]
    first_half = _iota((pg, 1), 0) < NSA_BLOCK
    scores, vals = [], []
    for p in range(n_pages):
        dist = (past - (p * pg + _iota((pg, 1), 0))).astype(F32)
        s = _dot(pgs[p][:, 2 * kvw:3 * kvw].astype(BF16), qn) * ATT_SCALE - slopes * dist
        chosen = jnp.where(first_half, sel8[2 * p:2 * p + 1, :], sel8[2 * p + 1:2 * p + 2, :])
        scores.append(jnp.where(chosen > 0.5, s, NEG))
        vals.append(pgs[p][:, 3 * kvw:4 * kvw])
    scores.append(_dot_row(kv_new[:, 2 * kvw:3 * kvw].astype(BF16), qn) * ATT_SCALE)
    vals.append(kv_new[:, 3 * kvw:4 * kvw])
    e_s, l_s = _softmax_cols(scores)
    acc_s = weighted_rows(e_s, vals)

    nw = win_ref.shape[0]
    wrow = _iota((nw, 1), 0)
    dist_w = nw - wrow
    s_w = _dot(win_ref[:, 0:kvw].astype(BF16), qn) * ATT_SCALE - slopes * dist_w.astype(F32)
    s_w = jnp.where((dist_w < NSA_WINDOW) & (past - dist_w >= 0), s_w, NEG)
    wn = wn_ref[...]
    s_wn = _dot_row(wn[:, 0:kvw].astype(BF16), qn) * ATT_SCALE
    e_w, l_w = _softmax_cols([s_w, s_wn])
    acc_w = weighted_rows(e_w, [win_ref[:, kvw:2 * kvw], wn[:, kvw:2 * kvw]])

    gates = jax.nn.sigmoid(sm_ref[...])
    lane = _iota((128, N_HEADS), 0)
    head = _iota((128, N_HEADS), 1)
    coef = []
    for k in range(3):
        pick = (lane == L_NG + 3 * head + k).astype(F32)
        coef.append(_dot_row(gates, pick, precision=HI))
    coef[1] = coef[1] / l_s
    coef[2] = coef[2] / l_w
    tots = []
    for i in range(NSA_GROUP):
        t = _dot_row(coef[0], spread_f[i], precision=HI) * acc_c[i]
        t = t + (_dot_row(coef[1], spread_f[i], precision=HI) * acc_s[i]
                 + _dot_row(coef[2], spread_f[i], precision=HI) * acc_w[i])
        tots.append(t)
    o_ref[...] = jnp.concatenate([t[:, 0:HEAD_DIM] for t in tots] + [t[:, HEAD_DIM:kvw] for t in tots], axis=1)


def _nsa_step(pt, kv_pool, win, q, kv_new, win_new, small, wcmp):
    n, n_pages = pt.shape
    w = GROUP_WIDTH
    nw = win.shape[1]
    row = lambda width: pl.BlockSpec((None, 1, width), lambda b, t: (b, 0, 0))
    in_specs = ([pl.BlockSpec((None, PAGE_SIZE, w), (lambda i: (lambda b, t: (t[b, i], 0, 0)))(i))
                 for i in range(n_pages)] +
                [pl.BlockSpec((None, nw, 256), lambda b, t: (b, 0, 0)),
                 row(w), row(w), row(256), row(128),
                 pl.BlockSpec((NSA_BLOCK, 256), lambda b, t: (0, 0))])
    return pl.pallas_call(
        functools.partial(_nsa_step_kernel, n_pages=n_pages),
        grid_spec=pltpu.PrefetchScalarGridSpec(
            num_scalar_prefetch=1, grid=(n,), in_specs=in_specs,
            out_specs=pl.BlockSpec((None, 1, w), lambda b, t: (b, 0, 0))),
        out_shape=jax.ShapeDtypeStruct((n, 1, w), F32),
        compiler_params=_cparams(("parallel",)),
        name="nsa_step",
    )(pt, *([kv_pool] * n_pages), win, q, kv_new, win_new, small, wcmp)


_O_FF, _O_LX, _O_NQ, _O_NKC, _O_NG, _O_DQKV, _O_DG, _O_DA, _O_END = (
    1536, 1544, 2568, 3080, 3848, 3872, 5408, 5920, 5928)


def _pack_w_in(w):
    d = w.shape[0]
    zeros = lambda n: jnp.zeros((d, n), w.dtype)
    parts = [w[:, 0:_O_FF], w[:, _O_LX:_O_NQ], w[:, _O_NQ:_O_NKC], w[:, _O_NKC:_O_NG],
             w[:, _O_FF:_O_LX], w[:, _O_NG:_O_DQKV], w[:, _O_DA:_O_END], zeros(128 - 40), zeros(128),
             w[:, _O_DQKV:_O_DG], w[:, _O_DG:_O_DA]]
    out = jnp.concatenate(parts, axis=1)
    assert out.shape[1] == N_PROJ
    return out.astype(BF16)


def _block_diag(w):
    n, d, e = w.shape
    eye = jnp.eye(n, dtype=w.dtype)
    return (eye[:, None, :, None] * w[:, :, None, :]).reshape(n * d, n * e)


def _lane_row(vals, start):
    return jnp.zeros((1, 128), F32).at[0, start:start + vals.shape[0]].set(vals)


def _layer_params(l, p):
    row = lambda a: a[l][None, :]
    return dict(
        norm_pre_mix=row(p['norm_pre_mix']), norm_post_mix=row(p['norm_post_mix']),
        norm_pre_mlp=row(p['norm_pre_mlp']), norm_post_mlp=row(p['norm_post_mlp']),
        w_in=_pack_w_in(p['w_in'][l]), w_out=p['w_out'][l].astype(BF16),
        w_up=p['w_up'][l].astype(BF16), w_down=p['w_down'][l].astype(BF16),
        fox_bf_row=_lane_row(p['fox_b_f'][l], L_FF),
        lru_conv_w=p['lru_conv_w'][l], lru_conv_b=row(p['lru_conv_b']),
        lru_wr=_block_diag(p['lru_w_r'][l]).astype(BF16), lru_b_r=row(p['lru_b_r']),
        lru_wi=_block_diag(p['lru_w_i'][l]).astype(BF16), lru_b_i=row(p['lru_b_i']),
        lru_lambda=row(p['lru_lambda']),
        nsa_wcmp=jnp.repeat(p['nsa_w_cmp'][l].T, 128, axis=1),
        gdn_conv_w=p['gdn_conv_w'][l],
        gdn_alog_row=_lane_row(p['gdn_A_log'][l], L_DA), gdn_dt_row=_lane_row(p['gdn_dt_bias'][l], L_DA),
        gdn_norm_w=row(p['gdn_norm_w']))


def kernel(x_prompt, x_sample, cache_fox_kv, cache_fox_logf, cache_nsa_kv, cache_nsa_win,
           state_rglru_conv, state_rglru_h, state_gdn_conv, state_gdn_S, page_table,
           norm_pre_mix, norm_post_mix, norm_pre_mlp, norm_post_mlp, w_in, w_out, w_up, w_down,
           fox_b_f, lru_conv_w, lru_conv_b, lru_w_r, lru_b_r, lru_w_i, lru_b_i, lru_lambda,
           nsa_w_cmp, gdn_conv_w, gdn_A_log, gdn_dt_bias, gdn_norm_w):
    params = dict(norm_pre_mix=norm_pre_mix, norm_post_mix=norm_post_mix, norm_pre_mlp=norm_pre_mlp,
                  norm_post_mlp=norm_post_mlp, w_in=w_in, w_out=w_out, w_up=w_up, w_down=w_down,
                  fox_b_f=fox_b_f, lru_conv_w=lru_conv_w, lru_conv_b=lru_conv_b, lru_w_r=lru_w_r,
                  lru_b_r=lru_b_r, lru_w_i=lru_w_i, lru_b_i=lru_b_i, lru_lambda=lru_lambda,
                  nsa_w_cmp=nsa_w_cmp, gdn_conv_w=gdn_conv_w, gdn_A_log=gdn_A_log,
                  gdn_dt_bias=gdn_dt_bias, gdn_norm_w=gdn_norm_w)
    b, t, d = x_prompt.shape
    ns = x_sample.shape[0]
    depth, n_pool = cache_fox_kv.shape[:2]
    bt = b * t
    w = GROUP_WIDTH
    xp = x_prompt.reshape(bt, d)
    xs = x_sample.reshape(ns, d)
    fox_pool = cache_fox_kv.reshape(depth * n_pool, PAGE_SIZE * 2 * N_HEADS, HEAD_DIM)
    logf_pool = cache_fox_logf.reshape(depth * n_pool, PAGE_SIZE, N_HEADS)
    nsa_pool = cache_nsa_kv.reshape(depth * n_pool, PAGE_SIZE, w)
    n_win = cache_nsa_win.shape[2]
    outs = [[] for _ in range(16)]
    for l in range(depth):
        prm = _layer_params(l, params)
        proj = _in_proj(xp, prm['norm_pre_mix'], prm['w_in'])
        ps = _in_proj(xs, prm['norm_pre_mix'], prm['w_in'])
        pp = proj.reshape(b, t, N_PROJ)
        row3 = lambda c0, c1: ps[:, None, c0:c1]

        qa, ka, va, logf_p = _fox_prep(proj, prm['fox_bf_row'], b, t)
        oa_p = _fox_flash(qa, ka, va, b, t)
        pt = page_table + l * n_pool
        oa_s, logf_s = _fox_step(pt, fox_pool, logf_pool, row3(C_FQ, C_FQ + 3 * w),
                                 row3(C_SMALL, C_SMALL + 128), prm['fox_bf_row'])

        ob_p, h_p = _lru_prompt(proj, prm, b, t)
        ob_s, h_s = _lru_step(ps[:, C_LX:C_LX + w], ps[:, C_LG:C_LG + w],
                              jnp.moveaxis(state_rglru_conv[l], 1, 0), state_rglru_h[l], prm)

        cmp = _nsa_cmp(proj, prm['nsa_wcmp'], b, t)
        oc_p = _nsa_prompt(proj, cmp, b, t)
        win_l = cache_nsa_win[l].reshape(ns, n_win, 256)
        oc_s = _nsa_step(pt, nsa_pool, win_l, row3(C_NQ, C_NQ + w), row3(C_NKV, C_NKV + w),
                         row3(C_NWIN, C_NWIN + 256), row3(C_SMALL, C_SMALL + 128), prm['nsa_wcmp'])

        od_p, s_p = _gdn_prompt(proj, prm, b, t)
        od_s, s_s = _gdn_step(row3(C_DQKV, C_DQKV + 3 * w), row3(C_DG, C_DG + w),
                              row3(C_SMALL, C_SMALL + 128), state_gdn_conv[l], state_gdn_S[l], prm)

        xp = _out_proj(oa_p, ob_p, oc_p, od_p, xp, prm['w_out'], prm['norm_post_mix'])
        xp = _mlp(xp, prm['norm_pre_mlp'], prm['w_up'], prm['w_down'], prm['norm_post_mlp'])
        flat = lambda a: a.reshape(ns, w)
        xs = _out_proj(flat(oa_s), ob_s, flat(oc_s), flat(od_s), xs, prm['w_out'], prm['norm_post_mix'])
        xs = _mlp(xs, prm['norm_pre_mlp'], prm['w_up'], prm['w_down'], prm['norm_post_mlp'])

        new_win_s = ps[:, C_NWIN:C_NWIN + 256].reshape(ns, 1, 2, KVH_NSA, HEAD_DIM)
        win_all = jnp.concatenate([cache_nsa_win[l], new_win_s], axis=1)
        layer_out = [
            pp[:, :, C_FK:C_FK + 2 * w].reshape(b, t, 2, N_HEADS, HEAD_DIM),
            ps[:, C_FK:C_FK + 2 * w].reshape(ns, 1, 2, N_HEADS, HEAD_DIM),
            logf_p[:, :N_HEADS].reshape(b, t, N_HEADS),
            logf_s[:, :, :N_HEADS],
            pp[:, :, C_NKV:C_NKV + w].reshape(b, t, 4, KVH_NSA, HEAD_DIM),
            ps[:, C_NKV:C_NKV + w].reshape(ns, 1, 4, KVH_NSA, HEAD_DIM),
            pp[:, t - min(NSA_WINDOW, t):, C_NWIN:C_NWIN + 256].reshape(b, min(NSA_WINDOW, t), 2, KVH_NSA, HEAD_DIM),
            win_all[:, -NSA_WINDOW:],
            pp[:, t - (CONV_W - 1):, C_LX:C_LX + w],
            jnp.concatenate([state_rglru_conv[l][:, 1:], ps[:, None, C_LX:C_LX + w]], axis=1),
            h_p[:, 0],
            h_s,
            pp[:, t - (CONV_W - 1):, C_DQKV:C_DQKV + 3 * w],
            jnp.concatenate([state_gdn_conv[l][:, 1:], ps[:, None, C_DQKV:C_DQKV + 3 * w]], axis=1),
            s_p,
            s_s,
        ]
        for i, a in enumerate(layer_out):
            outs[i].append(a)
    y_prompt = xp.reshape(b, t, d)
    y_sample = xs.reshape(ns, 1, d)
    return (y_prompt, y_sample) + tuple(jnp.stack(o) for o in outs)
```

```python
import functools

import jax
import jax.numpy as jnp
from jax import lax
from jax.experimental import pallas as pl
from jax.experimental.pallas import tpu as pltpu

F32 = jnp.float32
BF16 = jnp.bfloat16
HI = lax.Precision.HIGHEST

D_MODEL = 2048
GROUP_WIDTH = 512
HEAD_DIM = 64
N_HEADS = 8
KVH_NSA = 2
NSA_GROUP = 4
NSA_BLOCK = 64
NSA_TOPN = 16
NSA_WINDOW = 512
FORCED_SCORE = 1e4
H_GDN = 4
GDN_DK = 128
GDN_CHUNK = 64
LRU_C = 8.0
CONV_W = 4
D_FF = 4 * D_MODEL
RMS_EPS = 1e-6
NEG = -1e30
ATT_SCALE = HEAD_DIM ** -0.5
PAGE_SIZE = 128

C_FQ, C_FK, C_FV = 0, 512, 1024
C_LX, C_LG = 1536, 2048
C_NQ = 2560
C_NKV = 3072
C_NWIN = 3584
C_SMALL = 3840
C_DQKV = 4096
C_DG = 5632
N_PROJ = 6144
L_FF, L_NG, L_DA, L_DB = 0, 8, 32, 36

VMEM_LIMIT = 56 * 1024 * 1024


def _cparams(sem):
    return pltpu.CompilerParams(dimension_semantics=sem, vmem_limit_bytes=VMEM_LIMIT)


def _pick(n, prefs):
    for p in prefs:
        if n % p == 0:
            return p
    return n


def _rms(xf, g):
    return xf * lax.rsqrt(jnp.mean(xf * xf, axis=-1, keepdims=True) + RMS_EPS) * g


def _softplus(x):
    return jnp.maximum(x, 0.0) + jnp.log1p(jnp.exp(-jnp.abs(x)))


def _log_sigmoid(x):
    return -_softplus(-x)


def _iota(shape, dim):
    return lax.broadcasted_iota(jnp.int32, shape, dim)


def _dot(a, b, **kw):
    return jnp.dot(a, b, preferred_element_type=F32, **kw)


def _dot3(a, b):
    a_hi = a.astype(BF16)
    b_hi = b.astype(BF16)
    a_lo = (a - a_hi.astype(F32)).astype(BF16)
    b_lo = (b - b_hi.astype(F32)).astype(BF16)
    return _dot(a_hi, b_hi) + (_dot(a_hi, b_lo) + _dot(a_lo, b_hi))


def _dot_nt(a, b):
    return lax.dot_general(a, b, (((1,), (1,)), ((), ())), preferred_element_type=F32)


def _dot_tn(a, b):
    return lax.dot_general(a, b, (((0,), (0,)), ((), ())), preferred_element_type=F32)


def _in_proj_kernel(x_ref, g_ref, w_ref, o_ref, h_ref):
    @pl.when(pl.program_id(1) == 0)
    def _():
        h_ref[...] = _rms(x_ref[...], g_ref[...]).astype(BF16)

    o_ref[...] = _dot(h_ref[...], w_ref[...])


def _in_proj(x, g, w):
    m, d = x.shape
    n = w.shape[1]
    tm = _pick(m, (1024, 512, 256, 128, 64, 32, 16, 8))
    tn = _pick(n, (1536, 1024, 768, 512, 256, 128))
    return pl.pallas_call(
        _in_proj_kernel,
        grid=(m // tm, n // tn),
        in_specs=[pl.BlockSpec((tm, d), lambda i, j: (i, 0)),
                  pl.BlockSpec((1, d), lambda i, j: (0, 0)),
                  pl.BlockSpec((d, tn), lambda i, j: (0, j))],
        out_specs=pl.BlockSpec((tm, tn), lambda i, j: (i, j)),
        out_shape=jax.ShapeDtypeStruct((m, n), F32),
        scratch_shapes=[pltpu.VMEM((tm, d), BF16)],
        compiler_params=_cparams(("parallel", "arbitrary")),
        name="in_proj",
    )(x, g, w)


def _out_proj_kernel(oa_ref, ob_ref, oc_ref, od_ref, x_ref, w_ref, g_ref, o_ref):
    y = _dot(oa_ref[...].astype(BF16), w_ref[0:512, :])
    y = y + _dot(ob_ref[...].astype(BF16), w_ref[512:1024, :])
    y = y + _dot(oc_ref[...].astype(BF16), w_ref[1024:1536, :])
    y = y + _dot(od_ref[...].astype(BF16), w_ref[1536:2048, :])
    o_ref[...] = x_ref[...] + _rms(y, g_ref[...])


def _out_proj(oa, ob, oc, od, x, w, g):
    m, d = x.shape
    tm = _pick(m, (640, 512, 256, 128, 64, 32, 16, 8))
    gw = GROUP_WIDTH
    return pl.pallas_call(
        _out_proj_kernel,
        grid=(m // tm,),
        in_specs=[pl.BlockSpec((tm, gw), lambda i: (i, 0))] * 4 + [
            pl.BlockSpec((tm, d), lambda i: (i, 0)),
            pl.BlockSpec((d, d), lambda i: (0, 0)),
            pl.BlockSpec((1, d), lambda i: (0, 0))],
        out_specs=pl.BlockSpec((tm, d), lambda i: (i, 0)),
        out_shape=jax.ShapeDtypeStruct((m, d), F32),
        compiler_params=_cparams(("parallel",)),
        name="out_proj",
    )(oa, ob, oc, od, x, w, g)


def _mlp_kernel(x_ref, g1_ref, wu_ref, wd_ref, g2_ref, o_ref, h_ref, acc_ref):
    f = pl.program_id(1)

    @pl.when(f == 0)
    def _():
        h_ref[...] = _rms(x_ref[...], g1_ref[...]).astype(BF16)
        acc_ref[...] = jnp.zeros_like(acc_ref)

    u = jnp.maximum(_dot(h_ref[...], wu_ref[...]), 0.0)
    acc_ref[...] += _dot((u * u).astype(BF16), wd_ref[...])

    @pl.when(f == pl.num_programs(1) - 1)
    def _():
        o_ref[...] = x_ref[...] + _rms(acc_ref[...], g2_ref[...])


def _mlp(x, g1, wu, wd, g2):
    m, d = x.shape
    ff = wu.shape[1]
    tm = _pick(m, (640, 512, 256, 128, 64, 32, 16, 8))
    tf = _pick(ff, (512, 256, 128))
    return pl.pallas_call(
        _mlp_kernel,
        grid=(m // tm, ff // tf),
        in_specs=[pl.BlockSpec((tm, d), lambda i, f: (i, 0)),
                  pl.BlockSpec((1, d), lambda i, f: (0, 0)),
                  pl.BlockSpec((d, tf), lambda i, f: (0, f)),
                  pl.BlockSpec((tf, d), lambda i, f: (f, 0)),
                  pl.BlockSpec((1, d), lambda i, f: (0, 0))],
        out_specs=pl.BlockSpec((tm, d), lambda i, f: (i, 0)),
        out_shape=jax.ShapeDtypeStruct((m, d), F32),
        scratch_shapes=[pltpu.VMEM((tm, d), BF16), pltpu.VMEM((tm, d), F32)],
        compiler_params=_cparams(("parallel", "arbitrary")),
        name="mlp",
    )(x, g1, wu, wd, g2)


def _fox_prep_kernel(x_ref, s_ref, bf_ref, qa_ref, ka_ref, v_ref, logf_ref, carry_ref, *, tb):
    @pl.when(pl.program_id(1) == 0)
    def _():
        carry_ref[...] = jnp.zeros_like(carry_ref)

    logf = _log_sigmoid(s_ref[...] + bf_ref[...])
    logf_ref[...] = logf
    tri = (_iota((tb, tb), 1) <= _iota((tb, tb), 0)).astype(F32)
    c = _dot(tri, logf, precision=HI) + carry_ref[...]
    carry_ref[...] = c[tb - 1:tb, :]
    hi = c.astype(BF16).astype(F32)
    r1 = c - hi
    mid = r1.astype(BF16).astype(F32)
    lo = r1 - mid
    lane = _iota((tb, HEAD_DIM), 1)
    q_tail = jnp.where(lane < 3, 1.0, 0.0).astype(BF16)
    x = x_ref[...]
    w = GROUP_WIDTH
    for h in range(N_HEADS):
        sl = slice(h * HEAD_DIM, (h + 1) * HEAD_DIM)
        qa_ref[h] = jnp.concatenate([(x[:, sl] * ATT_SCALE).astype(BF16), q_tail], axis=1)
        k_tail = jnp.where(lane == 0, -hi[:, h:h + 1],
                           jnp.where(lane == 1, -mid[:, h:h + 1],
                                     jnp.where(lane == 2, -lo[:, h:h + 1], 0.0)))
        ka_ref[h] = jnp.concatenate([x[:, w + h * HEAD_DIM:w + (h + 1) * HEAD_DIM].astype(BF16),
                                     k_tail.astype(BF16)], axis=1)
        v_ref[h] = x[:, 2 * w + h * HEAD_DIM:2 * w + (h + 1) * HEAD_DIM].astype(BF16)


def _fox_prep(proj, bf_row, b, t):
    tb = _pick(t, (512, 256, 128, 64))
    nt = t // tb
    w = GROUP_WIDTH
    aug = jax.ShapeDtypeStruct((b, N_HEADS, t, 128), BF16)
    return pl.pallas_call(
        functools.partial(_fox_prep_kernel, tb=tb),
        grid=(b, nt),
        in_specs=[pl.BlockSpec((tb, 3 * w), lambda i, j: (i * nt + j, C_FQ // (3 * w))),
                  pl.BlockSpec((tb, 128), lambda i, j: (i * nt + j, C_SMALL // 128)),
                  pl.BlockSpec((1, 128), lambda i, j: (0, 0))],
        out_specs=[pl.BlockSpec((None, N_HEADS, tb, 128), lambda i, j: (i, 0, j, 0)),
                   pl.BlockSpec((None, N_HEADS, tb, 128), lambda i, j: (i, 0, j, 0)),
                   pl.BlockSpec((None, N_HEADS, tb, HEAD_DIM), lambda i, j: (i, 0, j, 0)),
                   pl.BlockSpec((tb, 128), lambda i, j: (i * nt + j, 0))],
        out_shape=[aug, aug, jax.ShapeDtypeStruct((b, N_HEADS, t, HEAD_DIM), BF16),
                   jax.ShapeDtypeStruct((b * t, 128), F32)],
        scratch_shapes=[pltpu.VMEM((1, 128), F32)],
        compiler_params=_cparams(("parallel", "arbitrary")),
        name="fox_prep",
    )(proj, proj, bf_row)


def _fox_flash_kernel(qi_ref, ki_ref, q_ref, k_ref, v_ref, o_ref, m_ref, l_ref, acc_ref, *, nt):
    step_id = pl.program_id(2)
    qi = qi_ref[step_id]
    ki = ki_ref[step_id]

    @pl.when(ki == 0)
    def _():
        m_ref[...] = jnp.full_like(m_ref, NEG)
        l_ref[...] = jnp.zeros_like(l_ref)
        acc_ref[...] = jnp.zeros_like(acc_ref)

    def step(diag):
        for j in range(2):
            s = _dot_nt(q_ref[j], k_ref[j])
            if diag:
                s = jnp.where(_iota((nt, nt), 1) <= _iota((nt, nt), 0), s, NEG)
            m_prev = m_ref[j]
            m_new = jnp.maximum(m_prev, jnp.max(s, axis=-1, keepdims=True))
            alpha = jnp.exp(m_prev - m_new)
            p = jnp.exp(s - m_new)
            l_ref[j] = alpha * l_ref[j] + jnp.sum(p, axis=-1, keepdims=True)
            acc_ref[j] = alpha * acc_ref[j] + _dot(p.astype(BF16), v_ref[j])
            m_ref[j] = m_new

    @pl.when(ki < qi)
    def _():
        step(False)

    @pl.when(ki == qi)
    def _():
        step(True)
        o_ref[...] = jnp.concatenate([acc_ref[j] / l_ref[j] for j in range(2)], axis=1)


def _causal_pairs(n):
    qs = [q for q in range(n) for _ in range(q + 1)]
    ks = [k for q in range(n) for k in range(q + 1)]
    return jnp.asarray(qs, jnp.int32), jnp.asarray(ks, jnp.int32)


def _fox_flash(qa, ka, v, b, t):
    nt = _pick(t, (1024, 512, 256, 128))
    n = t // nt
    qi, ki = _causal_pairs(n)
    hp = N_HEADS // 2
    return pl.pallas_call(
        functools.partial(_fox_flash_kernel, nt=nt),
        grid_spec=pltpu.PrefetchScalarGridSpec(
            num_scalar_prefetch=2, grid=(b, hp, qi.shape[0]),
            in_specs=[pl.BlockSpec((None, 2, nt, 128), lambda i, h, s, qr, kr: (i, h, qr[s], 0)),
                      pl.BlockSpec((None, 2, nt, 128), lambda i, h, s, qr, kr: (i, h, kr[s], 0)),
                      pl.BlockSpec((None, 2, nt, HEAD_DIM), lambda i, h, s, qr, kr: (i, h, kr[s], 0))],
            out_specs=pl.BlockSpec((nt, 128), lambda i, h, s, qr, kr: (i * n + qr[s], h)),
            scratch_shapes=[pltpu.VMEM((2, nt, 1), F32), pltpu.VMEM((2, nt, 1), F32),
                            pltpu.VMEM((2, nt, HEAD_DIM), F32)]),
        out_shape=jax.ShapeDtypeStruct((b * t, GROUP_WIDTH), F32),
        compiler_params=_cparams(("parallel", "parallel", "arbitrary")),
        name="fox_flash",
    )(qi, ki, qa, ka, v)


def _lru_gates(xc, wr, br, wi, bi, lam):
    xb = xc.astype(BF16)
    r = jax.nn.sigmoid(_dot(xb, wr) + br)
    ig = jax.nn.sigmoid(_dot(xb, wi) + bi)
    log_a = -LRU_C * r * _softplus(-lam)
    a = jnp.exp(log_a)
    th = jnp.tanh(log_a)
    u = jnp.sqrt(-2.0 * th / (1.0 - th)) * (ig * xc)
    return a, u


def _conv4(w_ref, x0, x1, x2, x3, cols=slice(None)):
    y = 0.0 + w_ref[0:1, cols] * x0
    y = y + w_ref[1:2, cols] * x1
    y = y + w_ref[2:3, cols] * x2
    return y + w_ref[3:4, cols] * x3


def _lru_prompt_kernel(x_ref, g_ref, cw_ref, cb_ref, wr_ref, br_ref, wi_ref, bi_ref, lam_ref,
                       y_ref, hfin_ref, xbuf, a_s, u_s, hs, h_s, *, tb):
    @pl.when(pl.program_id(1) == 0)
    def _():
        xbuf[0:8, :] = jnp.zeros((8, GROUP_WIDTH), F32)
        h_s[...] = jnp.zeros_like(h_s)

    xbuf[8:8 + tb, :] = x_ref[...]
    xc = _conv4(cw_ref, xbuf[5:5 + tb, :], xbuf[6:6 + tb, :], xbuf[7:7 + tb, :], xbuf[8:8 + tb, :])
    xc = xc + cb_ref[...]
    xbuf[0:8, :] = xbuf[tb:tb + 8, :]
    a, u = _lru_gates(xc, wr_ref[...], br_ref[...], wi_ref[...], bi_ref[...], lam_ref[...])
    a_s[...] = a
    u_s[...] = u

    def body(t, h):
        h = a_s[pl.ds(t, 1), :] * h + u_s[pl.ds(t, 1), :]
        hs[pl.ds(t, 1), :] = h
        return h

    h = lax.fori_loop(0, tb, body, h_s[...], unroll=8)
    h_s[...] = h
    y_ref[...] = hs[...] * jax.nn.gelu(g_ref[...])
    hfin_ref[...] = jnp.broadcast_to(h, (8, GROUP_WIDTH))


def _lru_prompt(proj, prm, b, t):
    tb = _pick(t, (512, 256, 128, 64))
    nt = t // tb
    w = GROUP_WIDTH
    full = lambda shape: pl.BlockSpec(shape, lambda i, j: (0,) * len(shape))
    return pl.pallas_call(
        functools.partial(_lru_prompt_kernel, tb=tb),
        grid=(b, nt),
        in_specs=[pl.BlockSpec((tb, w), lambda i, j: (i * nt + j, C_LX // w)),
                  pl.BlockSpec((tb, w), lambda i, j: (i * nt + j, C_LG // w)),
                  full((CONV_W, w)), full((1, w)), full((w, w)), full((1, w)), full((w, w)),
                  full((1, w)), full((1, w))],
        out_specs=[pl.BlockSpec((tb, w), lambda i, j: (i * nt + j, 0)),
                   pl.BlockSpec((None, 8, w), lambda i, j: (i, 0, 0))],
        out_shape=[jax.ShapeDtypeStruct((b * t, w), F32), jax.ShapeDtypeStruct((b, 8, w), F32)],
        scratch_shapes=[pltpu.VMEM((tb + 8, w), F32), pltpu.VMEM((tb, w), F32), pltpu.VMEM((tb, w), F32),
                        pltpu.VMEM((tb, w), F32), pltpu.VMEM((1, w), F32)],
        compiler_params=_cparams(("parallel", "arbitrary")),
        name="lru_prompt",
    )(proj, proj, prm['lru_conv_w'], prm['lru_conv_b'], prm['lru_wr'], prm['lru_b_r'], prm['lru_wi'],
      prm['lru_b_i'], prm['lru_lambda'])


def _lru_step_kernel(x_ref, g_ref, buf_ref, h0_ref, cw_ref, cb_ref, wr_ref, br_ref, wi_ref, bi_ref,
                     lam_ref, y_ref, h_ref):
    xc = _conv4(cw_ref, buf_ref[0], buf_ref[1], buf_ref[2], x_ref[...]) + cb_ref[...]
    a, u = _lru_gates(xc, wr_ref[...], br_ref[...], wi_ref[...], bi_ref[...], lam_ref[...])
    h = a * h0_ref[...] + u
    h_ref[...] = h
    y_ref[...] = h * jax.nn.gelu(g_ref[...])


def _lru_step(x, g, buf, h0, prm):
    n, w = x.shape
    out = jax.ShapeDtypeStruct((n, w), F32)
    return pl.pallas_call(
        _lru_step_kernel, out_shape=[out, out], name="lru_step",
        compiler_params=pltpu.CompilerParams(vmem_limit_bytes=VMEM_LIMIT),
    )(x, g, buf, h0, prm['lru_conv_w'], prm['lru_conv_b'], prm['lru_wr'], prm['lru_b_r'], prm['lru_wi'],
      prm['lru_b_i'], prm['lru_lambda'])


def _nsa_cmp_kernel(kv_ref, w_ref, o_ref, *, tb):
    x = kv_ref[...].reshape(tb // NSA_BLOCK, NSA_BLOCK, 256)
    o_ref[...] = jnp.sum(x * w_ref[...][None], axis=1)


def _nsa_cmp(proj, wcmp, b, t):
    tb = _pick(t, (512,))
    nt = t // tb
    nb = tb // NSA_BLOCK
    return pl.pallas_call(
        functools.partial(_nsa_cmp_kernel, tb=tb),
        grid=(b, nt),
        in_specs=[pl.BlockSpec((tb, 256), lambda i, j: (i * nt + j, C_NKV // 256)),
                  pl.BlockSpec((NSA_BLOCK, 256), lambda i, j: (0, 0))],
        out_specs=pl.BlockSpec((None, nb, 256), lambda i, j: (i, j, 0)),
        out_shape=jax.ShapeDtypeStruct((b, t // NSA_BLOCK, 256), F32),
        compiler_params=_cparams(("parallel", "parallel")),
        name="nsa_cmp",
    )(proj, wcmp)


def _slope(h):
    return float(2.0 ** (-8.0 * (h + 1) / N_HEADS))


def _nsa_prompt_kernel(qi_ref, ki_ref, q_ref, ks_ref, vs_ref, kw_ref, vw_ref, cmp_ref, sm_ref, o_ref,
                       qs, oc, sel, m_s, l_s, acc_s, m_w, l_w, acc_w, *, nq, nk, nb):
    step_id = pl.program_id(1)
    qi = qi_ref[step_id]
    ki = ki_ref[step_id]
    q0 = qi * nq
    k0 = ki * nk
    k_last = (q0 + nq - 1) // nk
    n_sel = min(NSA_TOPN, nb)
    rows4 = NSA_GROUP * nq
    hd = HEAD_DIM
    head_in_group = _iota((rows4, 1), 0) // nq

    def slope_col(g):
        return jnp.exp2(-(NSA_GROUP * g + head_in_group + 1).astype(F32) * (8.0 / N_HEADS))

    @pl.when(ki == 0)
    def _():
        q = q_ref[...] * ATT_SCALE
        cmp = cmp_ref[...]
        tpos = q0 + (_iota((rows4, nb), 0) % nq)
        blk = _iota((rows4, nb), 1)
        dist_c = tpos - ((blk + 1) * NSA_BLOCK - 1)
        valid = dist_c >= 0
        dist_cf = dist_c.astype(F32)
        lane = _iota((rows4, hd), 1)
        scores = []
        for g in range(KVH_NSA):
            qg = jnp.concatenate([q[:, (NSA_GROUP * g + i) * hd:(NSA_GROUP * g + i + 1) * hd]
                                  for i in range(NSA_GROUP)], axis=0)
            tail = jnp.where(lane < 2, slope_col(g), 0.0)
            qs[g] = jnp.concatenate([qg, tail] if g == 0 else [tail, qg], axis=1).astype(BF16)
            kc = cmp[:, g * hd:(g + 1) * hd].astype(BF16)
            vc = cmp[:, 128 + g * hd:128 + (g + 1) * hd].astype(BF16)
            s = _dot_nt(qg.astype(BF16), kc) - slope_col(g) * dist_cf
            s = jnp.where(valid, s, NEG)
            m = jnp.max(s, axis=-1, keepdims=True)
            e = jnp.where(valid, jnp.exp(s - m), 0.0)
            l = jnp.sum(e, axis=-1, keepdims=True)
            p = e / jnp.where(l > 0.0, l, 1.0)
            oc[g] = _dot(p.astype(BF16), vc)
            imp = p[0:nq]
            for i in range(1, NSA_GROUP):
                imp = imp + p[i * nq:(i + 1) * nq]
            scores.append(imp)
        imp2 = jnp.concatenate(scores, axis=1)
        lane2 = _iota((nq, KVH_NSA * nb), 1)
        blk2 = lane2 % nb
        cur = (q0 + _iota((nq, KVH_NSA * nb), 0)) // NSA_BLOCK
        forced = (blk2 == 0) | (blk2 >= cur - 1)
        score = jnp.where(blk2 <= cur, jnp.where(forced, FORCED_SCORE, imp2), -1.0)
        first = lane2 < nb
        cnt = jnp.zeros((nq, KVH_NSA * nb), F32)
        for mm in range(nb):
            col = jnp.where(first, score[:, mm:mm + 1], score[:, nb + mm:nb + mm + 1])
            tie = jnp.where(blk2 > mm, 1.0, 0.0)
            cnt = cnt + jnp.where(col > score, 1.0, jnp.where(col == score, tie, 0.0))
        sel[...] = jnp.where(cnt < n_sel, jnp.where(score >= 0.0, 1.0, 0.0), 0.0)
        for ref in (m_s, m_w):
            ref[...] = jnp.full_like(ref, NEG)
        for ref in (l_s, acc_s, l_w, acc_w):
            ref[...] = jnp.zeros_like(ref)

    def aug_keys(k_ref):
        k = k_ref[...]
        c = _iota((nk, 2 * hd), 0)
        lane = _iota((nk, 2 * hd), 1)
        c_hi = ((c // 256) * 256).astype(F32)
        c_lo = (c % 256).astype(F32)
        k_g0 = jnp.where(lane < hd, k, jnp.where(lane == hd, c_hi, jnp.where(lane == hd + 1, c_lo, 0.0)))
        k_g1 = jnp.where(lane >= hd, k, jnp.where(lane == 0, c_hi, jnp.where(lane == 1, c_lo, 0.0)))
        return k_g0.astype(BF16), k_g1.astype(BF16)

    def attend(g, k_aug, v_bf, bias, m_ref, l_ref, acc_ref):
        shift = slope_col(g) * (k0 - q0).astype(F32)
        s = _dot_nt(qs[g], k_aug)
        s = (s.reshape(NSA_GROUP, nq, nk) + bias[None]).reshape(rows4, nk)
        m_prev = m_ref[g]
        m_new = jnp.maximum(m_prev, jnp.max(s, axis=-1, keepdims=True) + shift)
        alpha = jnp.exp(m_prev - m_new)
        p = jnp.exp(s - (m_new - shift))
        l_ref[g] = alpha * l_ref[g] + jnp.sum(p, axis=-1, keepdims=True)
        acc_ref[g] = alpha * acc_ref[g] + _dot(p.astype(BF16), v_bf)
        m_ref[g] = m_new

    def dist_tile():
        return (q0 + _iota((nq, nk), 0)) - (k0 + _iota((nq, nk), 1))

    @pl.when(ki <= k_last)
    def _():
        causal = dist_tile() >= 0
        k_aug = aug_keys(ks_ref)
        v_bf = vs_ref[...].astype(BF16)
        sel_bf = sel[...].astype(BF16)
        key_blk = (k0 + _iota((KVH_NSA * nb, nk), 1)) // NSA_BLOCK
        row = _iota((KVH_NSA * nb, nk), 0)
        for g in range(KVH_NSA):
            onehot = (row == g * nb + key_blk).astype(BF16)
            picked = _dot(sel_bf, onehot) > 0.5
            bias = jnp.where(causal, jnp.where(picked, 0.0, NEG), NEG)
            attend(g, k_aug[g], v_bf, bias, m_s, l_s, acc_s)

    @pl.when((ki <= k_last) & (ki >= k_last - (NSA_WINDOW + nk - 1) // nk))
    def _():
        dist = dist_tile()
        bias = jnp.where(dist >= 0, jnp.where(dist < NSA_WINDOW, 0.0, NEG), NEG)
        k_aug = aug_keys(kw_ref)
        v_bf = vw_ref[...].astype(BF16)
        for g in range(KVH_NSA):
            attend(g, k_aug[g], v_bf, bias, m_w, l_w, acc_w)

    @pl.when(ki == k_last)
    def _():
        gates = jax.nn.sigmoid(sm_ref[...])
        outs = []
        for g in range(KVH_NSA):
            o_s = acc_s[g][:, g * hd:(g + 1) * hd] / l_s[g]
            o_w = acc_w[g][:, g * hd:(g + 1) * hd] / l_w[g]
            o_c = oc[g]
            for i in range(NSA_GROUP):
                rows = slice(i * nq, (i + 1) * nq)
                c = L_NG + 3 * (NSA_GROUP * g + i)
                o = gates[:, c:c + 1] * o_c[rows]
                outs.append(o + (gates[:, c + 1:c + 2] * o_s[rows] + gates[:, c + 2:c + 3] * o_w[rows]))
        o_ref[...] = jnp.concatenate(outs, axis=1)


def _nsa_prompt(proj, cmp, b, t):
    nq = _pick(t, (256, 128))
    nk = _pick(t, (512, 256, 128))
    n_q = t // nq
    n_k = t // nk
    nb = t // NSA_BLOCK
    last = lambda q: (q * nq + nq - 1) // nk
    win_tiles = (NSA_WINDOW + nk - 1) // nk
    pairs = [(q, k) for q in range(n_q) for k in range(last(q) + 1)]
    qi = jnp.asarray([p[0] for p in pairs], jnp.int32)
    ki = jnp.asarray([p[1] for p in pairs], jnp.int32)
    rows4 = NSA_GROUP * nq

    def widx(q, k):
        kl = (q * nq + nq - 1) // nk
        return jnp.clip(k, jnp.maximum(kl - win_tiles, 0), kl)

    return pl.pallas_call(
        functools.partial(_nsa_prompt_kernel, nq=nq, nk=nk, nb=nb),
        grid_spec=pltpu.PrefetchScalarGridSpec(
            num_scalar_prefetch=2, grid=(b, len(pairs)),
            in_specs=[pl.BlockSpec((nq, 512), lambda i, s, qr, kr: (i * n_q + qr[s], C_NQ // 512)),
                      pl.BlockSpec((nk, 128), lambda i, s, qr, kr: (i * n_k + kr[s], C_NKV // 128 + 2)),
                      pl.BlockSpec((nk, 128), lambda i, s, qr, kr: (i * n_k + kr[s], C_NKV // 128 + 3)),
                      pl.BlockSpec((nk, 128), lambda i, s, qr, kr: (i * n_k + widx(qr[s], kr[s]), C_NWIN // 128)),
                      pl.BlockSpec((nk, 128), lambda i, s, qr, kr: (i * n_k + widx(qr[s], kr[s]), C_NWIN // 128 + 1)),
                      pl.BlockSpec((None, nb, 256), lambda i, s, qr, kr: (i, 0, 0)),
                      pl.BlockSpec((nq, 128), lambda i, s, qr, kr: (i * n_q + qr[s], C_SMALL // 128))],
            out_specs=pl.BlockSpec((nq, 512), lambda i, s, qr, kr: (i * n_q + qr[s], 0)),
            scratch_shapes=[pltpu.VMEM((KVH_NSA, rows4, 2 * HEAD_DIM), BF16),
                            pltpu.VMEM((KVH_NSA, rows4, HEAD_DIM), F32),
                            pltpu.VMEM((nq, KVH_NSA * nb), F32),
                            pltpu.VMEM((KVH_NSA, rows4, 1), F32), pltpu.VMEM((KVH_NSA, rows4, 1), F32),
                            pltpu.VMEM((KVH_NSA, rows4, 2 * HEAD_DIM), F32),
                            pltpu.VMEM((KVH_NSA, rows4, 1), F32), pltpu.VMEM((KVH_NSA, rows4, 1), F32),
                            pltpu.VMEM((KVH_NSA, rows4, 2 * HEAD_DIM), F32)]),
        out_shape=jax.ShapeDtypeStruct((b * t, GROUP_WIDTH), F32),
        compiler_params=_cparams(("parallel", "arbitrary")),
        name="nsa_prompt",
    )(qi, ki, proj, proj, proj, proj, proj, cmp, proj)


def _col_to_row(col, eye):
    return jnp.sum(jnp.where(eye, col, 0.0), axis=0, keepdims=True)


def _row_to_col(row, eye):
    return jnp.sum(jnp.where(eye, row, 0.0), axis=1, keepdims=True)


def _gdn_qkv(act_q, act_k, h):
    sl = slice(h * GDN_DK, (h + 1) * GDN_DK)
    q = act_q[:, sl]
    k = act_k[:, sl]
    q = q * lax.rsqrt(jnp.sum(q * q, axis=-1, keepdims=True) + 1e-6) * (GDN_DK ** -0.5)
    k = k * lax.rsqrt(jnp.sum(k * k, axis=-1, keepdims=True) + 1e-6)
    return q, k


def _gdn_out(o, nw, gate):
    return _rms(o, nw) * (gate * jax.nn.sigmoid(gate))


def _silu(x):
    return x * jax.nn.sigmoid(x)


def _gdn_pre_kernel(q_ref, k_ref, v_ref, hq_ref, hk_ref, hv_ref, sm_ref, cw_ref, al_ref, dt_ref,
                    u_ref, w_ref, qe_ref, kd_ref, qk_ref, egl_ref, xbuf, act, *, rows):
    c = GDN_CHUNK
    w5 = GROUP_WIDTH
    first = pl.program_id(1) == 0
    for seg, (ref, halo) in enumerate(((q_ref, hq_ref), (k_ref, hk_ref), (v_ref, hv_ref))):
        xbuf[seg, 0:8, :] = jnp.where(first, 0.0, halo[...])
        xbuf[seg, 8:8 + rows, :] = ref[...]
        cols = slice(seg * w5, (seg + 1) * w5)
        act[seg] = _silu(_conv4(cw_ref, xbuf[seg, 5:5 + rows, :], xbuf[seg, 6:6 + rows, :],
                                xbuf[seg, 7:7 + rows, :], xbuf[seg, 8:8 + rows, :], cols))
    ri = _iota((c, c), 0)
    ci = _iota((c, c), 1)
    incl = ri >= ci
    strict = ri > ci
    eye = ri == ci
    eye_f = eye.astype(F32)
    incl_f = incl.astype(F32)

    def chunk(idx, carry):
        r0 = pl.multiple_of(idx * c, c)
        rs = pl.ds(r0, c)
        sm = sm_ref[rs, :]
        beta_all = jax.nn.sigmoid(sm)
        la_all = -jnp.exp(al_ref[...]) * _softplus(sm + dt_ref[...])
        g_all = _dot(incl_f, la_all, precision=HI)
        egl_ref[rs, :] = jnp.broadcast_to(jnp.exp(g_all[c - 1:c, :]), (c, 128))
        act_q = act[0, rs, :]
        act_k = act[1, rs, :]
        act_v = act[2, rs, :]
        qks = []
        for h in range(H_GDN):
            sl = slice(h * GDN_DK, (h + 1) * GDN_DK)
            q, k = _gdn_qkv(act_q, act_k, h)
            v = act_v[:, sl]
            beta = beta_all[:, L_DB + h:L_DB + h + 1]
            g = g_all[:, L_DA + h:L_DA + h + 1]
            g_row = _col_to_row(g, eye)
            decay = jnp.exp(jnp.where(incl, g - g_row, -jnp.inf))
            kb = k * beta
            a = jnp.where(strict, _dot_nt(kb, k) * decay, 0.0)
            tm = eye_f - a
            pw = _dot3(a, a)
            for it in range(5):
                tm = tm + _dot3(tm, pw)
                if it < 4:
                    pw = _dot3(pw, pw)
            eg = jnp.exp(g)
            u_ref[rs, sl] = _dot(tm, v * beta)
            w_ref[rs, sl] = _dot(tm, kb * eg)
            qe_ref[rs, sl] = q * eg
            kd_ref[rs, sl] = k * jnp.exp(g[c - 1:c, :] - g)
            qks.append(_dot_nt(q, k) * decay)
        qk_ref[rs, :] = jnp.concatenate(qks, axis=1)
        return carry

    lax.fori_loop(0, rows // c, chunk, 0, unroll=2)


def _gdn_seq_kernel(u_ref, w_ref, qe_ref, kd_ref, qk_ref, egl_ref, g_ref, nw_ref, o_ref, s_ref, *, rows):
    c = GDN_CHUNK

    @pl.when(pl.program_id(1) == 0)
    def _():
        s_ref[...] = jnp.zeros_like(s_ref)

    nw = nw_ref[...]

    def chunk(idx, carry):
        r0 = pl.multiple_of(idx * c, c)
        rs = pl.ds(r0, c)
        egl = egl_ref[pl.ds(r0, 1), :]
        for h in range(H_GDN):
            sl = slice(h * GDN_DK, (h + 1) * GDN_DK)
            s0 = s_ref[h]
            u_new = u_ref[rs, sl] - _dot(w_ref[rs, sl], s0)
            o = _dot(qe_ref[rs, sl], s0) + _dot(qk_ref[rs, h * c:(h + 1) * c], u_new)
            s_ref[h] = s0 * egl[:, L_DA + h:L_DA + h + 1] + _dot_tn(kd_ref[rs, sl], u_new)
            o_ref[rs, sl] = _gdn_out(o, nw, g_ref[rs, sl])
        return carry

    lax.fori_loop(0, rows // c, chunk, 0)


def _gdn_prompt(proj, prm, b, t):
    c = GDN_CHUNK
    w = GROUP_WIDTH
    rows = _pick(t, (512, 256, 128, 64))
    nr = t // rows
    full = lambda shape: pl.BlockSpec(shape, lambda i, j: (0,) * len(shape))
    seg = lambda col: pl.BlockSpec((rows, w), lambda i, j: (i * nr + j, col // w))
    halo = lambda col: pl.BlockSpec(
        (8, w), lambda i, j: (jnp.maximum((i * nr + j) * (rows // 8) - 1, 0), col // w))
    blk = lambda width: pl.BlockSpec((rows, width), lambda i, j: (i * nr + j, 0))
    f32 = lambda width: jax.ShapeDtypeStruct((b * t, width), F32)
    u, wk, qe, kd, qk, egl = pl.pallas_call(
        functools.partial(_gdn_pre_kernel, rows=rows),
        grid=(b, nr),
        in_specs=[seg(C_DQKV), seg(C_DQKV + w), seg(C_DQKV + 2 * w),
                  halo(C_DQKV), halo(C_DQKV + w), halo(C_DQKV + 2 * w),
                  pl.BlockSpec((rows, 128), lambda i, j: (i * nr + j, C_SMALL // 128)),
                  full((CONV_W, 3 * w)), full((1, 128)), full((1, 128))],
        out_specs=[blk(w), blk(w), blk(w), blk(w), blk(H_GDN * c), blk(128)],
        out_shape=[f32(w), f32(w), f32(w), f32(w), f32(H_GDN * c), f32(128)],
        scratch_shapes=[pltpu.VMEM((3, rows + 8, w), F32), pltpu.VMEM((3, rows, w), F32)],
        compiler_params=_cparams(("parallel", "parallel")),
        name="gdn_pre",
    )(proj, proj, proj, proj, proj, proj, proj, prm['gdn_conv_w'], prm['gdn_alog_row'], prm['gdn_dt_row'])
    return pl.pallas_call(
        functools.partial(_gdn_seq_kernel, rows=rows),
        grid=(b, nr),
        in_specs=[blk(w), blk(w), blk(w), blk(w), blk(H_GDN * c), blk(128), seg(C_DG), full((1, GDN_DK))],
        out_specs=[blk(w), pl.BlockSpec((None, H_GDN, GDN_DK, GDN_DK), lambda i, j: (i, 0, 0, 0))],
        out_shape=[f32(w), jax.ShapeDtypeStruct((b, H_GDN, GDN_DK, GDN_DK), F32)],
        compiler_params=_cparams(("parallel", "arbitrary")),
        name="gdn_seq",
    )(u, wk, qe, kd, qk, egl, proj, prm['gdn_norm_w'])


def _gdn_step_kernel(x_ref, g_ref, sm_ref, buf_ref, s0_ref, cw_ref, al_ref, dt_ref, nw_ref,
                     o_ref, s_ref):
    buf = buf_ref[...]
    act = _silu(_conv4(cw_ref, buf[0:1], buf[1:2], buf[2:3], x_ref[...]))
    sm = sm_ref[...]
    beta_all = jax.nn.sigmoid(sm)
    alpha_all = jnp.exp(-jnp.exp(al_ref[...]) * _softplus(sm + dt_ref[...]))
    eye = _iota((GDN_DK, GDN_DK), 0) == _iota((GDN_DK, GDN_DK), 1)
    w = GROUP_WIDTH
    gate = g_ref[...]
    outs = []
    for h in range(H_GDN):
        sl = slice(h * GDN_DK, (h + 1) * GDN_DK)
        q, k = _gdn_qkv(act[:, 0:w], act[:, w:2 * w], h)
        v = act[:, 2 * w + h * GDN_DK:2 * w + (h + 1) * GDN_DK]
        beta = beta_all[:, L_DB + h:L_DB + h + 1]
        alpha = alpha_all[:, L_DA + h:L_DA + h + 1]
        k_col = _row_to_col(k, eye)
        q_col = _row_to_col(q, eye)
        s0 = s0_ref[h]
        ks = jnp.sum(s0 * k_col, axis=0, keepdims=True)
        u = beta * v - (beta * alpha) * ks
        s1 = alpha * s0 + k_col * u
        s_ref[h] = s1
        o = jnp.sum(s1 * q_col, axis=0, keepdims=True)
        outs.append(_gdn_out(o, nw_ref[...], gate[:, sl]))
    o_ref[...] = jnp.concatenate(outs, axis=1)


def _gdn_step(x, gate, small, conv_buf, s0, prm):
    n = x.shape[0]
    w = GROUP_WIDTH
    full = lambda shape: pl.BlockSpec(shape, lambda i: (0,) * len(shape))
    return pl.pallas_call(
        _gdn_step_kernel,
        grid=(n,),
        in_specs=[pl.BlockSpec((None, 1, 3 * w), lambda i: (i, 0, 0)),
                  pl.BlockSpec((None, 1, w), lambda i: (i, 0, 0)),
                  pl.BlockSpec((None, 1, 128), lambda i: (i, 0, 0)),
                  pl.BlockSpec((None, CONV_W - 1, 3 * w), lambda i: (i, 0, 0)),
                  pl.BlockSpec((None, H_GDN, GDN_DK, GDN_DK), lambda i: (i, 0, 0, 0)),
                  full((CONV_W, 3 * w)), full((1, 128)), full((1, 128)), full((1, GDN_DK))],
        out_specs=[pl.BlockSpec((None, 1, w), lambda i: (i, 0, 0)),
                   pl.BlockSpec((None, H_GDN, GDN_DK, GDN_DK), lambda i: (i, 0, 0, 0))],
        out_shape=[jax.ShapeDtypeStruct((n, 1, w), F32),
                   jax.ShapeDtypeStruct((n, H_GDN, GDN_DK, GDN_DK), F32)],
        compiler_params=_cparams(("parallel",)),
        name="gdn_step",
    )(x, gate, small, conv_buf, s0, prm['gdn_conv_w'], prm['gdn_alog_row'], prm['gdn_dt_row'],
      prm['gdn_norm_w'])


def _dot_row(row, mat, **kw):
    return _dot(jnp.broadcast_to(row, (8, row.shape[1])), mat, **kw)[0:1]


def _fox_step_kernel(pt_ref, *refs, n_pages):
    kv = refs[:n_pages]
    lf = refs[n_pages:2 * n_pages]
    x_ref, sm_ref, bf_ref, o_ref, lf_out_ref = refs[2 * n_pages:]
    w = GROUP_WIDTH
    pg = PAGE_SIZE
    x = x_ref[...]
    q, kn, vn = x[:, 0:w], x[:, w:2 * w], x[:, 2 * w:3 * w]
    logf_new = _log_sigmoid(sm_ref[...] + bf_ref[...])
    lf_out_ref[...] = logf_new
    head_cols = (_iota((w, N_HEADS), 0) // HEAD_DIM == _iota((w, N_HEADS), 1)).astype(BF16)
    head_rows = _iota((N_HEADS, w), 1) // HEAD_DIM == _iota((N_HEADS, w), 0)
    eye = _iota((w, w), 0) == _iota((w, w), 1)
    qblk = _dot(jnp.where(eye, q, 0.0).astype(BF16), head_cols).astype(BF16)
    tri = (_iota((pg, pg), 1) <= _iota((pg, pg), 0)).astype(F32)
    carry = jnp.zeros((1, N_HEADS), F32)
    cs = []
    for p in range(n_pages):
        c = _dot(tri, lf[p][...], precision=HI) + carry
        carry = c[pg - 1:pg, :]
        cs.append(c)
    ctot = carry + logf_new[:, 0:N_HEADS]
    s_n = _dot_row(kn.astype(BF16), qblk) * ATT_SCALE
    m = s_n
    ss = []
    stride = 2 * N_HEADS
    for p in range(n_pages):
        s = None
        for h in range(N_HEADS):
            k_h = kv[p][pl.ds(h, pg, stride=stride), :].astype(BF16)
            part = _dot(k_h, qblk[h * HEAD_DIM:(h + 1) * HEAD_DIM, :])
            s = part if s is None else s + part
        s = s * ATT_SCALE + (ctot - cs[p])
        ss.append(s)
        m = jnp.maximum(m, jnp.max(s, axis=0, keepdims=True))
    e_n = jnp.exp(s_n - m)
    l = e_n
    acc = [jnp.zeros((8, HEAD_DIM), F32) for _ in range(N_HEADS)]
    for p in range(n_pages):
        e = jnp.exp(ss[p] - m)
        l = l + jnp.sum(e, axis=0, keepdims=True)
        for h in range(N_HEADS):
            v_h = kv[p][pl.ds(N_HEADS + h, pg, stride=stride), :]
            acc[h] = acc[h] + jnp.sum((e[:, h:h + 1] * v_h).reshape(pg // 8, 8, HEAD_DIM), axis=0)
    tot = jnp.concatenate([jnp.sum(a, axis=0, keepdims=True) for a in acc], axis=1)
    tot = tot + _dot_row(e_n.astype(BF16), head_rows.astype(BF16)) * vn
    o_ref[...] = tot * _dot_row(1.0 / l, head_rows.astype(F32), precision=HI)


def _fox_step(pt, kv_pool, lf_pool, x, small, bf_row):
    n, n_pages = pt.shape
    w = GROUP_WIDTH
    page = lambda i, rows, width: pl.BlockSpec((None, rows, width), lambda b, t: (t[b, i], 0, 0))
    in_specs = ([page(i, PAGE_SIZE * 2 * N_HEADS, HEAD_DIM) for i in range(n_pages)] +
                [page(i, PAGE_SIZE, N_HEADS) for i in range(n_pages)] +
                [pl.BlockSpec((None, 1, 3 * w), lambda b, t: (b, 0, 0)),
                 pl.BlockSpec((None, 1, 128), lambda b, t: (b, 0, 0)),
                 pl.BlockSpec((1, 128), lambda b, t: (0, 0))])
    return pl.pallas_call(
        functools.partial(_fox_step_kernel, n_pages=n_pages),
        grid_spec=pltpu.PrefetchScalarGridSpec(
            num_scalar_prefetch=1, grid=(n,), in_specs=in_specs,
            out_specs=[pl.BlockSpec((None, 1, w), lambda b, t: (b, 0, 0)),
                       pl.BlockSpec((None, 1, 128), lambda b, t: (b, 0, 0))]),
        out_shape=[jax.ShapeDtypeStruct((n, 1, w), F32), jax.ShapeDtypeStruct((n, 1, 128), F32)],
        compiler_params=_cparams(("parallel",)),
        name="fox_step",
    )(pt, *([kv_pool] * n_pages), *([lf_pool] * n_pages), x, small, bf_row)


def _softmax_cols(scores):
    m = None
    for s in scores:
        mx = jnp.max(s, axis=0, keepdims=True)
        m = mx if m is None else jnp.maximum(m, mx)
    es = [jnp.exp(s - m) for s in scores]
    l = None
    for e in es:
        sm = jnp.sum(e, axis=0, keepdims=True)
        l = sm if l is None else l + sm
    return es, l


def _nsa_step_kernel(pt_ref, *refs, n_pages):
    pgs = refs[:n_pages]
    win_ref, q_ref, kv_ref, wn_ref, sm_ref, wc_ref, o_ref = refs[n_pages:]
    pg = PAGE_SIZE
    w = GROUP_WIDTH
    kvw = KVH_NSA * HEAD_DIM
    past = n_pages * pg
    nbp = past // NSA_BLOCK
    q = q_ref[...]
    r = _iota((kvw, w), 0)
    c = _iota((kvw, w), 1)
    fold = (c % HEAD_DIM == r % HEAD_DIM) & (c // (NSA_GROUP * HEAD_DIM) == r // HEAD_DIM)
    head_cols = (_iota((w, N_HEADS), 0) // HEAD_DIM == _iota((w, N_HEADS), 1)).astype(BF16)
    qn = _dot(jnp.where(fold, q, 0.0).astype(BF16), head_cols).astype(BF16)
    hl = _iota((1, N_HEADS), 1)
    slopes = jnp.exp2(-(hl + 1).astype(F32) * (8.0 / N_HEADS))
    spread = [_iota((N_HEADS, kvw), 0) == NSA_GROUP * (_iota((N_HEADS, kvw), 1) // HEAD_DIM) + i
              for i in range(NSA_GROUP)]
    spread_bf = [s.astype(BF16) for s in spread]
    spread_f = [s.astype(F32) for s in spread]

    def weighted_rows(es, vals):
        out = []
        for i in range(NSA_GROUP):
            tot = None
            for e, v in zip(es, vals):
                if e.shape[0] == 1:
                    part = _dot_row(e.astype(BF16), spread_bf[i]) * v
                else:
                    part = jnp.sum(_dot(e.astype(BF16), spread_bf[i]) * v, axis=0, keepdims=True)
                tot = part if tot is None else tot + part
            out.append(tot)
        return out

    wc = wc_ref[...]
    kc_rows, vc_rows = [], []
    for p in range(n_pages):
        rows = pgs[p][:, 0:2 * kvw]
        for half in range(pg // NSA_BLOCK):
            cm = jnp.sum(rows[half * NSA_BLOCK:(half + 1) * NSA_BLOCK] * wc, axis=0, keepdims=True)
            kc_rows.append(cm[:, 0:kvw])
            vc_rows.append(cm[:, kvw:2 * kvw])
    kcmp = jnp.concatenate(kc_rows, axis=0)
    vcmp = jnp.concatenate(vc_rows, axis=0)
    blk = _iota((nbp, 1), 0)
    dist_c = (past - ((blk + 1) * NSA_BLOCK - 1)).astype(F32)
    s_c = _dot(kcmp.astype(BF16), qn) * ATT_SCALE - slopes * dist_c
    (e_c,), l_c = _softmax_cols([s_c])
    p_c = e_c / l_c
    acc_c = weighted_rows([p_c], [vcmp])

    cur = past // NSA_BLOCK
    forced = (blk == 0) | (blk >= cur - 1)
    eye_b = _iota((nbp, nbp), 0) == _iota((nbp, nbp), 1)
    lower = _iota((nbp, nbp), 1) < _iota((nbp, nbp), 0)
    n_sel = min(NSA_TOPN, nbp + 1)
    sels = []
    for g in range(KVH_NSA):
        imp = p_c[:, NSA_GROUP * g:NSA_GROUP * g + 1]
        for i in range(1, NSA_GROUP):
            imp = imp + p_c[:, NSA_GROUP * g + i:NSA_GROUP * g + i + 1]
        score = jnp.where(forced, FORCED_SCORE, imp)
        score_row = _col_to_row(score, eye_b)
        beats = (score_row > score) | ((score_row == score) & lower)
        cnt = jnp.sum(jnp.where(beats, 1.0, 0.0), axis=1, keepdims=True)
        cnt = cnt + jnp.where(score < FORCED_SCORE, 1.0, 0.0)
        sels.append(jnp.where(cnt < n_sel, 1.0, 0.0))
    sel8 = jnp.where(hl < NSA_GROUP, sels[0], sels[1])

    kv_new = kv_ref[...]
    first_half = _iota((pg, 1), 0) < NSA_BLOCK
    scores, vals = [], []
    for p in range(n_pages):
        dist = (past - (p * pg + _iota((pg, 1), 0))).astype(F32)
        s = _dot(pgs[p][:, 2 * kvw:3 * kvw].astype(BF16), qn) * ATT_SCALE - slopes * dist
        chosen = jnp.where(first_half, sel8[2 * p:2 * p + 1, :], sel8[2 * p + 1:2 * p + 2, :])
        scores.append(jnp.where(chosen > 0.5, s, NEG))
        vals.append(pgs[p][:, 3 * kvw:4 * kvw])
    scores.append(_dot_row(kv_new[:, 2 * kvw:3 * kvw].astype(BF16), qn) * ATT_SCALE)
    vals.append(kv_new[:, 3 * kvw:4 * kvw])
    e_s, l_s = _softmax_cols(scores)
    acc_s = weighted_rows(e_s, vals)

    nw = win_ref.shape[0]
    wrow = _iota((nw, 1), 0)
    dist_w = nw - wrow
    s_w = _dot(win_ref[:, 0:kvw].astype(BF16), qn) * ATT_SCALE - slopes * dist_w.astype(F32)
    s_w = jnp.where((dist_w < NSA_WINDOW) & (past - dist_w >= 0), s_w, NEG)
    wn = wn_ref[...]
    s_wn = _dot_row(wn[:, 0:kvw].astype(BF16), qn) * ATT_SCALE
    e_w, l_w = _softmax_cols([s_w, s_wn])
    acc_w = weighted_rows(e_w, [win_ref[:, kvw:2 * kvw], wn[:, kvw:2 * kvw]])

    gates = jax.nn.sigmoid(sm_ref[...])
    lane = _iota((128, N_HEADS), 0)
    head = _iota((128, N_HEADS), 1)
    coef = []
    for k in range(3):
        pick = (lane == L_NG + 3 * head + k).astype(F32)
        coef.append(_dot_row(gates, pick, precision=HI))
    coef[1] = coef[1] / l_s
    coef[2] = coef[2] / l_w
    tots = []
    for i in range(NSA_GROUP):
        t = _dot_row(coef[0], spread_f[i], precision=HI) * acc_c[i]
        t = t + (_dot_row(coef[1], spread_f[i], precision=HI) * acc_s[i]
                 + _dot_row(coef[2], spread_f[i], precision=HI) * acc_w[i])
        tots.append(t)
    o_ref[...] = jnp.concatenate([t[:, 0:HEAD_DIM] for t in tots] + [t[:, HEAD_DIM:kvw] for t in tots], axis=1)


def _nsa_step(pt, kv_pool, win, q, kv_new, win_new, small, wcmp):
    n, n_pages = pt.shape
    w = GROUP_WIDTH
    nw = win.shape[1]
    row = lambda width: pl.BlockSpec((None, 1, width), lambda b, t: (b, 0, 0))
    in_specs = ([pl.BlockSpec((None, PAGE_SIZE, w), (lambda i: (lambda b, t: (t[b, i], 0, 0)))(i))
                 for i in range(n_pages)] +
                [pl.BlockSpec((None, nw, 256), lambda b, t: (b, 0, 0)),
                 row(w), row(w), row(256), row(128),
                 pl.BlockSpec((NSA_BLOCK, 256), lambda b, t: (0, 0))])
    return pl.pallas_call(
        functools.partial(_nsa_step_kernel, n_pages=n_pages),
        grid_spec=pltpu.PrefetchScalarGridSpec(
            num_scalar_prefetch=1, grid=(n,), in_specs=in_specs,
            out_specs=pl.BlockSpec((None, 1, w), lambda b, t: (b, 0, 0))),
        out_shape=jax.ShapeDtypeStruct((n, 1, w), F32),
        compiler_params=_cparams(("parallel",)),
        name="nsa_step",
    )(pt, *([kv_pool] * n_pages), win, q, kv_new, win_new, small, wcmp)


def _head_cols(row, eye):
    return [_row_to_col(row[:, h * HEAD_DIM:(h + 1) * HEAD_DIM], eye) for h in range(N_HEADS)]


def _rows_to_block(rows):
    sub = _iota((N_HEADS, rows[0].shape[1]), 0)
    out = jnp.broadcast_to(rows[0], sub.shape)
    for h in range(1, N_HEADS):
        out = jnp.where(sub == h, rows[h], out)
    return out


def _tile_scores(tile_of_head, q_cols):
    return _rows_to_block([jnp.sum(tile_of_head(h) * q_cols[h], axis=0, keepdims=True)
                           for h in range(N_HEADS)])


def _fox_step_kernel_t(pt_ref, *refs, n_pages):
    kv = refs[:n_pages]
    lf = refs[n_pages:2 * n_pages]
    x_ref, sm_ref, bf_ref, o_ref, lf_out_ref = refs[2 * n_pages:]
    w = GROUP_WIDTH
    pg = PAGE_SIZE
    hd = HEAD_DIM
    x = x_ref[...]
    q, kn, vn = x[:, 0:w], x[:, w:2 * w], x[:, 2 * w:3 * w]
    logf_new = _log_sigmoid(sm_ref[...] + bf_ref[...])
    lf_out_ref[...] = logf_new
    eye = _iota((hd, hd), 0) == _iota((hd, hd), 1)
    eye8 = _iota((N_HEADS, N_HEADS), 0) == _iota((N_HEADS, N_HEADS), 1)
    q_cols = _head_cols(q, eye)
    vn_cols = _head_cols(vn, eye)
    sub = _iota((N_HEADS, 1), 0)
    upper = (_iota((pg, pg), 0) <= _iota((pg, pg), 1)).astype(F32)
    carry = jnp.zeros((N_HEADS, 1), F32)
    cs = []
    for p in range(n_pages):
        c = _dot(lf[p][...], upper, precision=HI) + carry
        carry = c[:, pg - 1:pg]
        cs.append(c)
    ctot = carry + _row_to_col(logf_new[:, 0:N_HEADS], eye8)
    prod = kn * q
    s_n = jnp.zeros((N_HEADS, 1), F32)
    for h in range(N_HEADS):
        s_n = jnp.where(sub == h, jnp.sum(prod[:, h * hd:(h + 1) * hd], axis=1, keepdims=True), s_n)
    s_n = s_n * ATT_SCALE
    m = s_n
    ss = []
    for p in range(n_pages):
        s = _tile_scores(lambda h: kv[p][0, h], q_cols) * ATT_SCALE + (ctot - cs[p])
        ss.append(s)
        m = jnp.maximum(m, jnp.max(s, axis=1, keepdims=True))
    e_n = jnp.exp(s_n - m)
    l = e_n
    es = []
    for p in range(n_pages):
        e = jnp.exp(ss[p] - m)
        es.append(e)
        l = l + jnp.sum(e, axis=1, keepdims=True)
    inv_l = 1.0 / l
    outs = []
    for h in range(N_HEADS):
        acc = kv[0][1, h] * es[0][h:h + 1, :]
        for p in range(1, n_pages):
            acc = acc + kv[p][1, h] * es[p][h:h + 1, :]
        o_col = jnp.sum(acc, axis=1, keepdims=True) + vn_cols[h] * e_n[h:h + 1, :]
        outs.append(_col_to_row(o_col * inv_l[h:h + 1, :], eye))
    o_ref[...] = jnp.concatenate(outs, axis=1)


def _fox_step_t(pt, layer, kv_t, lf_t, x, small, bf_row):
    n, n_pages = pt.shape
    w = GROUP_WIDTH
    kv_spec = lambda i: pl.BlockSpec((None, None, 2, N_HEADS, HEAD_DIM, PAGE_SIZE),
                                     lambda b, t: (layer, t[b, i], 0, 0, 0, 0))
    lf_spec = lambda i: pl.BlockSpec((None, None, N_HEADS, PAGE_SIZE), lambda b, t: (layer, t[b, i], 0, 0))
    in_specs = ([kv_spec(i) for i in range(n_pages)] + [lf_spec(i) for i in range(n_pages)] +
                [pl.BlockSpec((None, 1, 3 * w), lambda b, t: (b, 0, 0)),
                 pl.BlockSpec((None, 1, 128), lambda b, t: (b, 0, 0)),
                 pl.BlockSpec((1, 128), lambda b, t: (0, 0))])
    return pl.pallas_call(
        functools.partial(_fox_step_kernel_t, n_pages=n_pages),
        grid_spec=pltpu.PrefetchScalarGridSpec(
            num_scalar_prefetch=1, grid=(n,), in_specs=in_specs,
            out_specs=[pl.BlockSpec((None, 1, w), lambda b, t: (b, 0, 0)),
                       pl.BlockSpec((None, 1, 128), lambda b, t: (b, 0, 0))]),
        out_shape=[jax.ShapeDtypeStruct((n, 1, w), F32), jax.ShapeDtypeStruct((n, 1, 128), F32)],
        compiler_params=_cparams(("parallel",)),
        name="fox_step",
    )(pt, *([kv_t] * n_pages), *([lf_t] * n_pages), x, small, bf_row)


def _nsa_step_kernel_t(pt_ref, *refs, n_pages):
    pgs = refs[:n_pages]
    win_ref, q_ref, kv_ref, wn_ref, sm_ref, wc_ref, o_ref = refs[n_pages:]
    pg = PAGE_SIZE
    hd = HEAD_DIM
    nh = N_HEADS
    past = n_pages * pg
    nbp = past // NSA_BLOCK
    bpp = pg // NSA_BLOCK
    eye = _iota((hd, hd), 0) == _iota((hd, hd), 1)
    q_cols = _head_cols(q_ref[...], eye)
    sub = _iota((nh, 1), 0)
    lane = _iota((1, pg), 1)
    slopes = jnp.exp2(-(sub + 1).astype(F32) * (8.0 / nh))
    grp = lambda h: h // NSA_GROUP
    kv_new = kv_ref[...]
    wn = wn_ref[...]
    wc = wc_ref[...]

    def new_scores(row):
        out = jnp.zeros((nh, 1), F32)
        for h in range(nh):
            k_col = _row_to_col(row[:, grp(h) * hd:(grp(h) + 1) * hd], eye)
            out = jnp.where(sub == h, jnp.sum(k_col * q_cols[h], axis=0, keepdims=True), out)
        return out * ATT_SCALE

    def softmax_lanes(scores, extra=None):
        m = extra
        for s in scores:
            mx = jnp.max(s, axis=1, keepdims=True)
            m = mx if m is None else jnp.maximum(m, mx)
        es = [jnp.exp(s - m) for s in scores]
        l = None if extra is None else jnp.exp(extra - m)
        e_extra = l
        for e in es:
            sm = jnp.sum(e, axis=1, keepdims=True)
            l = sm if l is None else l + sm
        return es, e_extra, l

    def weighted_values(tiles, es):
        cols = []
        for h in range(nh):
            acc = None
            for tile, e in zip(tiles, es):
                part = jnp.sum(tile(grp(h)) * e[h:h + 1, :], axis=1, keepdims=True)
                acc = part if acc is None else acc + part
            cols.append(acc)
        return cols

    half = [lane // NSA_BLOCK == i for i in range(bpp)]
    blk_lane = _iota((1, pg), 1)
    s_c = jnp.full((nh, pg), NEG, F32)
    for p in range(n_pages):
        raw = _tile_scores(lambda h: pgs[p][0, grp(h)], q_cols) * wc[0:1, :]
        for i in range(bpp):
            r = jnp.sum(jnp.where(half[i], raw, 0.0), axis=1, keepdims=True)
            s_c = jnp.where(blk_lane == bpp * p + i, r, s_c)
    dist_c = (past - ((blk_lane + 1) * NSA_BLOCK - 1)).astype(F32)
    s_c = jnp.where(blk_lane < nbp, s_c * ATT_SCALE - slopes * dist_c, NEG)
    (e_c,), _, l_c = softmax_lanes([s_c])
    p_c = e_c / l_c
    pes = []
    for p in range(n_pages):
        pe = jnp.zeros((nh, pg), F32)
        for i in range(bpp):
            pe = jnp.where(half[i], p_c[:, bpp * p + i:bpp * p + i + 1], pe)
        pes.append(pe * wc[1:2, :])
    o_c = weighted_values([(lambda g, p=p: pgs[p][1, g]) for p in range(n_pages)], pes)

    nbl = 32 * ((nbp + 31) // 32)
    eye_b = _iota((nbl, nbl), 0) == _iota((nbl, nbl), 1)
    m_lt_n = _iota((nbl, nbl), 0) < _iota((nbl, nbl), 1)
    blk_row = _iota((1, nbl), 1)
    n_sel = min(NSA_TOPN, nbp + 1)
    sel_rows = []
    for g in range(KVH_NSA):
        imp = jnp.sum(jnp.where(sub // NSA_GROUP == g, p_c, 0.0), axis=0, keepdims=True)[:, 0:nbl]
        forced = (blk_row == 0) | (blk_row >= nbp - 1)
        score = jnp.where(blk_row < nbp, jnp.where(forced, FORCED_SCORE, imp), -1.0)
        score_col = _row_to_col(score, eye_b)
        beats = jnp.where(score_col > score, 1.0, jnp.where((score_col == score) & m_lt_n, 1.0, 0.0))
        cnt = jnp.sum(beats, axis=0, keepdims=True)
        cnt = cnt + jnp.where(score < FORCED_SCORE, 1.0, 0.0)
        sel_rows.append(jnp.where((cnt < n_sel) & (score >= 0.0), 1.0, 0.0))

    scores = []
    for p in range(n_pages):
        dist = (past - (p * pg + lane)).astype(F32)
        s = _tile_scores(lambda h: pgs[p][2, grp(h)], q_cols) * ATT_SCALE - slopes * dist
        chosen = []
        for g in range(KVH_NSA):
            row = jnp.zeros((1, pg), F32)
            for i in range(bpp):
                row = jnp.where(half[i], sel_rows[g][:, bpp * p + i:bpp * p + i + 1], row)
            chosen.append(row)
        mask = jnp.where(sub // NSA_GROUP == 0, chosen[0], chosen[1])
        scores.append(jnp.where(mask > 0.5, s, NEG))
    s_new = new_scores(kv_new[:, 2 * 128:3 * 128])
    e_s, e_sn, l_s = softmax_lanes(scores, s_new)
    o_s = weighted_values([(lambda g, p=p: pgs[p][3, g]) for p in range(n_pages)], e_s)
    v_new = kv_new[:, 3 * 128:4 * 128]
    o_s = [o_s[h] + _row_to_col(v_new[:, grp(h) * hd:(grp(h) + 1) * hd], eye) * e_sn[h:h + 1, :]
           for h in range(nh)]

    nw = win_ref.shape[-1]
    scores = []
    for c0 in range(0, nw, pg):
        dist = nw - (c0 + lane)
        s = _tile_scores(lambda h: win_ref[0, grp(h), :, c0:c0 + pg], q_cols) * ATT_SCALE
        s = s - slopes * dist.astype(F32)
        scores.append(jnp.where((dist < NSA_WINDOW) & (past - dist >= 0), s, NEG))
    s_new = new_scores(wn[:, 0:128])
    e_w, e_wn, l_w = softmax_lanes(scores, s_new)
    o_w = weighted_values([(lambda g, c0=c0: win_ref[1, g, :, c0:c0 + pg]) for c0 in range(0, nw, pg)], e_w)
    vw_new = wn[:, 128:256]
    o_w = [o_w[h] + _row_to_col(vw_new[:, grp(h) * hd:(grp(h) + 1) * hd], eye) * e_wn[h:h + 1, :]
           for h in range(nh)]

    gates = jax.nn.sigmoid(sm_ref[...])
    outs = []
    for h in range(nh):
        c = L_NG + 3 * h
        o = gates[:, c:c + 1] * o_c[h]
        o = o + ((gates[:, c + 1:c + 2] / l_s[h:h + 1, :]) * o_s[h]
                 + (gates[:, c + 2:c + 3] / l_w[h:h + 1, :]) * o_w[h])
        outs.append(_col_to_row(o, eye))
    o_ref[...] = jnp.concatenate(outs, axis=1)


def _nsa_step_t(pt, layer, kv_t, win_t, q, kv_new, win_new, small, wc_rows):
    n, n_pages = pt.shape
    w = GROUP_WIDTH
    nw = win_t.shape[-1]
    row = lambda width: pl.BlockSpec((None, 1, width), lambda b, t: (b, 0, 0))
    page = lambda i: pl.BlockSpec((None, None, 4, KVH_NSA, HEAD_DIM, PAGE_SIZE),
                                  lambda b, t: (layer, t[b, i], 0, 0, 0, 0))
    in_specs = ([page(i) for i in range(n_pages)] +
                [pl.BlockSpec((None, None, 2, KVH_NSA, HEAD_DIM, nw), lambda b, t: (layer, b, 0, 0, 0, 0)),
                 row(w), row(w), row(256), row(128),
                 pl.BlockSpec((2, PAGE_SIZE), lambda b, t: (0, 0))])
    return pl.pallas_call(
        functools.partial(_nsa_step_kernel_t, n_pages=n_pages),
        grid_spec=pltpu.PrefetchScalarGridSpec(
            num_scalar_prefetch=1, grid=(n,), in_specs=in_specs,
            out_specs=pl.BlockSpec((None, 1, w), lambda b, t: (b, 0, 0))),
        out_shape=jax.ShapeDtypeStruct((n, 1, w), F32),
        compiler_params=_cparams(("parallel",)),
        name="nsa_step",
    )(pt, *([kv_t] * n_pages), win_t, q, kv_new, win_new, small, wc_rows)


_O_FF, _O_LX, _O_NQ, _O_NKC, _O_NG, _O_DQKV, _O_DG, _O_DA, _O_END = (
    1536, 1544, 2568, 3080, 3848, 3872, 5408, 5920, 5928)


def _pack_w_in(w):
    d = w.shape[0]
    zeros = lambda n: jnp.zeros((d, n), w.dtype)
    parts = [w[:, 0:_O_FF], w[:, _O_LX:_O_NQ], w[:, _O_NQ:_O_NKC], w[:, _O_NKC:_O_NG],
             w[:, _O_FF:_O_LX], w[:, _O_NG:_O_DQKV], w[:, _O_DA:_O_END], zeros(128 - 40), zeros(128),
             w[:, _O_DQKV:_O_DG], w[:, _O_DG:_O_DA]]
    out = jnp.concatenate(parts, axis=1)
    assert out.shape[1] == N_PROJ
    return out.astype(BF16)


def _block_diag(w):
    n, d, e = w.shape
    eye = jnp.eye(n, dtype=w.dtype)
    return (eye[:, None, :, None] * w[:, :, None, :]).reshape(n * d, n * e)


def _lane_row(vals, start):
    return jnp.zeros((1, 128), F32).at[0, start:start + vals.shape[0]].set(vals)


def _layer_params(l, p):
    row = lambda a: a[l][None, :]
    return dict(
        norm_pre_mix=row(p['norm_pre_mix']), norm_post_mix=row(p['norm_post_mix']),
        norm_pre_mlp=row(p['norm_pre_mlp']), norm_post_mlp=row(p['norm_post_mlp']),
        w_in=_pack_w_in(p['w_in'][l]), w_out=p['w_out'][l].astype(BF16),
        w_up=p['w_up'][l].astype(BF16), w_down=p['w_down'][l].astype(BF16),
        fox_bf_row=_lane_row(p['fox_b_f'][l], L_FF),
        lru_conv_w=p['lru_conv_w'][l], lru_conv_b=row(p['lru_conv_b']),
        lru_wr=_block_diag(p['lru_w_r'][l]).astype(BF16), lru_b_r=row(p['lru_b_r']),
        lru_wi=_block_diag(p['lru_w_i'][l]).astype(BF16), lru_b_i=row(p['lru_b_i']),
        lru_lambda=row(p['lru_lambda']),
        nsa_wcmp=jnp.repeat(p['nsa_w_cmp'][l].T, 128, axis=1),
        gdn_conv_w=p['gdn_conv_w'][l],
        gdn_alog_row=_lane_row(p['gdn_A_log'][l], L_DA), gdn_dt_row=_lane_row(p['gdn_dt_bias'][l], L_DA),
        gdn_norm_w=row(p['gdn_norm_w']))


def kernel(x_prompt, x_sample, cache_fox_kv, cache_fox_logf, cache_nsa_kv, cache_nsa_win,
           state_rglru_conv, state_rglru_h, state_gdn_conv, state_gdn_S, page_table,
           norm_pre_mix, norm_post_mix, norm_pre_mlp, norm_post_mlp, w_in, w_out, w_up, w_down,
           fox_b_f, lru_conv_w, lru_conv_b, lru_w_r, lru_b_r, lru_w_i, lru_b_i, lru_lambda,
           nsa_w_cmp, gdn_conv_w, gdn_A_log, gdn_dt_bias, gdn_norm_w):
    params = dict(norm_pre_mix=norm_pre_mix, norm_post_mix=norm_post_mix, norm_pre_mlp=norm_pre_mlp,
                  norm_post_mlp=norm_post_mlp, w_in=w_in, w_out=w_out, w_up=w_up, w_down=w_down,
                  fox_b_f=fox_b_f, lru_conv_w=lru_conv_w, lru_conv_b=lru_conv_b, lru_w_r=lru_w_r,
                  lru_b_r=lru_b_r, lru_w_i=lru_w_i, lru_b_i=lru_b_i, lru_lambda=lru_lambda,
                  nsa_w_cmp=nsa_w_cmp, gdn_conv_w=gdn_conv_w, gdn_A_log=gdn_A_log,
                  gdn_dt_bias=gdn_dt_bias, gdn_norm_w=gdn_norm_w)
    b, t, d = x_prompt.shape
    ns = x_sample.shape[0]
    depth, n_pool = cache_fox_kv.shape[:2]
    bt = b * t
    w = GROUP_WIDTH
    xp = x_prompt.reshape(bt, d)
    xs = x_sample.reshape(ns, d)
    fox_t = jnp.transpose(cache_fox_kv, (0, 1, 3, 4, 5, 2))
    logf_t = jnp.transpose(cache_fox_logf, (0, 1, 3, 2))
    nsa_t = jnp.transpose(cache_nsa_kv, (0, 1, 3, 4, 5, 2))
    win_t = jnp.transpose(cache_nsa_win, (0, 1, 3, 4, 5, 2))
    outs = [[] for _ in range(16)]
    for l in range(depth):
        prm = _layer_params(l, params)
        proj = _in_proj(xp, prm['norm_pre_mix'], prm['w_in'])
        ps = _in_proj(xs, prm['norm_pre_mix'], prm['w_in'])
        pp = proj.reshape(b, t, N_PROJ)
        row3 = lambda c0, c1: ps[:, None, c0:c1]

        qa, ka, va, logf_p = _fox_prep(proj, prm['fox_bf_row'], b, t)
        oa_p = _fox_flash(qa, ka, va, b, t)
        oa_s, logf_s = _fox_step_t(page_table, l, fox_t, logf_t, row3(C_FQ, C_FQ + 3 * w),
                                   row3(C_SMALL, C_SMALL + 128), prm['fox_bf_row'])

        ob_p, h_p = _lru_prompt(proj, prm, b, t)
        ob_s, h_s = _lru_step(ps[:, C_LX:C_LX + w], ps[:, C_LG:C_LG + w],
                              jnp.moveaxis(state_rglru_conv[l], 1, 0), state_rglru_h[l], prm)

        cmp = _nsa_cmp(proj, prm['nsa_wcmp'], b, t)
        oc_p = _nsa_prompt(proj, cmp, b, t)
        oc_s = _nsa_step_t(page_table, l, nsa_t, win_t, row3(C_NQ, C_NQ + w), row3(C_NKV, C_NKV + w),
                           row3(C_NWIN, C_NWIN + 256), row3(C_SMALL, C_SMALL + 128),
                           jnp.tile(params['nsa_w_cmp'][l], (1, PAGE_SIZE // NSA_BLOCK)))

        od_p, s_p = _gdn_prompt(proj, prm, b, t)
        od_s, s_s = _gdn_step(row3(C_DQKV, C_DQKV + 3 * w), row3(C_DG, C_DG + w),
                              row3(C_SMALL, C_SMALL + 128), state_gdn_conv[l], state_gdn_S[l], prm)

        xp = _out_proj(oa_p, ob_p, oc_p, od_p, xp, prm['w_out'], prm['norm_post_mix'])
        xp = _mlp(xp, prm['norm_pre_mlp'], prm['w_up'], prm['w_down'], prm['norm_post_mlp'])
        flat = lambda a: a.reshape(ns, w)
        xs = _out_proj(flat(oa_s), ob_s, flat(oc_s), flat(od_s), xs, prm['w_out'], prm['norm_post_mix'])
        xs = _mlp(xs, prm['norm_pre_mlp'], prm['w_up'], prm['w_down'], prm['norm_post_mlp'])

        new_win_s = ps[:, C_NWIN:C_NWIN + 256].reshape(ns, 1, 2, KVH_NSA, HEAD_DIM)
        win_all = jnp.concatenate([cache_nsa_win[l], new_win_s], axis=1)
        layer_out = [
            pp[:, :, C_FK:C_FK + 2 * w].reshape(b, t, 2, N_HEADS, HEAD_DIM),
            ps[:, C_FK:C_FK + 2 * w].reshape(ns, 1, 2, N_HEADS, HEAD_DIM),
            logf_p[:, :N_HEADS].reshape(b, t, N_HEADS),
            logf_s[:, :, :N_HEADS],
            pp[:, :, C_NKV:C_NKV + w].reshape(b, t, 4, KVH_NSA, HEAD_DIM),
            ps[:, C_NKV:C_NKV + w].reshape(ns, 1, 4, KVH_NSA, HEAD_DIM),
            pp[:, t - min(NSA_WINDOW, t):, C_NWIN:C_NWIN + 256].reshape(b, min(NSA_WINDOW, t), 2, KVH_NSA, HEAD_DIM),
            win_all[:, -NSA_WINDOW:],
            pp[:, t - (CONV_W - 1):, C_LX:C_LX + w],
            jnp.concatenate([state_rglru_conv[l][:, 1:], ps[:, None, C_LX:C_LX + w]], axis=1),
            h_p[:, 0],
            h_s,
            pp[:, t - (CONV_W - 1):, C_DQKV:C_DQKV + 3 * w],
            jnp.concatenate([state_gdn_conv[l][:, 1:], ps[:, None, C_DQKV:C_DQKV + 3 * w]], axis=1),
            s_p,
            s_s,
        ]
        for i, a in enumerate(layer_out):
            outs[i].append(a)
    y_prompt = xp.reshape(b, t, d)
    y_sample = xs.reshape(ns, 1, d)
    return (y_prompt, y_sample) + tuple(jnp.stack(o) for o in outs)
```

```python
import functools

import jax
import jax.numpy as jnp
from jax import lax
from jax.experimental import pallas as pl
from jax.experimental.pallas import tpu as pltpu

F32 = jnp.float32
BF16 = jnp.bfloat16
HI = lax.Precision.HIGHEST

D_MODEL = 2048
GROUP_WIDTH = 512
HEAD_DIM = 64
N_HEADS = 8
KVH_NSA = 2
NSA_GROUP = 4
NSA_BLOCK = 64
NSA_TOPN = 16
NSA_WINDOW = 512
FORCED_SCORE = 1e4
H_GDN = 4
GDN_DK = 128
GDN_CHUNK = 64
LRU_C = 8.0
CONV_W = 4
D_FF = 4 * D_MODEL
RMS_EPS = 1e-6
NEG = -1e30
ATT_SCALE = HEAD_DIM ** -0.5
PAGE_SIZE = 128

C_FQ, C_FK, C_FV = 0, 512, 1024
C_LX, C_LG = 1536, 2048
C_NQ = 2560
C_NKV = 3072
C_NWIN = 3584
C_SMALL = 3840
C_DQKV = 4096
C_DG = 5632
N_PROJ = 6144
L_FF, L_NG, L_DA, L_DB = 0, 8, 32, 36

VMEM_LIMIT = 56 * 1024 * 1024


def _cparams(sem):
    return pltpu.CompilerParams(dimension_semantics=sem, vmem_limit_bytes=VMEM_LIMIT)


def _pick(n, prefs):
    for p in prefs:
        if n % p == 0:
            return p
    return n


def _rms(xf, g):
    return xf * lax.rsqrt(jnp.mean(xf * xf, axis=-1, keepdims=True) + RMS_EPS) * g


def _softplus(x):
    return jnp.maximum(x, 0.0) + jnp.log1p(jnp.exp(-jnp.abs(x)))


def _log_sigmoid(x):
    return -_softplus(-x)


def _iota(shape, dim):
    return lax.broadcasted_iota(jnp.int32, shape, dim)


def _dot(a, b, **kw):
    return jnp.dot(a, b, preferred_element_type=F32, **kw)


def _dot3(a, b):
    a_hi = a.astype(BF16)
    b_hi = b.astype(BF16)
    a_lo = (a - a_hi.astype(F32)).astype(BF16)
    b_lo = (b - b_hi.astype(F32)).astype(BF16)
    return _dot(a_hi, b_hi) + (_dot(a_hi, b_lo) + _dot(a_lo, b_hi))


def _dot_nt(a, b):
    return lax.dot_general(a, b, (((1,), (1,)), ((), ())), preferred_element_type=F32)


def _dot_tn(a, b):
    return lax.dot_general(a, b, (((0,), (0,)), ((), ())), preferred_element_type=F32)


def _in_proj_kernel(x_ref, g_ref, w_ref, o_ref, h_ref):
    @pl.when(pl.program_id(1) == 0)
    def _():
        h_ref[...] = _rms(x_ref[...], g_ref[...]).astype(BF16)

    o_ref[...] = _dot(h_ref[...], w_ref[...])


def _in_proj(x, g, w):
    m, d = x.shape
    n = w.shape[1]
    tm = _pick(m, (1024, 512, 256, 128, 64, 32, 16, 8))
    tn = _pick(n, (1536, 1024, 768, 512, 256, 128))
    return pl.pallas_call(
        _in_proj_kernel,
        grid=(m // tm, n // tn),
        in_specs=[pl.BlockSpec((tm, d), lambda i, j: (i, 0)),
                  pl.BlockSpec((1, d), lambda i, j: (0, 0)),
                  pl.BlockSpec((d, tn), lambda i, j: (0, j))],
        out_specs=pl.BlockSpec((tm, tn), lambda i, j: (i, j)),
        out_shape=jax.ShapeDtypeStruct((m, n), F32),
        scratch_shapes=[pltpu.VMEM((tm, d), BF16)],
        compiler_params=_cparams(("parallel", "arbitrary")),
        name="in_proj",
    )(x, g, w)


def _out_proj_kernel(oa_ref, ob_ref, oc_ref, od_ref, x_ref, w_ref, g_ref, o_ref):
    y = _dot(oa_ref[...].astype(BF16), w_ref[0:512, :])
    y = y + _dot(ob_ref[...].astype(BF16), w_ref[512:1024, :])
    y = y + _dot(oc_ref[...].astype(BF16), w_ref[1024:1536, :])
    y = y + _dot(od_ref[...].astype(BF16), w_ref[1536:2048, :])
    o_ref[...] = x_ref[...] + _rms(y, g_ref[...])


def _out_proj(oa, ob, oc, od, x, w, g):
    m, d = x.shape
    tm = _pick(m, (640, 512, 256, 128, 64, 32, 16, 8))
    gw = GROUP_WIDTH
    return pl.pallas_call(
        _out_proj_kernel,
        grid=(m // tm,),
        in_specs=[pl.BlockSpec((tm, gw), lambda i: (i, 0))] * 4 + [
            pl.BlockSpec((tm, d), lambda i: (i, 0)),
            pl.BlockSpec((d, d), lambda i: (0, 0)),
            pl.BlockSpec((1, d), lambda i: (0, 0))],
        out_specs=pl.BlockSpec((tm, d), lambda i: (i, 0)),
        out_shape=jax.ShapeDtypeStruct((m, d), F32),
        compiler_params=_cparams(("parallel",)),
        name="out_proj",
    )(oa, ob, oc, od, x, w, g)


def _mlp_kernel(x_ref, g1_ref, wu_ref, wd_ref, g2_ref, o_ref, h_ref, acc_ref):
    f = pl.program_id(1)

    @pl.when(f == 0)
    def _():
        h_ref[...] = _rms(x_ref[...], g1_ref[...]).astype(BF16)
        acc_ref[...] = jnp.zeros_like(acc_ref)

    u = jnp.maximum(_dot(h_ref[...], wu_ref[...]), 0.0)
    acc_ref[...] += _dot((u * u).astype(BF16), wd_ref[...])

    @pl.when(f == pl.num_programs(1) - 1)
    def _():
        o_ref[...] = x_ref[...] + _rms(acc_ref[...], g2_ref[...])


def _mlp(x, g1, wu, wd, g2):
    m, d = x.shape
    ff = wu.shape[1]
    tm = _pick(m, (640, 512, 256, 128, 64, 32, 16, 8))
    tf = _pick(ff, (512, 256, 128))
    return pl.pallas_call(
        _mlp_kernel,
        grid=(m // tm, ff // tf),
        in_specs=[pl.BlockSpec((tm, d), lambda i, f: (i, 0)),
                  pl.BlockSpec((1, d), lambda i, f: (0, 0)),
                  pl.BlockSpec((d, tf), lambda i, f: (0, f)),
                  pl.BlockSpec((tf, d), lambda i, f: (f, 0)),
                  pl.BlockSpec((1, d), lambda i, f: (0, 0))],
        out_specs=pl.BlockSpec((tm, d), lambda i, f: (i, 0)),
        out_shape=jax.ShapeDtypeStruct((m, d), F32),
        scratch_shapes=[pltpu.VMEM((tm, d), BF16), pltpu.VMEM((tm, d), F32)],
        compiler_params=_cparams(("parallel", "arbitrary")),
        name="mlp",
    )(x, g1, wu, wd, g2)


def _fox_prep_kernel(x_ref, s_ref, bf_ref, qa_ref, ka_ref, v_ref, logf_ref, carry_ref, *, tb):
    @pl.when(pl.program_id(1) == 0)
    def _():
        carry_ref[...] = jnp.zeros_like(carry_ref)

    logf = _log_sigmoid(s_ref[...] + bf_ref[...])
    logf_ref[...] = logf
    tri = (_iota((tb, tb), 1) <= _iota((tb, tb), 0)).astype(F32)
    c = _dot(tri, logf, precision=HI) + carry_ref[...]
    carry_ref[...] = c[tb - 1:tb, :]
    hi = c.astype(BF16).astype(F32)
    r1 = c - hi
    mid = r1.astype(BF16).astype(F32)
    lo = r1 - mid
    lane = _iota((tb, HEAD_DIM), 1)
    q_tail = jnp.where(lane < 3, 1.0, 0.0).astype(BF16)
    x = x_ref[...]
    w = GROUP_WIDTH
    for h in range(N_HEADS):
        sl = slice(h * HEAD_DIM, (h + 1) * HEAD_DIM)
        qa_ref[h] = jnp.concatenate([(x[:, sl] * ATT_SCALE).astype(BF16), q_tail], axis=1)
        k_tail = jnp.where(lane == 0, -hi[:, h:h + 1],
                           jnp.where(lane == 1, -mid[:, h:h + 1],
                                     jnp.where(lane == 2, -lo[:, h:h + 1], 0.0)))
        ka_ref[h] = jnp.concatenate([x[:, w + h * HEAD_DIM:w + (h + 1) * HEAD_DIM].astype(BF16),
                                     k_tail.astype(BF16)], axis=1)
        v_ref[h] = x[:, 2 * w + h * HEAD_DIM:2 * w + (h + 1) * HEAD_DIM].astype(BF16)


def _fox_prep(proj, bf_row, b, t):
    tb = _pick(t, (512, 256, 128, 64))
    nt = t // tb
    w = GROUP_WIDTH
    aug = jax.ShapeDtypeStruct((b, N_HEADS, t, 128), BF16)
    return pl.pallas_call(
        functools.partial(_fox_prep_kernel, tb=tb),
        grid=(b, nt),
        in_specs=[pl.BlockSpec((tb, 3 * w), lambda i, j: (i * nt + j, C_FQ // (3 * w))),
                  pl.BlockSpec((tb, 128), lambda i, j: (i * nt + j, C_SMALL // 128)),
                  pl.BlockSpec((1, 128), lambda i, j: (0, 0))],
        out_specs=[pl.BlockSpec((None, N_HEADS, tb, 128), lambda i, j: (i, 0, j, 0)),
                   pl.BlockSpec((None, N_HEADS, tb, 128), lambda i, j: (i, 0, j, 0)),
                   pl.BlockSpec((None, N_HEADS, tb, HEAD_DIM), lambda i, j: (i, 0, j, 0)),
                   pl.BlockSpec((tb, 128), lambda i, j: (i * nt + j, 0))],
        out_shape=[aug, aug, jax.ShapeDtypeStruct((b, N_HEADS, t, HEAD_DIM), BF16),
                   jax.ShapeDtypeStruct((b * t, 128), F32)],
        scratch_shapes=[pltpu.VMEM((1, 128), F32)],
        compiler_params=_cparams(("parallel", "arbitrary")),
        name="fox_prep",
    )(proj, proj, bf_row)


def _fox_flash_kernel(qi_ref, ki_ref, q_ref, k_ref, v_ref, o_ref, m_ref, l_ref, acc_ref, *, nt):
    step_id = pl.program_id(2)
    qi = qi_ref[step_id]
    ki = ki_ref[step_id]

    @pl.when(ki == 0)
    def _():
        m_ref[...] = jnp.full_like(m_ref, NEG)
        l_ref[...] = jnp.zeros_like(l_ref)
        acc_ref[...] = jnp.zeros_like(acc_ref)

    def step(diag):
        for j in range(2):
            s = _dot_nt(q_ref[j], k_ref[j])
            if diag:
                s = jnp.where(_iota((nt, nt), 1) <= _iota((nt, nt), 0), s, NEG)
            m_prev = m_ref[j]
            m_new = jnp.maximum(m_prev, jnp.max(s, axis=-1, keepdims=True))
            alpha = jnp.exp(m_prev - m_new)
            p = jnp.exp(s - m_new)
            l_ref[j] = alpha * l_ref[j] + jnp.sum(p, axis=-1, keepdims=True)
            acc_ref[j] = alpha * acc_ref[j] + _dot(p.astype(BF16), v_ref[j])
            m_ref[j] = m_new

    @pl.when(ki < qi)
    def _():
        step(False)

    @pl.when(ki == qi)
    def _():
        step(True)
        o_ref[...] = jnp.concatenate([acc_ref[j] / l_ref[j] for j in range(2)], axis=1)


def _causal_pairs(n):
    qs = [q for q in range(n) for _ in range(q + 1)]
    ks = [k for q in range(n) for k in range(q + 1)]
    return jnp.asarray(qs, jnp.int32), jnp.asarray(ks, jnp.int32)


def _fox_flash(qa, ka, v, b, t):
    nt = _pick(t, (1024, 512, 256, 128))
    n = t // nt
    qi, ki = _causal_pairs(n)
    hp = N_HEADS // 2
    return pl.pallas_call(
        functools.partial(_fox_flash_kernel, nt=nt),
        grid_spec=pltpu.PrefetchScalarGridSpec(
            num_scalar_prefetch=2, grid=(b, hp, qi.shape[0]),
            in_specs=[pl.BlockSpec((None, 2, nt, 128), lambda i, h, s, qr, kr: (i, h, qr[s], 0)),
                      pl.BlockSpec((None, 2, nt, 128), lambda i, h, s, qr, kr: (i, h, kr[s], 0)),
                      pl.BlockSpec((None, 2, nt, HEAD_DIM), lambda i, h, s, qr, kr: (i, h, kr[s], 0))],
            out_specs=pl.BlockSpec((nt, 128), lambda i, h, s, qr, kr: (i * n + qr[s], h)),
            scratch_shapes=[pltpu.VMEM((2, nt, 1), F32), pltpu.VMEM((2, nt, 1), F32),
                            pltpu.VMEM((2, nt, HEAD_DIM), F32)]),
        out_shape=jax.ShapeDtypeStruct((b * t, GROUP_WIDTH), F32),
        compiler_params=_cparams(("parallel", "parallel", "arbitrary")),
        name="fox_flash",
    )(qi, ki, qa, ka, v)


def _lru_gates(xc, wr, br, wi, bi, lam):
    xb = xc.astype(BF16)
    r = jax.nn.sigmoid(_dot(xb, wr) + br)
    ig = jax.nn.sigmoid(_dot(xb, wi) + bi)
    log_a = -LRU_C * r * _softplus(-lam)
    a = jnp.exp(log_a)
    th = jnp.tanh(log_a)
    u = jnp.sqrt(-2.0 * th / (1.0 - th)) * (ig * xc)
    return a, u


def _conv4(w_ref, x0, x1, x2, x3, cols=slice(None)):
    y = 0.0 + w_ref[0:1, cols] * x0
    y = y + w_ref[1:2, cols] * x1
    y = y + w_ref[2:3, cols] * x2
    return y + w_ref[3:4, cols] * x3


def _lru_prompt_kernel(x_ref, g_ref, cw_ref, cb_ref, wr_ref, br_ref, wi_ref, bi_ref, lam_ref,
                       y_ref, hfin_ref, xbuf, a_s, u_s, hs, h_s, *, tb):
    @pl.when(pl.program_id(1) == 0)
    def _():
        xbuf[0:8, :] = jnp.zeros((8, GROUP_WIDTH), F32)
        h_s[...] = jnp.zeros_like(h_s)

    xbuf[8:8 + tb, :] = x_ref[...]
    xc = _conv4(cw_ref, xbuf[5:5 + tb, :], xbuf[6:6 + tb, :], xbuf[7:7 + tb, :], xbuf[8:8 + tb, :])
    xc = xc + cb_ref[...]
    xbuf[0:8, :] = xbuf[tb:tb + 8, :]
    a, u = _lru_gates(xc, wr_ref[...], br_ref[...], wi_ref[...], bi_ref[...], lam_ref[...])
    a_s[...] = a
    u_s[...] = u

    def body(t, h):
        h = a_s[pl.ds(t, 1), :] * h + u_s[pl.ds(t, 1), :]
        hs[pl.ds(t, 1), :] = h
        return h

    h = lax.fori_loop(0, tb, body, h_s[...], unroll=8)
    h_s[...] = h
    y_ref[...] = hs[...] * jax.nn.gelu(g_ref[...])
    hfin_ref[...] = jnp.broadcast_to(h, (8, GROUP_WIDTH))


def _lru_prompt(proj, prm, b, t):
    tb = _pick(t, (512, 256, 128, 64))
    nt = t // tb
    w = GROUP_WIDTH
    full = lambda shape: pl.BlockSpec(shape, lambda i, j: (0,) * len(shape))
    return pl.pallas_call(
        functools.partial(_lru_prompt_kernel, tb=tb),
        grid=(b, nt),
        in_specs=[pl.BlockSpec((tb, w), lambda i, j: (i * nt + j, C_LX // w)),
                  pl.BlockSpec((tb, w), lambda i, j: (i * nt + j, C_LG // w)),
                  full((CONV_W, w)), full((1, w)), full((w, w)), full((1, w)), full((w, w)),
                  full((1, w)), full((1, w))],
        out_specs=[pl.BlockSpec((tb, w), lambda i, j: (i * nt + j, 0)),
                   pl.BlockSpec((None, 8, w), lambda i, j: (i, 0, 0))],
        out_shape=[jax.ShapeDtypeStruct((b * t, w), F32), jax.ShapeDtypeStruct((b, 8, w), F32)],
        scratch_shapes=[pltpu.VMEM((tb + 8, w), F32), pltpu.VMEM((tb, w), F32), pltpu.VMEM((tb, w), F32),
                        pltpu.VMEM((tb, w), F32), pltpu.VMEM((1, w), F32)],
        compiler_params=_cparams(("parallel", "arbitrary")),
        name="lru_prompt",
    )(proj, proj, prm['lru_conv_w'], prm['lru_conv_b'], prm['lru_wr'], prm['lru_b_r'], prm['lru_wi'],
      prm['lru_b_i'], prm['lru_lambda'])


def _lru_step_kernel(x_ref, g_ref, buf_ref, h0_ref, cw_ref, cb_ref, wr_ref, br_ref, wi_ref, bi_ref,
                     lam_ref, y_ref, h_ref):
    xc = _conv4(cw_ref, buf_ref[0], buf_ref[1], buf_ref[2], x_ref[...]) + cb_ref[...]
    a, u = _lru_gates(xc, wr_ref[...], br_ref[...], wi_ref[...], bi_ref[...], lam_ref[...])
    h = a * h0_ref[...] + u
    h_ref[...] = h
    y_ref[...] = h * jax.nn.gelu(g_ref[...])


def _lru_step(x, g, buf, h0, prm):
    n, w = x.shape
    out = jax.ShapeDtypeStruct((n, w), F32)
    return pl.pallas_call(
        _lru_step_kernel, out_shape=[out, out], name="lru_step",
        compiler_params=pltpu.CompilerParams(vmem_limit_bytes=VMEM_LIMIT),
    )(x, g, buf, h0, prm['lru_conv_w'], prm['lru_conv_b'], prm['lru_wr'], prm['lru_b_r'], prm['lru_wi'],
      prm['lru_b_i'], prm['lru_lambda'])


def _nsa_cmp_kernel(kv_ref, w_ref, o_ref, *, tb):
    x = kv_ref[...].reshape(tb // NSA_BLOCK, NSA_BLOCK, 256)
    o_ref[...] = jnp.sum(x * w_ref[...][None], axis=1)


def _nsa_cmp(proj, wcmp, b, t):
    tb = _pick(t, (512,))
    nt = t // tb
    nb = tb // NSA_BLOCK
    return pl.pallas_call(
        functools.partial(_nsa_cmp_kernel, tb=tb),
        grid=(b, nt),
        in_specs=[pl.BlockSpec((tb, 256), lambda i, j: (i * nt + j, C_NKV // 256)),
                  pl.BlockSpec((NSA_BLOCK, 256), lambda i, j: (0, 0))],
        out_specs=pl.BlockSpec((None, nb, 256), lambda i, j: (i, j, 0)),
        out_shape=jax.ShapeDtypeStruct((b, t // NSA_BLOCK, 256), F32),
        compiler_params=_cparams(("parallel", "parallel")),
        name="nsa_cmp",
    )(proj, wcmp)


def _slope(h):
    return float(2.0 ** (-8.0 * (h + 1) / N_HEADS))


def _nsa_prompt_kernel(qi_ref, ki_ref, q_ref, ks_ref, vs_ref, kw_ref, vw_ref, cmp_ref, sm_ref, o_ref,
                       qs, oc, sel, m_s, l_s, acc_s, m_w, l_w, acc_w, *, nq, nk, nb):
    step_id = pl.program_id(1)
    qi = qi_ref[step_id]
    ki = ki_ref[step_id]
    q0 = qi * nq
    k0 = ki * nk
    k_last = (q0 + nq - 1) // nk
    n_sel = min(NSA_TOPN, nb)
    rows4 = NSA_GROUP * nq
    hd = HEAD_DIM
    head_in_group = _iota((rows4, 1), 0) // nq

    def slope_col(g):
        return jnp.exp2(-(NSA_GROUP * g + head_in_group + 1).astype(F32) * (8.0 / N_HEADS))

    @pl.when(ki == 0)
    def _():
        q = q_ref[...] * ATT_SCALE
        cmp = cmp_ref[...]
        tpos = q0 + (_iota((rows4, nb), 0) % nq)
        blk = _iota((rows4, nb), 1)
        dist_c = tpos - ((blk + 1) * NSA_BLOCK - 1)
        valid = dist_c >= 0
        dist_cf = dist_c.astype(F32)
        lane = _iota((rows4, hd), 1)
        scores = []
        for g in range(KVH_NSA):
            qg = jnp.concatenate([q[:, (NSA_GROUP * g + i) * hd:(NSA_GROUP * g + i + 1) * hd]
                                  for i in range(NSA_GROUP)], axis=0)
            tail = jnp.where(lane < 2, slope_col(g), 0.0)
            qs[g] = jnp.concatenate([qg, tail] if g == 0 else [tail, qg], axis=1).astype(BF16)
            kc = cmp[:, g * hd:(g + 1) * hd].astype(BF16)
            vc = cmp[:, 128 + g * hd:128 + (g + 1) * hd].astype(BF16)
            s = _dot_nt(qg.astype(BF16), kc) - slope_col(g) * dist_cf
            s = jnp.where(valid, s, NEG)
            m = jnp.max(s, axis=-1, keepdims=True)
            e = jnp.where(valid, jnp.exp(s - m), 0.0)
            l = jnp.sum(e, axis=-1, keepdims=True)
            p = e / jnp.where(l > 0.0, l, 1.0)
            oc[g] = _dot(p.astype(BF16), vc)
            imp = p[0:nq]
            for i in range(1, NSA_GROUP):
                imp = imp + p[i * nq:(i + 1) * nq]
            scores.append(imp)
        imp2 = jnp.concatenate(scores, axis=1)
        lane2 = _iota((nq, KVH_NSA * nb), 1)
        blk2 = lane2 % nb
        cur = (q0 + _iota((nq, KVH_NSA * nb), 0)) // NSA_BLOCK
        forced = (blk2 == 0) | (blk2 >= cur - 1)
        score = jnp.where(blk2 <= cur, jnp.where(forced, FORCED_SCORE, imp2), -1.0)
        score_t = score.T
        rg = 8 if nb % 8 == 0 else nb
        for g in range(KVH_NSA):
            s_g = score_t[g * nb:(g + 1) * nb, :]
            for j0 in range(0, nb, rg):
                sub = s_g[j0:j0 + rg, :]
                n_idx = j0 + _iota((rg, nq), 0)
                cnt = jnp.zeros((rg, nq), F32)
                for mm in range(nb):
                    r = s_g[mm:mm + 1, :]
                    ge = jnp.where(r >= sub, 1.0, 0.0)
                    gt = jnp.where(r > sub, 1.0, 0.0)
                    if j0 > mm:
                        cnt = cnt + ge
                    elif j0 + rg - 1 <= mm:
                        cnt = cnt + gt
                    else:
                        cnt = cnt + jnp.where(n_idx > mm, ge, gt)
                sel[g * nb + j0:g * nb + j0 + rg, :] = jnp.where(
                    cnt < n_sel, jnp.where(sub >= 0.0, 1.0, 0.0), 0.0)
        for ref in (m_s, m_w):
            ref[...] = jnp.full_like(ref, NEG)
        for ref in (l_s, acc_s, l_w, acc_w):
            ref[...] = jnp.zeros_like(ref)

    def aug_keys(k_ref):
        k = k_ref[...]
        c = _iota((nk, 2 * hd), 0)
        lane = _iota((nk, 2 * hd), 1)
        c_hi = ((c // 256) * 256).astype(F32)
        c_lo = (c % 256).astype(F32)
        k_g0 = jnp.where(lane < hd, k, jnp.where(lane == hd, c_hi, jnp.where(lane == hd + 1, c_lo, 0.0)))
        k_g1 = jnp.where(lane >= hd, k, jnp.where(lane == 0, c_hi, jnp.where(lane == 1, c_lo, 0.0)))
        return k_g0.astype(BF16), k_g1.astype(BF16)

    def attend(g, k_aug, v_bf, bias, m_ref, l_ref, acc_ref):
        head = NSA_GROUP * g + _iota((1, rows4), 1) // nq
        shift = jnp.exp2(-(head + 1).astype(F32) * (8.0 / N_HEADS)) * (k0 - q0).astype(F32)
        s = _dot_nt(k_aug, qs[g])
        s = jnp.concatenate([s[:, i * nq:(i + 1) * nq] + bias for i in range(NSA_GROUP)], axis=1)
        m_prev = m_ref[g]
        m_new = jnp.maximum(m_prev, jnp.max(s, axis=0, keepdims=True) + shift)
        alpha = jnp.exp(m_prev - m_new)
        p = jnp.exp(s - (m_new - shift))
        l_ref[g] = alpha * l_ref[g] + jnp.sum(p, axis=0, keepdims=True)
        acc_ref[g] = alpha * acc_ref[g] + _dot_tn(v_bf, p.astype(BF16))
        m_ref[g] = m_new

    def dist_tile():
        return (q0 + _iota((nk, nq), 1)) - (k0 + _iota((nk, nq), 0))

    @pl.when(ki <= k_last)
    def _():
        causal = dist_tile() >= 0
        k_aug = aug_keys(ks_ref)
        v_bf = vs_ref[...].astype(BF16)
        sel_bf = sel[...].astype(BF16)
        key_blk = (k0 + _iota((nk, KVH_NSA * nb), 0)) // NSA_BLOCK
        col = _iota((nk, KVH_NSA * nb), 1)
        for g in range(KVH_NSA):
            onehot = (col == g * nb + key_blk).astype(BF16)
            picked = _dot(onehot, sel_bf) > 0.5
            bias = jnp.where(causal, jnp.where(picked, 0.0, NEG), NEG)
            attend(g, k_aug[g], v_bf, bias, m_s, l_s, acc_s)

    @pl.when((ki <= k_last) & (ki >= k_last - (NSA_WINDOW + nk - 1) // nk))
    def _():
        dist = dist_tile()
        bias = jnp.where(dist >= 0, jnp.where(dist < NSA_WINDOW, 0.0, NEG), NEG)
        k_aug = aug_keys(kw_ref)
        v_bf = vw_ref[...].astype(BF16)
        for g in range(KVH_NSA):
            attend(g, k_aug[g], v_bf, bias, m_w, l_w, acc_w)

    @pl.when(ki == k_last)
    def _():
        gates = jax.nn.sigmoid(sm_ref[...])
        outs = []
        for g in range(KVH_NSA):
            o_s = (acc_s[g] / l_s[g]).T[:, g * hd:(g + 1) * hd]
            o_w = (acc_w[g] / l_w[g]).T[:, g * hd:(g + 1) * hd]
            o_c = oc[g]
            for i in range(NSA_GROUP):
                rows = slice(i * nq, (i + 1) * nq)
                c = L_NG + 3 * (NSA_GROUP * g + i)
                o = gates[:, c:c + 1] * o_c[rows]
                outs.append(o + (gates[:, c + 1:c + 2] * o_s[rows] + gates[:, c + 2:c + 3] * o_w[rows]))
        o_ref[...] = jnp.concatenate(outs, axis=1)


def _nsa_prompt(proj, cmp, b, t):
    nq = _pick(t, (256, 128))
    nk = _pick(t, (512, 256, 128))
    n_q = t // nq
    n_k = t // nk
    nb = t // NSA_BLOCK
    last = lambda q: (q * nq + nq - 1) // nk
    win_tiles = (NSA_WINDOW + nk - 1) // nk
    pairs = [(q, k) for q in range(n_q) for k in range(last(q) + 1)]
    qi = jnp.asarray([p[0] for p in pairs], jnp.int32)
    ki = jnp.asarray([p[1] for p in pairs], jnp.int32)
    rows4 = NSA_GROUP * nq

    def widx(q, k):
        kl = (q * nq + nq - 1) // nk
        return jnp.clip(k, jnp.maximum(kl - win_tiles, 0), kl)

    return pl.pallas_call(
        functools.partial(_nsa_prompt_kernel, nq=nq, nk=nk, nb=nb),
        grid_spec=pltpu.PrefetchScalarGridSpec(
            num_scalar_prefetch=2, grid=(b, len(pairs)),
            in_specs=[pl.BlockSpec((nq, 512), lambda i, s, qr, kr: (i * n_q + qr[s], C_NQ // 512)),
                      pl.BlockSpec((nk, 128), lambda i, s, qr, kr: (i * n_k + kr[s], C_NKV // 128 + 2)),
                      pl.BlockSpec((nk, 128), lambda i, s, qr, kr: (i * n_k + kr[s], C_NKV // 128 + 3)),
                      pl.BlockSpec((nk, 128), lambda i, s, qr, kr: (i * n_k + widx(qr[s], kr[s]), C_NWIN // 128)),
                      pl.BlockSpec((nk, 128), lambda i, s, qr, kr: (i * n_k + widx(qr[s], kr[s]), C_NWIN // 128 + 1)),
                      pl.BlockSpec((None, nb, 256), lambda i, s, qr, kr: (i, 0, 0)),
                      pl.BlockSpec((nq, 128), lambda i, s, qr, kr: (i * n_q + qr[s], C_SMALL // 128))],
            out_specs=pl.BlockSpec((nq, 512), lambda i, s, qr, kr: (i * n_q + qr[s], 0)),
            scratch_shapes=[pltpu.VMEM((KVH_NSA, rows4, 2 * HEAD_DIM), BF16),
                            pltpu.VMEM((KVH_NSA, rows4, HEAD_DIM), F32),
                            pltpu.VMEM((KVH_NSA * nb, nq), F32),
                            pltpu.VMEM((KVH_NSA, 1, rows4), F32), pltpu.VMEM((KVH_NSA, 1, rows4), F32),
                            pltpu.VMEM((KVH_NSA, 2 * HEAD_DIM, rows4), F32),
                            pltpu.VMEM((KVH_NSA, 1, rows4), F32), pltpu.VMEM((KVH_NSA, 1, rows4), F32),
                            pltpu.VMEM((KVH_NSA, 2 * HEAD_DIM, rows4), F32)]),
        out_shape=jax.ShapeDtypeStruct((b * t, GROUP_WIDTH), F32),
        compiler_params=_cparams(("parallel", "arbitrary")),
        name="nsa_prompt",
    )(qi, ki, proj, proj, proj, proj, proj, cmp, proj)


def _col_to_row(col, eye):
    return jnp.sum(jnp.where(eye, col, 0.0), axis=0, keepdims=True)


def _row_to_col(row, eye):
    return jnp.sum(jnp.where(eye, row, 0.0), axis=1, keepdims=True)


def _gdn_qkv(act_q, act_k, h):
    sl = slice(h * GDN_DK, (h + 1) * GDN_DK)
    q = act_q[:, sl]
    k = act_k[:, sl]
    q = q * lax.rsqrt(jnp.sum(q * q, axis=-1, keepdims=True) + 1e-6) * (GDN_DK ** -0.5)
    k = k * lax.rsqrt(jnp.sum(k * k, axis=-1, keepdims=True) + 1e-6)
    return q, k


def _gdn_out(o, nw, gate):
    return _rms(o, nw) * (gate * jax.nn.sigmoid(gate))


def _silu(x):
    return x * jax.nn.sigmoid(x)


def _gdn_pre_kernel(q_ref, k_ref, v_ref, hq_ref, hk_ref, hv_ref, sm_ref, cw_ref, al_ref, dt_ref,
                    u_ref, w_ref, qe_ref, kd_ref, qk_ref, egl_ref, xbuf, act, *, rows):
    c = GDN_CHUNK
    w5 = GROUP_WIDTH
    first = pl.program_id(1) == 0
    for seg, (ref, halo) in enumerate(((q_ref, hq_ref), (k_ref, hk_ref), (v_ref, hv_ref))):
        xbuf[seg, 0:8, :] = jnp.where(first, 0.0, halo[...])
        xbuf[seg, 8:8 + rows, :] = ref[...]
        cols = slice(seg * w5, (seg + 1) * w5)
        act[seg] = _silu(_conv4(cw_ref, xbuf[seg, 5:5 + rows, :], xbuf[seg, 6:6 + rows, :],
                                xbuf[seg, 7:7 + rows, :], xbuf[seg, 8:8 + rows, :], cols))
    ri = _iota((c, c), 0)
    ci = _iota((c, c), 1)
    incl = ri >= ci
    strict = ri > ci
    eye = ri == ci
    eye_f = eye.astype(F32)
    incl_f = incl.astype(F32)

    def chunk(idx, carry):
        r0 = pl.multiple_of(idx * c, c)
        rs = pl.ds(r0, c)
        sm = sm_ref[rs, :]
        beta_all = jax.nn.sigmoid(sm)
        la_all = -jnp.exp(al_ref[...]) * _softplus(sm + dt_ref[...])
        g_all = _dot(incl_f, la_all, precision=HI)
        egl_ref[rs, :] = jnp.broadcast_to(jnp.exp(g_all[c - 1:c, :]), (c, 128))
        act_q = act[0, rs, :]
        act_k = act[1, rs, :]
        act_v = act[2, rs, :]
        qks = []
        for h in range(H_GDN):
            sl = slice(h * GDN_DK, (h + 1) * GDN_DK)
            q, k = _gdn_qkv(act_q, act_k, h)
            v = act_v[:, sl]
            beta = beta_all[:, L_DB + h:L_DB + h + 1]
            g = g_all[:, L_DA + h:L_DA + h + 1]
            g_row = _col_to_row(g, eye)
            decay = jnp.exp(jnp.where(incl, g - g_row, -jnp.inf))
            kb = k * beta
            a = jnp.where(strict, _dot_nt(kb, k) * decay, 0.0)
            tm = eye_f - a
            pw = _dot3(a, a)
            for it in range(5):
                tm = tm + _dot3(tm, pw)
                if it < 4:
                    pw = _dot3(pw, pw)
            eg = jnp.exp(g)
            u_ref[rs, sl] = _dot(tm, v * beta)
            w_ref[rs, sl] = _dot(tm, kb * eg)
            qe_ref[rs, sl] = q * eg
            kd_ref[rs, sl] = k * jnp.exp(g[c - 1:c, :] - g)
            qks.append(_dot_nt(q, k) * decay)
        qk_ref[rs, :] = jnp.concatenate(qks, axis=1)
        return carry

    lax.fori_loop(0, rows // c, chunk, 0, unroll=2)


def _gdn_seq_kernel(u_ref, w_ref, qe_ref, kd_ref, qk_ref, egl_ref, g_ref, nw_ref, o_ref, s_ref, *, rows):
    c = GDN_CHUNK

    @pl.when(pl.program_id(1) == 0)
    def _():
        s_ref[...] = jnp.zeros_like(s_ref)

    nw = nw_ref[...]

    def chunk(idx, carry):
        r0 = pl.multiple_of(idx * c, c)
        rs = pl.ds(r0, c)
        egl = egl_ref[pl.ds(r0, 1), :]
        for h in range(H_GDN):
            sl = slice(h * GDN_DK, (h + 1) * GDN_DK)
            s0 = s_ref[h]
            u_new = u_ref[rs, sl] - _dot(w_ref[rs, sl], s0)
            o = _dot(qe_ref[rs, sl], s0) + _dot(qk_ref[rs, h * c:(h + 1) * c], u_new)
            s_ref[h] = s0 * egl[:, L_DA + h:L_DA + h + 1] + _dot_tn(kd_ref[rs, sl], u_new)
            o_ref[rs, sl] = _gdn_out(o, nw, g_ref[rs, sl])
        return carry

    lax.fori_loop(0, rows // c, chunk, 0)


def _gdn_prompt(proj, prm, b, t):
    c = GDN_CHUNK
    w = GROUP_WIDTH
    rows = _pick(t, (512, 256, 128, 64))
    nr = t // rows
    full = lambda shape: pl.BlockSpec(shape, lambda i, j: (0,) * len(shape))
    seg = lambda col: pl.BlockSpec((rows, w), lambda i, j: (i * nr + j, col // w))
    halo = lambda col: pl.BlockSpec(
        (8, w), lambda i, j: (jnp.maximum((i * nr + j) * (rows // 8) - 1, 0), col // w))
    blk = lambda width: pl.BlockSpec((rows, width), lambda i, j: (i * nr + j, 0))
    f32 = lambda width: jax.ShapeDtypeStruct((b * t, width), F32)
    u, wk, qe, kd, qk, egl = pl.pallas_call(
        functools.partial(_gdn_pre_kernel, rows=rows),
        grid=(b, nr),
        in_specs=[seg(C_DQKV), seg(C_DQKV + w), seg(C_DQKV + 2 * w),
                  halo(C_DQKV), halo(C_DQKV + w), halo(C_DQKV + 2 * w),
                  pl.BlockSpec((rows, 128), lambda i, j: (i * nr + j, C_SMALL // 128)),
                  full((CONV_W, 3 * w)), full((1, 128)), full((1, 128))],
        out_specs=[blk(w), blk(w), blk(w), blk(w), blk(H_GDN * c), blk(128)],
        out_shape=[f32(w), f32(w), f32(w), f32(w), f32(H_GDN * c), f32(128)],
        scratch_shapes=[pltpu.VMEM((3, rows + 8, w), F32), pltpu.VMEM((3, rows, w), F32)],
        compiler_params=_cparams(("parallel", "parallel")),
        name="gdn_pre",
    )(proj, proj, proj, proj, proj, proj, proj, prm['gdn_conv_w'], prm['gdn_alog_row'], prm['gdn_dt_row'])
    return pl.pallas_call(
        functools.partial(_gdn_seq_kernel, rows=rows),
        grid=(b, nr),
        in_specs=[blk(w), blk(w), blk(w), blk(w), blk(H_GDN * c), blk(128), seg(C_DG), full((1, GDN_DK))],
        out_specs=[blk(w), pl.BlockSpec((None, H_GDN, GDN_DK, GDN_DK), lambda i, j: (i, 0, 0, 0))],
        out_shape=[f32(w), jax.ShapeDtypeStruct((b, H_GDN, GDN_DK, GDN_DK), F32)],
        compiler_params=_cparams(("parallel", "arbitrary")),
        name="gdn_seq",
    )(u, wk, qe, kd, qk, egl, proj, prm['gdn_norm_w'])


def _gdn_step_kernel(x_ref, g_ref, sm_ref, buf_ref, s0_ref, cw_ref, al_ref, dt_ref, nw_ref,
                     o_ref, s_ref):
    buf = buf_ref[...]
    act = _silu(_conv4(cw_ref, buf[0:1], buf[1:2], buf[2:3], x_ref[...]))
    sm = sm_ref[...]
    beta_all = jax.nn.sigmoid(sm)
    alpha_all = jnp.exp(-jnp.exp(al_ref[...]) * _softplus(sm + dt_ref[...]))
    eye = _iota((GDN_DK, GDN_DK), 0) == _iota((GDN_DK, GDN_DK), 1)
    w = GROUP_WIDTH
    gate = g_ref[...]
    outs = []
    for h in range(H_GDN):
        sl = slice(h * GDN_DK, (h + 1) * GDN_DK)
        q, k = _gdn_qkv(act[:, 0:w], act[:, w:2 * w], h)
        v = act[:, 2 * w + h * GDN_DK:2 * w + (h + 1) * GDN_DK]
        beta = beta_all[:, L_DB + h:L_DB + h + 1]
        alpha = alpha_all[:, L_DA + h:L_DA + h + 1]
        k_col = _row_to_col(k, eye)
        q_col = _row_to_col(q, eye)
        s0 = s0_ref[h]
        ks = jnp.sum(s0 * k_col, axis=0, keepdims=True)
        u = beta * v - (beta * alpha) * ks
        s1 = alpha * s0 + k_col * u
        s_ref[h] = s1
        o = jnp.sum(s1 * q_col, axis=0, keepdims=True)
        outs.append(_gdn_out(o, nw_ref[...], gate[:, sl]))
    o_ref[...] = jnp.concatenate(outs, axis=1)


def _gdn_step(x, gate, small, conv_buf, s0, prm):
    n = x.shape[0]
    w = GROUP_WIDTH
    full = lambda shape: pl.BlockSpec(shape, lambda i: (0,) * len(shape))
    return pl.pallas_call(
        _gdn_step_kernel,
        grid=(n,),
        in_specs=[pl.BlockSpec((None, 1, 3 * w), lambda i: (i, 0, 0)),
                  pl.BlockSpec((None, 1, w), lambda i: (i, 0, 0)),
                  pl.BlockSpec((None, 1, 128), lambda i: (i, 0, 0)),
                  pl.BlockSpec((None, CONV_W - 1, 3 * w), lambda i: (i, 0, 0)),
                  pl.BlockSpec((None, H_GDN, GDN_DK, GDN_DK), lambda i: (i, 0, 0, 0)),
                  full((CONV_W, 3 * w)), full((1, 128)), full((1, 128)), full((1, GDN_DK))],
        out_specs=[pl.BlockSpec((None, 1, w), lambda i: (i, 0, 0)),
                   pl.BlockSpec((None, H_GDN, GDN_DK, GDN_DK), lambda i: (i, 0, 0, 0))],
        out_shape=[jax.ShapeDtypeStruct((n, 1, w), F32),
                   jax.ShapeDtypeStruct((n, H_GDN, GDN_DK, GDN_DK), F32)],
        compiler_params=_cparams(("parallel",)),
        name="gdn_step",
    )(x, gate, small, conv_buf, s0, prm['gdn_conv_w'], prm['gdn_alog_row'], prm['gdn_dt_row'],
      prm['gdn_norm_w'])


def _dot_row(row, mat, **kw):
    return _dot(jnp.broadcast_to(row, (8, row.shape[1])), mat, **kw)[0:1]


def _fox_step_kernel(pt_ref, *refs, n_pages):
    kv = refs[:n_pages]
    lf = refs[n_pages:2 * n_pages]
    x_ref, sm_ref, bf_ref, o_ref, lf_out_ref = refs[2 * n_pages:]
    w = GROUP_WIDTH
    pg = PAGE_SIZE
    x = x_ref[...]
    q, kn, vn = x[:, 0:w], x[:, w:2 * w], x[:, 2 * w:3 * w]
    logf_new = _log_sigmoid(sm_ref[...] + bf_ref[...])
    lf_out_ref[...] = logf_new
    head_cols = (_iota((w, N_HEADS), 0) // HEAD_DIM == _iota((w, N_HEADS), 1)).astype(BF16)
    head_rows = _iota((N_HEADS, w), 1) // HEAD_DIM == _iota((N_HEADS, w), 0)
    eye = _iota((w, w), 0) == _iota((w, w), 1)
    qblk = _dot(jnp.where(eye, q, 0.0).astype(BF16), head_cols).astype(BF16)
    tri = (_iota((pg, pg), 1) <= _iota((pg, pg), 0)).astype(F32)
    carry = jnp.zeros((1, N_HEADS), F32)
    cs = []
    for p in range(n_pages):
        c = _dot(tri, lf[p][...], precision=HI) + carry
        carry = c[pg - 1:pg, :]
        cs.append(c)
    ctot = carry + logf_new[:, 0:N_HEADS]
    s_n = _dot_row(kn.astype(BF16), qblk) * ATT_SCALE
    m = s_n
    ss = []
    stride = 2 * N_HEADS
    for p in range(n_pages):
        s = None
        for h in range(N_HEADS):
            k_h = kv[p][pl.ds(h, pg, stride=stride), :].astype(BF16)
            part = _dot(k_h, qblk[h * HEAD_DIM:(h + 1) * HEAD_DIM, :])
            s = part if s is None else s + part
        s = s * ATT_SCALE + (ctot - cs[p])
        ss.append(s)
        m = jnp.maximum(m, jnp.max(s, axis=0, keepdims=True))
    e_n = jnp.exp(s_n - m)
    l = e_n
    acc = [jnp.zeros((8, HEAD_DIM), F32) for _ in range(N_HEADS)]
    for p in range(n_pages):
        e = jnp.exp(ss[p] - m)
        l = l + jnp.sum(e, axis=0, keepdims=True)
        for h in range(N_HEADS):
            v_h = kv[p][pl.ds(N_HEADS + h, pg, stride=stride), :]
            acc[h] = acc[h] + jnp.sum((e[:, h:h + 1] * v_h).reshape(pg // 8, 8, HEAD_DIM), axis=0)
    tot = jnp.concatenate([jnp.sum(a, axis=0, keepdims=True) for a in acc], axis=1)
    tot = tot + _dot_row(e_n.astype(BF16), head_rows.astype(BF16)) * vn
    o_ref[...] = tot * _dot_row(1.0 / l, head_rows.astype(F32), precision=HI)


def _fox_step(pt, kv_pool, lf_pool, x, small, bf_row):
    n, n_pages = pt.shape
    w = GROUP_WIDTH
    page = lambda i, rows, width: pl.BlockSpec((None, rows, width), lambda b, t: (t[b, i], 0, 0))
    in_specs = ([page(i, PAGE_SIZE * 2 * N_HEADS, HEAD_DIM) for i in range(n_pages)] +
                [page(i, PAGE_SIZE, N_HEADS) for i in range(n_pages)] +
                [pl.BlockSpec((None, 1, 3 * w), lambda b, t: (b, 0, 0)),
                 pl.BlockSpec((None, 1, 128), lambda b, t: (b, 0, 0)),
                 pl.BlockSpec((1, 128), lambda b, t: (0, 0))])
    return pl.pallas_call(
        functools.partial(_fox_step_kernel, n_pages=n_pages),
        grid_spec=pltpu.PrefetchScalarGridSpec(
            num_scalar_prefetch=1, grid=(n,), in_specs=in_specs,
            out_specs=[pl.BlockSpec((None, 1, w), lambda b, t: (b, 0, 0)),
                       pl.BlockSpec((None, 1, 128), lambda b, t: (b, 0, 0))]),
        out_shape=[jax.ShapeDtypeStruct((n, 1, w), F32), jax.ShapeDtypeStruct((n, 1, 128), F32)],
        compiler_params=_cparams(("parallel",)),
        name="fox_step",
    )(pt, *([kv_pool] * n_pages), *([lf_pool] * n_pages), x, small, bf_row)


def _softmax_cols(scores):
    m = None
    for s in scores:
        mx = jnp.max(s, axis=0, keepdims=True)
        m = mx if m is None else jnp.maximum(m, mx)
    es = [jnp.exp(s - m) for s in scores]
    l = None
    for e in es:
        sm = jnp.sum(e, axis=0, keepdims=True)
        l = sm if l is None else l + sm
    return es, l


def _nsa_step_kernel(pt_ref, *refs, n_pages):
    pgs = refs[:n_pages]
    win_ref, q_ref, kv_ref, wn_ref, sm_ref, wc_ref, o_ref = refs[n_pages:]
    pg = PAGE_SIZE
    w = GROUP_WIDTH
    kvw = KVH_NSA * HEAD_DIM
    past = n_pages * pg
    nbp = past // NSA_BLOCK
    q = q_ref[...]
    r = _iota((kvw, w), 0)
    c = _iota((kvw, w), 1)
    fold = (c % HEAD_DIM == r % HEAD_DIM) & (c // (NSA_GROUP * HEAD_DIM) == r // HEAD_DIM)
    head_cols = (_iota((w, N_HEADS), 0) // HEAD_DIM == _iota((w, N_HEADS), 1)).astype(BF16)
    qn = _dot(jnp.where(fold, q, 0.0).astype(BF16), head_cols).astype(BF16)
    hl = _iota((1, N_HEADS), 1)
    slopes = jnp.exp2(-(hl + 1).astype(F32) * (8.0 / N_HEADS))
    spread = [_iota((N_HEADS, kvw), 0) == NSA_GROUP * (_iota((N_HEADS, kvw), 1) // HEAD_DIM) + i
              for i in range(NSA_GROUP)]
    spread_bf = [s.astype(BF16) for s in spread]
    spread_f = [s.astype(F32) for s in spread]

    def weighted_rows(es, vals):
        out = []
        for i in range(NSA_GROUP):
            tot = None
            for e, v in zip(es, vals):
                if e.shape[0] == 1:
                    part = _dot_row(e.astype(BF16), spread_bf[i]) * v
                else:
                    part = jnp.sum(_dot(e.astype(BF16), spread_bf[i]) * v, axis=0, keepdims=True)
                tot = part if tot is None else tot + part
            out.append(tot)
        return out

    wc = wc_ref[...]
    kc_rows, vc_rows = [], []
    for p in range(n_pages):
        rows = pgs[p][:, 0:2 * kvw]
        for half in range(pg // NSA_BLOCK):
            cm = jnp.sum(rows[half * NSA_BLOCK:(half + 1) * NSA_BLOCK] * wc, axis=0, keepdims=True)
            kc_rows.append(cm[:, 0:kvw])
            vc_rows.append(cm[:, kvw:2 * kvw])
    kcmp = jnp.concatenate(kc_rows, axis=0)
    vcmp = jnp.concatenate(vc_rows, axis=0)
    blk = _iota((nbp, 1), 0)
    dist_c = (past - ((blk + 1) * NSA_BLOCK - 1)).astype(F32)
    s_c = _dot(kcmp.astype(BF16), qn) * ATT_SCALE - slopes * dist_c
    (e_c,), l_c = _softmax_cols([s_c])
    p_c = e_c / l_c
    acc_c = weighted_rows([p_c], [vcmp])

    cur = past // NSA_BLOCK
    forced = (blk == 0) | (blk >= cur - 1)
    eye_b = _iota((nbp, nbp), 0) == _iota((nbp, nbp), 1)
    lower = _iota((nbp, nbp), 1) < _iota((nbp, nbp), 0)
    n_sel = min(NSA_TOPN, nbp + 1)
    sels = []
    for g in range(KVH_NSA):
        imp = p_c[:, NSA_GROUP * g:NSA_GROUP * g + 1]
        for i in range(1, NSA_GROUP):
            imp = imp + p_c[:, NSA_GROUP * g + i:NSA_GROUP * g + i + 1]
        score = jnp.where(forced, FORCED_SCORE, imp)
        score_row = _col_to_row(score, eye_b)
        beats = (score_row > score) | ((score_row == score) & lower)
        cnt = jnp.sum(jnp.where(beats, 1.0, 0.0), axis=1, keepdims=True)
        cnt = cnt + jnp.where(score < FORCED_SCORE, 1.0, 0.0)
        sels.append(jnp.where(cnt < n_sel, 1.0, 0.0))
    sel8 = jnp.where(hl < NSA_GROUP, sels[0], sels[1])

    kv_new = kv_ref[...]
    first_half = _iota((pg, 1), 0) < NSA_BLOCK
    scores, vals = [], []
    for p in range(n_pages):
        dist = (past - (p * pg + _iota((pg, 1), 0))).astype(F32)
        s = _dot(pgs[p][:, 2 * kvw:3 * kvw].astype(BF16), qn) * ATT_SCALE - slopes * dist
        chosen = jnp.where(first_half, sel8[2 * p:2 * p + 1, :], sel8[2 * p + 1:2 * p + 2, :])
        scores.append(jnp.where(chosen > 0.5, s, NEG))
        vals.append(pgs[p][:, 3 * kvw:4 * kvw])
    scores.append(_dot_row(kv_new[:, 2 * kvw:3 * kvw].astype(BF16), qn) * ATT_SCALE)
    vals.append(kv_new[:, 3 * kvw:4 * kvw])
    e_s, l_s = _softmax_cols(scores)
    acc_s = weighted_rows(e_s, vals)

    nw = win_ref.shape[0]
    wrow = _iota((nw, 1), 0)
    dist_w = nw - wrow
    s_w = _dot(win_ref[:, 0:kvw].astype(BF16), qn) * ATT_SCALE - slopes * dist_w.astype(F32)
    s_w = jnp.where((dist_w < NSA_WINDOW) & (past - dist_w >= 0), s_w, NEG)
    wn = wn_ref[...]
    s_wn = _dot_row(wn[:, 0:kvw].astype(BF16), qn) * ATT_SCALE
    e_w, l_w = _softmax_cols([s_w, s_wn])
    acc_w = weighted_rows(e_w, [win_ref[:, kvw:2 * kvw], wn[:, kvw:2 * kvw]])

    gates = jax.nn.sigmoid(sm_ref[...])
    lane = _iota((128, N_HEADS), 0)
    head = _iota((128, N_HEADS), 1)
    coef = []
    for k in range(3):
        pick = (lane == L_NG + 3 * head + k).astype(F32)
        coef.append(_dot_row(gates, pick, precision=HI))
    coef[1] = coef[1] / l_s
    coef[2] = coef[2] / l_w
    tots = []
    for i in range(NSA_GROUP):
        t = _dot_row(coef[0], spread_f[i], precision=HI) * acc_c[i]
        t = t + (_dot_row(coef[1], spread_f[i], precision=HI) * acc_s[i]
                 + _dot_row(coef[2], spread_f[i], precision=HI) * acc_w[i])
        tots.append(t)
    o_ref[...] = jnp.concatenate([t[:, 0:HEAD_DIM] for t in tots] + [t[:, HEAD_DIM:kvw] for t in tots], axis=1)


def _nsa_step(pt, kv_pool, win, q, kv_new, win_new, small, wcmp):
    n, n_pages = pt.shape
    w = GROUP_WIDTH
    nw = win.shape[1]
    row = lambda width: pl.BlockSpec((None, 1, width), lambda b, t: (b, 0, 0))
    in_specs = ([pl.BlockSpec((None, PAGE_SIZE, w), (lambda i: (lambda b, t: (t[b, i], 0, 0)))(i))
                 for i in range(n_pages)] +
                [pl.BlockSpec((None, nw, 256), lambda b, t: (b, 0, 0)),
                 row(w), row(w), row(256), row(128),
                 pl.BlockSpec((NSA_BLOCK, 256), lambda b, t: (0, 0))])
    return pl.pallas_call(
        functools.partial(_nsa_step_kernel, n_pages=n_pages),
        grid_spec=pltpu.PrefetchScalarGridSpec(
            num_scalar_prefetch=1, grid=(n,), in_specs=in_specs,
            out_specs=pl.BlockSpec((None, 1, w), lambda b, t: (b, 0, 0))),
        out_shape=jax.ShapeDtypeStruct((n, 1, w), F32),
        compiler_params=_cparams(("parallel",)),
        name="nsa_step",
    )(pt, *([kv_pool] * n_pages), win, q, kv_new, win_new, small, wcmp)


def _head_cols(row, eye):
    return [_row_to_col(row[:, h * HEAD_DIM:(h + 1) * HEAD_DIM], eye) for h in range(N_HEADS)]


def _rows_to_block(rows):
    sub = _iota((N_HEADS, rows[0].shape[1]), 0)
    out = jnp.broadcast_to(rows[0], sub.shape)
    for h in range(1, N_HEADS):
        out = jnp.where(sub == h, rows[h], out)
    return out


def _tile_scores(tile_of_head, q_cols):
    return _rows_to_block([jnp.sum(tile_of_head(h) * q_cols[h], axis=0, keepdims=True)
                           for h in range(N_HEADS)])


def _fox_step_kernel_t(pt_ref, *refs, n_pages):
    kv = refs[:n_pages]
    lf = refs[n_pages:2 * n_pages]
    x_ref, sm_ref, bf_ref, o_ref, lf_out_ref = refs[2 * n_pages:]
    w = GROUP_WIDTH
    pg = PAGE_SIZE
    hd = HEAD_DIM
    x = x_ref[...]
    q, kn, vn = x[:, 0:w], x[:, w:2 * w], x[:, 2 * w:3 * w]
    logf_new = _log_sigmoid(sm_ref[...] + bf_ref[...])
    lf_out_ref[...] = logf_new
    eye = _iota((hd, hd), 0) == _iota((hd, hd), 1)
    eye8 = _iota((N_HEADS, N_HEADS), 0) == _iota((N_HEADS, N_HEADS), 1)
    q_cols = _head_cols(q, eye)
    vn_cols = _head_cols(vn, eye)
    sub = _iota((N_HEADS, 1), 0)
    upper = (_iota((pg, pg), 0) <= _iota((pg, pg), 1)).astype(F32)
    carry = jnp.zeros((N_HEADS, 1), F32)
    cs = []
    for p in range(n_pages):
        c = _dot(lf[p][...], upper, precision=HI) + carry
        carry = c[:, pg - 1:pg]
        cs.append(c)
    ctot = carry + _row_to_col(logf_new[:, 0:N_HEADS], eye8)
    prod = kn * q
    s_n = jnp.zeros((N_HEADS, 1), F32)
    for h in range(N_HEADS):
        s_n = jnp.where(sub == h, jnp.sum(prod[:, h * hd:(h + 1) * hd], axis=1, keepdims=True), s_n)
    s_n = s_n * ATT_SCALE
    m = s_n
    ss = []
    for p in range(n_pages):
        s = _tile_scores(lambda h: kv[p][0, h], q_cols) * ATT_SCALE + (ctot - cs[p])
        ss.append(s)
        m = jnp.maximum(m, jnp.max(s, axis=1, keepdims=True))
    e_n = jnp.exp(s_n - m)
    l = e_n
    es = []
    for p in range(n_pages):
        e = jnp.exp(ss[p] - m)
        es.append(e)
        l = l + jnp.sum(e, axis=1, keepdims=True)
    inv_l = 1.0 / l
    outs = []
    for h in range(N_HEADS):
        acc = kv[0][1, h] * es[0][h:h + 1, :]
        for p in range(1, n_pages):
            acc = acc + kv[p][1, h] * es[p][h:h + 1, :]
        o_col = jnp.sum(acc, axis=1, keepdims=True) + vn_cols[h] * e_n[h:h + 1, :]
        outs.append(_col_to_row(o_col * inv_l[h:h + 1, :], eye))
    o_ref[...] = jnp.concatenate(outs, axis=1)


def _fox_step_t(pt, layer, kv_t, lf_t, x, small, bf_row):
    n, n_pages = pt.shape
    w = GROUP_WIDTH
    kv_spec = lambda i: pl.BlockSpec((None, None, 2, N_HEADS, HEAD_DIM, PAGE_SIZE),
                                     lambda b, t: (layer, t[b, i], 0, 0, 0, 0))
    lf_spec = lambda i: pl.BlockSpec((None, None, N_HEADS, PAGE_SIZE), lambda b, t: (layer, t[b, i], 0, 0))
    in_specs = ([kv_spec(i) for i in range(n_pages)] + [lf_spec(i) for i in range(n_pages)] +
                [pl.BlockSpec((None, 1, 3 * w), lambda b, t: (b, 0, 0)),
                 pl.BlockSpec((None, 1, 128), lambda b, t: (b, 0, 0)),
                 pl.BlockSpec((1, 128), lambda b, t: (0, 0))])
    return pl.pallas_call(
        functools.partial(_fox_step_kernel_t, n_pages=n_pages),
        grid_spec=pltpu.PrefetchScalarGridSpec(
            num_scalar_prefetch=1, grid=(n,), in_specs=in_specs,
            out_specs=[pl.BlockSpec((None, 1, w), lambda b, t: (b, 0, 0)),
                       pl.BlockSpec((None, 1, 128), lambda b, t: (b, 0, 0))]),
        out_shape=[jax.ShapeDtypeStruct((n, 1, w), F32), jax.ShapeDtypeStruct((n, 1, 128), F32)],
        compiler_params=_cparams(("parallel",)),
        name="fox_step",
    )(pt, *([kv_t] * n_pages), *([lf_t] * n_pages), x, small, bf_row)


def _nsa_step_kernel_t(pt_ref, *refs, n_pages):
    pgs = refs[:n_pages]
    win_ref, q_ref, kv_ref, wn_ref, sm_ref, wc_ref, o_ref = refs[n_pages:]
    pg = PAGE_SIZE
    hd = HEAD_DIM
    nh = N_HEADS
    past = n_pages * pg
    nbp = past // NSA_BLOCK
    bpp = pg // NSA_BLOCK
    eye = _iota((hd, hd), 0) == _iota((hd, hd), 1)
    q_cols = _head_cols(q_ref[...], eye)
    sub = _iota((nh, 1), 0)
    lane = _iota((1, pg), 1)
    slopes = jnp.exp2(-(sub + 1).astype(F32) * (8.0 / nh))
    grp = lambda h: h // NSA_GROUP
    kv_new = kv_ref[...]
    wn = wn_ref[...]
    wc = wc_ref[...]

    def new_scores(row):
        out = jnp.zeros((nh, 1), F32)
        for h in range(nh):
            k_col = _row_to_col(row[:, grp(h) * hd:(grp(h) + 1) * hd], eye)
            out = jnp.where(sub == h, jnp.sum(k_col * q_cols[h], axis=0, keepdims=True), out)
        return out * ATT_SCALE

    def softmax_lanes(scores, extra=None):
        m = extra
        for s in scores:
            mx = jnp.max(s, axis=1, keepdims=True)
            m = mx if m is None else jnp.maximum(m, mx)
        es = [jnp.exp(s - m) for s in scores]
        l = None if extra is None else jnp.exp(extra - m)
        e_extra = l
        for e in es:
            sm = jnp.sum(e, axis=1, keepdims=True)
            l = sm if l is None else l + sm
        return es, e_extra, l

    def weighted_values(tiles, es):
        cols = []
        for h in range(nh):
            acc = None
            for tile, e in zip(tiles, es):
                part = tile(grp(h)) * e[h:h + 1, :]
                acc = part if acc is None else acc + part
            cols.append(jnp.sum(acc, axis=1, keepdims=True))
        return cols

    half = [lane // NSA_BLOCK == i for i in range(bpp)]
    blk_lane = _iota((1, pg), 1)
    s_c = jnp.full((nh, pg), NEG, F32)
    for p in range(n_pages):
        raw = _tile_scores(lambda h: pgs[p][0, grp(h)], q_cols) * wc[0:1, :]
        for i in range(bpp):
            r = jnp.sum(jnp.where(half[i], raw, 0.0), axis=1, keepdims=True)
            s_c = jnp.where(blk_lane == bpp * p + i, r, s_c)
    dist_c = (past - ((blk_lane + 1) * NSA_BLOCK - 1)).astype(F32)
    s_c = jnp.where(blk_lane < nbp, s_c * ATT_SCALE - slopes * dist_c, NEG)
    (e_c,), _, l_c = softmax_lanes([s_c])
    p_c = e_c / l_c
    pes = []
    for p in range(n_pages):
        pe = jnp.zeros((nh, pg), F32)
        for i in range(bpp):
            pe = jnp.where(half[i], p_c[:, bpp * p + i:bpp * p + i + 1], pe)
        pes.append(pe * wc[1:2, :])
    o_c = weighted_values([(lambda g, p=p: pgs[p][1, g]) for p in range(n_pages)], pes)

    nbl = 32 * ((nbp + 31) // 32)
    eye_b = _iota((nbl, nbl), 0) == _iota((nbl, nbl), 1)
    m_lt_n = _iota((nbl, nbl), 0) < _iota((nbl, nbl), 1)
    blk_row = _iota((1, nbl), 1)
    n_sel = min(NSA_TOPN, nbp + 1)
    sel_rows = []
    for g in range(KVH_NSA):
        imp = jnp.sum(jnp.where(sub // NSA_GROUP == g, p_c, 0.0), axis=0, keepdims=True)[:, 0:nbl]
        forced = (blk_row == 0) | (blk_row >= nbp - 1)
        score = jnp.where(blk_row < nbp, jnp.where(forced, FORCED_SCORE, imp), -1.0)
        score_col = _row_to_col(score, eye_b)
        beats = jnp.where(score_col > score, 1.0, jnp.where((score_col == score) & m_lt_n, 1.0, 0.0))
        cnt = jnp.sum(beats, axis=0, keepdims=True)
        cnt = cnt + jnp.where(score < FORCED_SCORE, 1.0, 0.0)
        sel_rows.append(jnp.where((cnt < n_sel) & (score >= 0.0), 1.0, 0.0))

    scores = []
    for p in range(n_pages):
        dist = (past - (p * pg + lane)).astype(F32)
        s = _tile_scores(lambda h: pgs[p][2, grp(h)], q_cols) * ATT_SCALE - slopes * dist
        chosen = []
        for g in range(KVH_NSA):
            row = jnp.zeros((1, pg), F32)
            for i in range(bpp):
                row = jnp.where(half[i], sel_rows[g][:, bpp * p + i:bpp * p + i + 1], row)
            chosen.append(row)
        mask = jnp.where(sub // NSA_GROUP == 0, chosen[0], chosen[1])
        scores.append(jnp.where(mask > 0.5, s, NEG))
    s_new = new_scores(kv_new[:, 2 * 128:3 * 128])
    e_s, e_sn, l_s = softmax_lanes(scores, s_new)
    o_s = weighted_values([(lambda g, p=p: pgs[p][3, g]) for p in range(n_pages)], e_s)
    v_new = kv_new[:, 3 * 128:4 * 128]
    o_s = [o_s[h] + _row_to_col(v_new[:, grp(h) * hd:(grp(h) + 1) * hd], eye) * e_sn[h:h + 1, :]
           for h in range(nh)]

    nw = win_ref.shape[-1]
    scores = []
    for c0 in range(0, nw, pg):
        dist = nw - (c0 + lane)
        s = _tile_scores(lambda h: win_ref[0, grp(h), :, c0:c0 + pg], q_cols) * ATT_SCALE
        s = s - slopes * dist.astype(F32)
        scores.append(jnp.where((dist < NSA_WINDOW) & (past - dist >= 0), s, NEG))
    s_new = new_scores(wn[:, 0:128])
    e_w, e_wn, l_w = softmax_lanes(scores, s_new)
    o_w = weighted_values([(lambda g, c0=c0: win_ref[1, g, :, c0:c0 + pg]) for c0 in range(0, nw, pg)], e_w)
    vw_new = wn[:, 128:256]
    o_w = [o_w[h] + _row_to_col(vw_new[:, grp(h) * hd:(grp(h) + 1) * hd], eye) * e_wn[h:h + 1, :]
           for h in range(nh)]

    gates = jax.nn.sigmoid(sm_ref[...])
    outs = []
    for h in range(nh):
        c = L_NG + 3 * h
        o = gates[:, c:c + 1] * o_c[h]
        o = o + ((gates[:, c + 1:c + 2] / l_s[h:h + 1, :]) * o_s[h]
                 + (gates[:, c + 2:c + 3] / l_w[h:h + 1, :]) * o_w[h])
        outs.append(_col_to_row(o, eye))
    o_ref[...] = jnp.concatenate(outs, axis=1)


def _nsa_step_t(pt, layer, kv_t, win_t, q, kv_new, win_new, small, wc_rows):
    n, n_pages = pt.shape
    w = GROUP_WIDTH
    nw = win_t.shape[-1]
    row = lambda width: pl.BlockSpec((None, 1, width), lambda b, t: (b, 0, 0))
    page = lambda i: pl.BlockSpec((None, None, 4, KVH_NSA, HEAD_DIM, PAGE_SIZE),
                                  lambda b, t: (layer, t[b, i], 0, 0, 0, 0))
    in_specs = ([page(i) for i in range(n_pages)] +
                [pl.BlockSpec((None, None, 2, KVH_NSA, HEAD_DIM, nw), lambda b, t: (layer, b, 0, 0, 0, 0)),
                 row(w), row(w), row(256), row(128),
                 pl.BlockSpec((2, PAGE_SIZE), lambda b, t: (0, 0))])
    return pl.pallas_call(
        functools.partial(_nsa_step_kernel_t, n_pages=n_pages),
        grid_spec=pltpu.PrefetchScalarGridSpec(
            num_scalar_prefetch=1, grid=(n,), in_specs=in_specs,
            out_specs=pl.BlockSpec((None, 1, w), lambda b, t: (b, 0, 0))),
        out_shape=jax.ShapeDtypeStruct((n, 1, w), F32),
        compiler_params=_cparams(("parallel",)),
        name="nsa_step",
    )(pt, *([kv_t] * n_pages), win_t, q, kv_new, win_new, small, wc_rows)


_O_FF, _O_LX, _O_NQ, _O_NKC, _O_NG, _O_DQKV, _O_DG, _O_DA, _O_END = (
    1536, 1544, 2568, 3080, 3848, 3872, 5408, 5920, 5928)


def _pack_w_in(w):
    d = w.shape[0]
    zeros = lambda n: jnp.zeros((d, n), w.dtype)
    parts = [w[:, 0:_O_FF], w[:, _O_LX:_O_NQ], w[:, _O_NQ:_O_NKC], w[:, _O_NKC:_O_NG],
             w[:, _O_FF:_O_LX], w[:, _O_NG:_O_DQKV], w[:, _O_DA:_O_END], zeros(128 - 40), zeros(128),
             w[:, _O_DQKV:_O_DG], w[:, _O_DG:_O_DA]]
    out = jnp.concatenate(parts, axis=1)
    assert out.shape[1] == N_PROJ
    return out.astype(BF16)


def _block_diag(w):
    n, d, e = w.shape
    eye = jnp.eye(n, dtype=w.dtype)
    return (eye[:, None, :, None] * w[:, :, None, :]).reshape(n * d, n * e)


def _lane_row(vals, start):
    return jnp.zeros((1, 128), F32).at[0, start:start + vals.shape[0]].set(vals)


def _layer_params(l, p):
    row = lambda a: a[l][None, :]
    return dict(
        norm_pre_mix=row(p['norm_pre_mix']), norm_post_mix=row(p['norm_post_mix']),
        norm_pre_mlp=row(p['norm_pre_mlp']), norm_post_mlp=row(p['norm_post_mlp']),
        w_in=_pack_w_in(p['w_in'][l]), w_out=p['w_out'][l].astype(BF16),
        w_up=p['w_up'][l].astype(BF16), w_down=p['w_down'][l].astype(BF16),
        fox_bf_row=_lane_row(p['fox_b_f'][l], L_FF),
        lru_conv_w=p['lru_conv_w'][l], lru_conv_b=row(p['lru_conv_b']),
        lru_wr=_block_diag(p['lru_w_r'][l]).astype(BF16), lru_b_r=row(p['lru_b_r']),
        lru_wi=_block_diag(p['lru_w_i'][l]).astype(BF16), lru_b_i=row(p['lru_b_i']),
        lru_lambda=row(p['lru_lambda']),
        nsa_wcmp=jnp.repeat(p['nsa_w_cmp'][l].T, 128, axis=1),
        gdn_conv_w=p['gdn_conv_w'][l],
        gdn_alog_row=_lane_row(p['gdn_A_log'][l], L_DA), gdn_dt_row=_lane_row(p['gdn_dt_bias'][l], L_DA),
        gdn_norm_w=row(p['gdn_norm_w']))


def kernel(x_prompt, x_sample, cache_fox_kv, cache_fox_logf, cache_nsa_kv, cache_nsa_win,
           state_rglru_conv, state_rglru_h, state_gdn_conv, state_gdn_S, page_table,
           norm_pre_mix, norm_post_mix, norm_pre_mlp, norm_post_mlp, w_in, w_out, w_up, w_down,
           fox_b_f, lru_conv_w, lru_conv_b, lru_w_r, lru_b_r, lru_w_i, lru_b_i, lru_lambda,
           nsa_w_cmp, gdn_conv_w, gdn_A_log, gdn_dt_bias, gdn_norm_w):
    params = dict(norm_pre_mix=norm_pre_mix, norm_post_mix=norm_post_mix, norm_pre_mlp=norm_pre_mlp,
                  norm_post_mlp=norm_post_mlp, w_in=w_in, w_out=w_out, w_up=w_up, w_down=w_down,
                  fox_b_f=fox_b_f, lru_conv_w=lru_conv_w, lru_conv_b=lru_conv_b, lru_w_r=lru_w_r,
                  lru_b_r=lru_b_r, lru_w_i=lru_w_i, lru_b_i=lru_b_i, lru_lambda=lru_lambda,
                  nsa_w_cmp=nsa_w_cmp, gdn_conv_w=gdn_conv_w, gdn_A_log=gdn_A_log,
                  gdn_dt_bias=gdn_dt_bias, gdn_norm_w=gdn_norm_w)
    b, t, d = x_prompt.shape
    ns = x_sample.shape[0]
    depth, n_pool = cache_fox_kv.shape[:2]
    bt = b * t
    w = GROUP_WIDTH
    xp = x_prompt.reshape(bt, d)
    xs = x_sample.reshape(ns, d)
    fox_t = jnp.transpose(cache_fox_kv, (0, 1, 3, 4, 5, 2))
    logf_t = jnp.transpose(cache_fox_logf, (0, 1, 3, 2))
    nsa_t = jnp.transpose(cache_nsa_kv, (0, 1, 3, 4, 5, 2))
    win_t = jnp.transpose(cache_nsa_win, (0, 1, 3, 4, 5, 2))
    outs = [[] for _ in range(16)]
    for l in range(depth):
        prm = _layer_params(l, params)
        proj = _in_proj(xp, prm['norm_pre_mix'], prm['w_in'])
        ps = _in_proj(xs, prm['norm_pre_mix'], prm['w_in'])
        pp = proj.reshape(b, t, N_PROJ)
        row3 = lambda c0, c1: ps[:, None, c0:c1]

        qa, ka, va, logf_p = _fox_prep(proj, prm['fox_bf_row'], b, t)
        oa_p = _fox_flash(qa, ka, va, b, t)
        oa_s, logf_s = _fox_step_t(page_table, l, fox_t, logf_t, row3(C_FQ, C_FQ + 3 * w),
                                   row3(C_SMALL, C_SMALL + 128), prm['fox_bf_row'])

        ob_p, h_p = _lru_prompt(proj, prm, b, t)
        ob_s, h_s = _lru_step(ps[:, C_LX:C_LX + w], ps[:, C_LG:C_LG + w],
                              jnp.moveaxis(state_rglru_conv[l], 1, 0), state_rglru_h[l], prm)

        cmp = _nsa_cmp(proj, prm['nsa_wcmp'], b, t)
        oc_p = _nsa_prompt(proj, cmp, b, t)
        oc_s = _nsa_step_t(page_table, l, nsa_t, win_t, row3(C_NQ, C_NQ + w), row3(C_NKV, C_NKV + w),
                           row3(C_NWIN, C_NWIN + 256), row3(C_SMALL, C_SMALL + 128),
                           jnp.tile(params['nsa_w_cmp'][l], (1, PAGE_SIZE // NSA_BLOCK)))

        od_p, s_p = _gdn_prompt(proj, prm, b, t)
        od_s, s_s = _gdn_step(row3(C_DQKV, C_DQKV + 3 * w), row3(C_DG, C_DG + w),
                              row3(C_SMALL, C_SMALL + 128), state_gdn_conv[l], state_gdn_S[l], prm)

        xp = _out_proj(oa_p, ob_p, oc_p, od_p, xp, prm['w_out'], prm['norm_post_mix'])
        xp = _mlp(xp, prm['norm_pre_mlp'], prm['w_up'], prm['w_down'], prm['norm_post_mlp'])
        flat = lambda a: a.reshape(ns, w)
        xs = _out_proj(flat(oa_s), ob_s, flat(oc_s), flat(od_s), xs, prm['w_out'], prm['norm_post_mix'])
        xs = _mlp(xs, prm['norm_pre_mlp'], prm['w_up'], prm['w_down'], prm['norm_post_mlp'])

        new_win_s = ps[:, C_NWIN:C_NWIN + 256].reshape(ns, 1, 2, KVH_NSA, HEAD_DIM)
        win_all = jnp.concatenate([cache_nsa_win[l], new_win_s], axis=1)
        layer_out = [
            pp[:, :, C_FK:C_FK + 2 * w].reshape(b, t, 2, N_HEADS, HEAD_DIM),
            ps[:, C_FK:C_FK + 2 * w].reshape(ns, 1, 2, N_HEADS, HEAD_DIM),
            logf_p[:, :N_HEADS].reshape(b, t, N_HEADS),
            logf_s[:, :, :N_HEADS],
            pp[:, :, C_NKV:C_NKV + w].reshape(b, t, 4, KVH_NSA, HEAD_DIM),
            ps[:, C_NKV:C_NKV + w].reshape(ns, 1, 4, KVH_NSA, HEAD_DIM),
            pp[:, t - min(NSA_WINDOW, t):, C_NWIN:C_NWIN + 256].reshape(b, min(NSA_WINDOW, t), 2, KVH_NSA, HEAD_DIM),
            win_all[:, -NSA_WINDOW:],
            pp[:, t - (CONV_W - 1):, C_LX:C_LX + w],
            jnp.concatenate([state_rglru_conv[l][:, 1:], ps[:, None, C_LX:C_LX + w]], axis=1),
            h_p[:, 0],
            h_s,
            pp[:, t - (CONV_W - 1):, C_DQKV:C_DQKV + 3 * w],
            jnp.concatenate([state_gdn_conv[l][:, 1:], ps[:, None, C_DQKV:C_DQKV + 3 * w]], axis=1),
            s_p,
            s_s,
        ]
        for i, a in enumerate(layer_out):
            outs[i].append(a)
    y_prompt = xp.reshape(b, t, d)
    y_sample = xs.reshape(ns, 1, d)
    return (y_prompt, y_sample) + tuple(jnp.stack(o) for o in outs)
```

```python
import functools

import jax
import jax.numpy as jnp
from jax import lax
from jax.experimental import pallas as pl
from jax.experimental.pallas import tpu as pltpu

F32 = jnp.float32
BF16 = jnp.bfloat16
HI = lax.Precision.HIGHEST

D_MODEL = 2048
GROUP_WIDTH = 512
HEAD_DIM = 64
N_HEADS = 8
KVH_NSA = 2
NSA_GROUP = 4
NSA_BLOCK = 64
NSA_TOPN = 16
NSA_WINDOW = 512
FORCED_SCORE = 1e4
H_GDN = 4
GDN_DK = 128
GDN_CHUNK = 64
LRU_C = 8.0
CONV_W = 4
D_FF = 4 * D_MODEL
RMS_EPS = 1e-6
NEG = -1e30
ATT_SCALE = HEAD_DIM ** -0.5
PAGE_SIZE = 128

C_FQ, C_FK, C_FV = 0, 512, 1024
C_LX, C_LG = 1536, 2048
C_NQ = 2560
C_NKV = 3072
C_NWIN = 3584
C_SMALL = 3840
C_DQKV = 4096
C_DG = 5632
N_PROJ = 6144
L_FF, L_NG, L_DA, L_DB = 0, 8, 32, 36

VMEM_LIMIT = 56 * 1024 * 1024


def _cparams(sem):
    return pltpu.CompilerParams(dimension_semantics=sem, vmem_limit_bytes=VMEM_LIMIT)


def _pick(n, prefs):
    for p in prefs:
        if n % p == 0:
            return p
    return n


def _rms(xf, g):
    return xf * lax.rsqrt(jnp.mean(xf * xf, axis=-1, keepdims=True) + RMS_EPS) * g


def _softplus(x):
    return jnp.maximum(x, 0.0) + jnp.log1p(jnp.exp(-jnp.abs(x)))


def _log_sigmoid(x):
    return -_softplus(-x)


def _iota(shape, dim):
    return lax.broadcasted_iota(jnp.int32, shape, dim)


def _dot(a, b, **kw):
    return jnp.dot(a, b, preferred_element_type=F32, **kw)


def _dot3(a, b):
    a_hi = a.astype(BF16)
    b_hi = b.astype(BF16)
    a_lo = (a - a_hi.astype(F32)).astype(BF16)
    b_lo = (b - b_hi.astype(F32)).astype(BF16)
    return _dot(a_hi, b_hi) + (_dot(a_hi, b_lo) + _dot(a_lo, b_hi))


def _dot_nt(a, b):
    return lax.dot_general(a, b, (((1,), (1,)), ((), ())), preferred_element_type=F32)


def _dot_tn(a, b):
    return lax.dot_general(a, b, (((0,), (0,)), ((), ())), preferred_element_type=F32)


def _in_proj_kernel(x_ref, g_ref, w_ref, o_ref, h_ref):
    @pl.when(pl.program_id(1) == 0)
    def _():
        h_ref[...] = _rms(x_ref[...], g_ref[...]).astype(BF16)

    o_ref[...] = _dot(h_ref[...], w_ref[...])


def _in_proj(x, g, w):
    m, d = x.shape
    n = w.shape[1]
    tm = _pick(m, (1024, 512, 256, 128, 64, 32, 16, 8))
    tn = _pick(n, (1536, 1024, 768, 512, 256, 128))
    return pl.pallas_call(
        _in_proj_kernel,
        grid=(m // tm, n // tn),
        in_specs=[pl.BlockSpec((tm, d), lambda i, j: (i, 0)),
                  pl.BlockSpec((1, d), lambda i, j: (0, 0)),
                  pl.BlockSpec((d, tn), lambda i, j: (0, j))],
        out_specs=pl.BlockSpec((tm, tn), lambda i, j: (i, j)),
        out_shape=jax.ShapeDtypeStruct((m, n), F32),
        scratch_shapes=[pltpu.VMEM((tm, d), BF16)],
        compiler_params=_cparams(("parallel", "arbitrary")),
        name="in_proj",
    )(x, g, w)


def _out_proj_kernel(oa_ref, ob_ref, oc_ref, od_ref, x_ref, w_ref, g_ref, o_ref):
    y = _dot(oa_ref[...].astype(BF16), w_ref[0:512, :])
    y = y + _dot(ob_ref[...].astype(BF16), w_ref[512:1024, :])
    y = y + _dot(oc_ref[...].astype(BF16), w_ref[1024:1536, :])
    y = y + _dot(od_ref[...].astype(BF16), w_ref[1536:2048, :])
    o_ref[...] = x_ref[...] + _rms(y, g_ref[...])


def _out_proj(oa, ob, oc, od, x, w, g):
    m, d = x.shape
    tm = _pick(m, (640, 512, 256, 128, 64, 32, 16, 8))
    gw = GROUP_WIDTH
    return pl.pallas_call(
        _out_proj_kernel,
        grid=(m // tm,),
        in_specs=[pl.BlockSpec((tm, gw), lambda i: (i, 0))] * 4 + [
            pl.BlockSpec((tm, d), lambda i: (i, 0)),
            pl.BlockSpec((d, d), lambda i: (0, 0)),
            pl.BlockSpec((1, d), lambda i: (0, 0))],
        out_specs=pl.BlockSpec((tm, d), lambda i: (i, 0)),
        out_shape=jax.ShapeDtypeStruct((m, d), F32),
        compiler_params=_cparams(("parallel",)),
        name="out_proj",
    )(oa, ob, oc, od, x, w, g)


def _mlp_kernel(x_ref, g1_ref, wu_ref, wd_ref, g2_ref, o_ref, h_ref, acc_ref):
    f = pl.program_id(1)

    @pl.when(f == 0)
    def _():
        h_ref[...] = _rms(x_ref[...], g1_ref[...]).astype(BF16)
        acc_ref[...] = jnp.zeros_like(acc_ref)

    u = jnp.maximum(_dot(h_ref[...], wu_ref[...]), 0.0)
    acc_ref[...] += _dot((u * u).astype(BF16), wd_ref[...])

    @pl.when(f == pl.num_programs(1) - 1)
    def _():
        o_ref[...] = x_ref[...] + _rms(acc_ref[...], g2_ref[...])


def _mlp(x, g1, wu, wd, g2):
    m, d = x.shape
    ff = wu.shape[1]
    tm = _pick(m, (640, 512, 256, 128, 64, 32, 16, 8))
    tf = _pick(ff, (512, 256, 128))
    return pl.pallas_call(
        _mlp_kernel,
        grid=(m // tm, ff // tf),
        in_specs=[pl.BlockSpec((tm, d), lambda i, f: (i, 0)),
                  pl.BlockSpec((1, d), lambda i, f: (0, 0)),
                  pl.BlockSpec((d, tf), lambda i, f: (0, f)),
                  pl.BlockSpec((tf, d), lambda i, f: (f, 0)),
                  pl.BlockSpec((1, d), lambda i, f: (0, 0))],
        out_specs=pl.BlockSpec((tm, d), lambda i, f: (i, 0)),
        out_shape=jax.ShapeDtypeStruct((m, d), F32),
        scratch_shapes=[pltpu.VMEM((tm, d), BF16), pltpu.VMEM((tm, d), F32)],
        compiler_params=_cparams(("parallel", "arbitrary")),
        name="mlp",
    )(x, g1, wu, wd, g2)


def _fox_prep_kernel(x_ref, s_ref, bf_ref, qa_ref, ka_ref, v_ref, logf_ref, carry_ref, *, tb):
    @pl.when(pl.program_id(1) == 0)
    def _():
        carry_ref[...] = jnp.zeros_like(carry_ref)

    logf = _log_sigmoid(s_ref[...] + bf_ref[...])
    logf_ref[...] = logf
    tri = (_iota((tb, tb), 1) <= _iota((tb, tb), 0)).astype(F32)
    c = _dot(tri, logf, precision=HI) + carry_ref[...]
    carry_ref[...] = c[tb - 1:tb, :]
    hi = c.astype(BF16).astype(F32)
    r1 = c - hi
    mid = r1.astype(BF16).astype(F32)
    lo = r1 - mid
    lane = _iota((tb, HEAD_DIM), 1)
    q_tail = jnp.where(lane < 3, 1.0, 0.0).astype(BF16)
    x = x_ref[...]
    w = GROUP_WIDTH
    for h in range(N_HEADS):
        sl = slice(h * HEAD_DIM, (h + 1) * HEAD_DIM)
        qa_ref[h] = jnp.concatenate([(x[:, sl] * ATT_SCALE).astype(BF16), q_tail], axis=1)
        k_tail = jnp.where(lane == 0, -hi[:, h:h + 1],
                           jnp.where(lane == 1, -mid[:, h:h + 1],
                                     jnp.where(lane == 2, -lo[:, h:h + 1], 0.0)))
        ka_ref[h] = jnp.concatenate([x[:, w + h * HEAD_DIM:w + (h + 1) * HEAD_DIM].astype(BF16),
                                     k_tail.astype(BF16)], axis=1)
        v_ref[h] = x[:, 2 * w + h * HEAD_DIM:2 * w + (h + 1) * HEAD_DIM].astype(BF16)


def _fox_prep(proj, bf_row, b, t):
    tb = _pick(t, (512, 256, 128, 64))
    nt = t // tb
    w = GROUP_WIDTH
    aug = jax.ShapeDtypeStruct((b, N_HEADS, t, 128), BF16)
    return pl.pallas_call(
        functools.partial(_fox_prep_kernel, tb=tb),
        grid=(b, nt),
        in_specs=[pl.BlockSpec((tb, 3 * w), lambda i, j: (i * nt + j, C_FQ // (3 * w))),
                  pl.BlockSpec((tb, 128), lambda i, j: (i * nt + j, C_SMALL // 128)),
                  pl.BlockSpec((1, 128), lambda i, j: (0, 0))],
        out_specs=[pl.BlockSpec((None, N_HEADS, tb, 128), lambda i, j: (i, 0, j, 0)),
                   pl.BlockSpec((None, N_HEADS, tb, 128), lambda i, j: (i, 0, j, 0)),
                   pl.BlockSpec((None, N_HEADS, tb, HEAD_DIM), lambda i, j: (i, 0, j, 0)),
                   pl.BlockSpec((tb, 128), lambda i, j: (i * nt + j, 0))],
        out_shape=[aug, aug, jax.ShapeDtypeStruct((b, N_HEADS, t, HEAD_DIM), BF16),
                   jax.ShapeDtypeStruct((b * t, 128), F32)],
        scratch_shapes=[pltpu.VMEM((1, 128), F32)],
        compiler_params=_cparams(("parallel", "arbitrary")),
        name="fox_prep",
    )(proj, proj, bf_row)


def _fox_flash_kernel(qi_ref, ki_ref, q_ref, k_ref, v_ref, o_ref, m_ref, l_ref, acc_ref, *, nt):
    step_id = pl.program_id(2)
    qi = qi_ref[step_id]
    ki = ki_ref[step_id]

    @pl.when(ki == 0)
    def _():
        m_ref[...] = jnp.full_like(m_ref, NEG)
        l_ref[...] = jnp.zeros_like(l_ref)
        acc_ref[...] = jnp.zeros_like(acc_ref)

    def step(diag):
        for j in range(2):
            s = _dot_nt(q_ref[j], k_ref[j])
            if diag:
                s = jnp.where(_iota((nt, nt), 1) <= _iota((nt, nt), 0), s, NEG)
            m_prev = m_ref[j]
            m_new = jnp.maximum(m_prev, jnp.max(s, axis=-1, keepdims=True))
            alpha = jnp.exp(m_prev - m_new)
            p = jnp.exp(s - m_new)
            l_ref[j] = alpha * l_ref[j] + jnp.sum(p, axis=-1, keepdims=True)
            acc_ref[j] = alpha * acc_ref[j] + _dot(p.astype(BF16), v_ref[j])
            m_ref[j] = m_new

    @pl.when(ki < qi)
    def _():
        step(False)

    @pl.when(ki == qi)
    def _():
        step(True)
        o_ref[...] = jnp.concatenate([acc_ref[j] / l_ref[j] for j in range(2)], axis=1)


def _causal_pairs(n):
    qs = [q for q in range(n) for _ in range(q + 1)]
    ks = [k for q in range(n) for k in range(q + 1)]
    return jnp.asarray(qs, jnp.int32), jnp.asarray(ks, jnp.int32)


def _fox_flash(qa, ka, v, b, t):
    nt = _pick(t, (1024, 512, 256, 128))
    n = t // nt
    qi, ki = _causal_pairs(n)
    hp = N_HEADS // 2
    return pl.pallas_call(
        functools.partial(_fox_flash_kernel, nt=nt),
        grid_spec=pltpu.PrefetchScalarGridSpec(
            num_scalar_prefetch=2, grid=(b, hp, qi.shape[0]),
            in_specs=[pl.BlockSpec((None, 2, nt, 128), lambda i, h, s, qr, kr: (i, h, qr[s], 0)),
                      pl.BlockSpec((None, 2, nt, 128), lambda i, h, s, qr, kr: (i, h, kr[s], 0)),
                      pl.BlockSpec((None, 2, nt, HEAD_DIM), lambda i, h, s, qr, kr: (i, h, kr[s], 0))],
            out_specs=pl.BlockSpec((nt, 128), lambda i, h, s, qr, kr: (i * n + qr[s], h)),
            scratch_shapes=[pltpu.VMEM((2, nt, 1), F32), pltpu.VMEM((2, nt, 1), F32),
                            pltpu.VMEM((2, nt, HEAD_DIM), F32)]),
        out_shape=jax.ShapeDtypeStruct((b * t, GROUP_WIDTH), F32),
        compiler_params=_cparams(("parallel", "parallel", "arbitrary")),
        name="fox_flash",
    )(qi, ki, qa, ka, v)


def _lru_gates(xc, wr, br, wi, bi, lam):
    xb = xc.astype(BF16)
    r = jax.nn.sigmoid(_dot(xb, wr) + br)
    ig = jax.nn.sigmoid(_dot(xb, wi) + bi)
    log_a = -LRU_C * r * _softplus(-lam)
    a = jnp.exp(log_a)
    th = jnp.tanh(log_a)
    u = jnp.sqrt(-2.0 * th / (1.0 - th)) * (ig * xc)
    return a, u


def _conv4(w_ref, x0, x1, x2, x3, cols=slice(None)):
    y = 0.0 + w_ref[0:1, cols] * x0
    y = y + w_ref[1:2, cols] * x1
    y = y + w_ref[2:3, cols] * x2
    return y + w_ref[3:4, cols] * x3


def _lru_prompt_kernel(x_ref, g_ref, cw_ref, cb_ref, wr_ref, br_ref, wi_ref, bi_ref, lam_ref,
                       y_ref, hfin_ref, xbuf, a_s, u_s, hs, h_s, *, tb):
    @pl.when(pl.program_id(1) == 0)
    def _():
        xbuf[0:8, :] = jnp.zeros((8, GROUP_WIDTH), F32)
        h_s[...] = jnp.zeros_like(h_s)

    xbuf[8:8 + tb, :] = x_ref[...]
    xc = _conv4(cw_ref, xbuf[5:5 + tb, :], xbuf[6:6 + tb, :], xbuf[7:7 + tb, :], xbuf[8:8 + tb, :])
    xc = xc + cb_ref[...]
    xbuf[0:8, :] = xbuf[tb:tb + 8, :]
    a, u = _lru_gates(xc, wr_ref[...], br_ref[...], wi_ref[...], bi_ref[...], lam_ref[...])
    a_s[...] = a
    u_s[...] = u

    def body(t, h):
        h = a_s[pl.ds(t, 1), :] * h + u_s[pl.ds(t, 1), :]
        hs[pl.ds(t, 1), :] = h
        return h

    h = lax.fori_loop(0, tb, body, h_s[...], unroll=8)
    h_s[...] = h
    y_ref[...] = hs[...] * jax.nn.gelu(g_ref[...])
    hfin_ref[...] = jnp.broadcast_to(h, (8, GROUP_WIDTH))


def _lru_prompt(proj, prm, b, t):
    tb = _pick(t, (512, 256, 128, 64))
    nt = t // tb
    w = GROUP_WIDTH
    full = lambda shape: pl.BlockSpec(shape, lambda i, j: (0,) * len(shape))
    return pl.pallas_call(
        functools.partial(_lru_prompt_kernel, tb=tb),
        grid=(b, nt),
        in_specs=[pl.BlockSpec((tb, w), lambda i, j: (i * nt + j, C_LX // w)),
                  pl.BlockSpec((tb, w), lambda i, j: (i * nt + j, C_LG // w)),
                  full((CONV_W, w)), full((1, w)), full((w, w)), full((1, w)), full((w, w)),
                  full((1, w)), full((1, w))],
        out_specs=[pl.BlockSpec((tb, w), lambda i, j: (i * nt + j, 0)),
                   pl.BlockSpec((None, 8, w), lambda i, j: (i, 0, 0))],
        out_shape=[jax.ShapeDtypeStruct((b * t, w), F32), jax.ShapeDtypeStruct((b, 8, w), F32)],
        scratch_shapes=[pltpu.VMEM((tb + 8, w), F32), pltpu.VMEM((tb, w), F32), pltpu.VMEM((tb, w), F32),
                        pltpu.VMEM((tb, w), F32), pltpu.VMEM((1, w), F32)],
        compiler_params=_cparams(("parallel", "arbitrary")),
        name="lru_prompt",
    )(proj, proj, prm['lru_conv_w'], prm['lru_conv_b'], prm['lru_wr'], prm['lru_b_r'], prm['lru_wi'],
      prm['lru_b_i'], prm['lru_lambda'])


def _lru_step_kernel(x_ref, g_ref, buf_ref, h0_ref, cw_ref, cb_ref, wr_ref, br_ref, wi_ref, bi_ref,
                     lam_ref, y_ref, h_ref):
    xc = _conv4(cw_ref, buf_ref[0], buf_ref[1], buf_ref[2], x_ref[...]) + cb_ref[...]
    a, u = _lru_gates(xc, wr_ref[...], br_ref[...], wi_ref[...], bi_ref[...], lam_ref[...])
    h = a * h0_ref[...] + u
    h_ref[...] = h
    y_ref[...] = h * jax.nn.gelu(g_ref[...])


def _lru_step(x, g, buf, h0, prm):
    n, w = x.shape
    out = jax.ShapeDtypeStruct((n, w), F32)
    return pl.pallas_call(
        _lru_step_kernel, out_shape=[out, out], name="lru_step",
        compiler_params=pltpu.CompilerParams(vmem_limit_bytes=VMEM_LIMIT),
    )(x, g, buf, h0, prm['lru_conv_w'], prm['lru_conv_b'], prm['lru_wr'], prm['lru_b_r'], prm['lru_wi'],
      prm['lru_b_i'], prm['lru_lambda'])


def _nsa_cmp_kernel(kv_ref, w_ref, o_ref, *, tb):
    x = kv_ref[...].reshape(tb // NSA_BLOCK, NSA_BLOCK, 256)
    o_ref[...] = jnp.sum(x * w_ref[...][None], axis=1)


def _nsa_cmp(proj, wcmp, b, t):
    tb = _pick(t, (512,))
    nt = t // tb
    nb = tb // NSA_BLOCK
    return pl.pallas_call(
        functools.partial(_nsa_cmp_kernel, tb=tb),
        grid=(b, nt),
        in_specs=[pl.BlockSpec((tb, 256), lambda i, j: (i * nt + j, C_NKV // 256)),
                  pl.BlockSpec((NSA_BLOCK, 256), lambda i, j: (0, 0))],
        out_specs=pl.BlockSpec((None, nb, 256), lambda i, j: (i, j, 0)),
        out_shape=jax.ShapeDtypeStruct((b, t // NSA_BLOCK, 256), F32),
        compiler_params=_cparams(("parallel", "parallel")),
        name="nsa_cmp",
    )(proj, wcmp)


def _slope(h):
    return float(2.0 ** (-8.0 * (h + 1) / N_HEADS))


def _nsa_prompt_kernel(qi_ref, ki_ref, q_ref, ks_ref, vs_ref, kw_ref, vw_ref, cmp_ref, sm_ref, o_ref,
                       qs, oc, sel, m_s, l_s, acc_s, m_w, l_w, acc_w, tile_hit, *, nq, nk, nb):
    step_id = pl.program_id(1)
    qi = qi_ref[step_id]
    ki = ki_ref[step_id]
    q0 = qi * nq
    k0 = ki * nk
    k_last = (q0 + nq - 1) // nk
    n_sel = min(NSA_TOPN, nb)
    rows4 = NSA_GROUP * nq
    hd = HEAD_DIM
    head_in_group = _iota((rows4, 1), 0) // nq

    def slope_col(g):
        return jnp.exp2(-(NSA_GROUP * g + head_in_group + 1).astype(F32) * (8.0 / N_HEADS))

    @pl.when(ki == 0)
    def _():
        q = q_ref[...] * ATT_SCALE
        cmp = cmp_ref[...]
        tpos = q0 + (_iota((rows4, nb), 0) % nq)
        blk = _iota((rows4, nb), 1)
        dist_c = tpos - ((blk + 1) * NSA_BLOCK - 1)
        valid = dist_c >= 0
        dist_cf = dist_c.astype(F32)
        lane = _iota((rows4, hd), 1)
        scores = []
        for g in range(KVH_NSA):
            qg = jnp.concatenate([q[:, (NSA_GROUP * g + i) * hd:(NSA_GROUP * g + i + 1) * hd]
                                  for i in range(NSA_GROUP)], axis=0)
            tail = jnp.where(lane < 2, slope_col(g), 0.0)
            qs[g] = jnp.concatenate([qg, tail] if g == 0 else [tail, qg], axis=1).astype(BF16)
            kc = cmp[:, g * hd:(g + 1) * hd].astype(BF16)
            vc = cmp[:, 128 + g * hd:128 + (g + 1) * hd].astype(BF16)
            s = _dot_nt(qg.astype(BF16), kc) - slope_col(g) * dist_cf
            s = jnp.where(valid, s, NEG)
            m = jnp.max(s, axis=-1, keepdims=True)
            e = jnp.where(valid, jnp.exp(s - m), 0.0)
            l = jnp.sum(e, axis=-1, keepdims=True)
            p = e / jnp.where(l > 0.0, l, 1.0)
            oc[g] = _dot(p.astype(BF16), vc)
            imp = p[0:nq]
            for i in range(1, NSA_GROUP):
                imp = imp + p[i * nq:(i + 1) * nq]
            scores.append(imp)
        imp2 = jnp.concatenate(scores, axis=1)
        lane2 = _iota((nq, KVH_NSA * nb), 1)
        blk2 = lane2 % nb
        cur = (q0 + _iota((nq, KVH_NSA * nb), 0)) // NSA_BLOCK
        forced = (blk2 == 0) | (blk2 >= cur - 1)
        score = jnp.where(blk2 <= cur, jnp.where(forced, FORCED_SCORE, imp2), -1.0)
        score_t = score.T
        rg = 8 if nb % 8 == 0 else nb
        for g in range(KVH_NSA):
            s_g = score_t[g * nb:(g + 1) * nb, :]
            for j0 in range(0, nb, rg):
                sub = s_g[j0:j0 + rg, :]
                n_idx = j0 + _iota((rg, nq), 0)
                cnt = jnp.zeros((rg, nq), F32)
                for mm in range(nb):
                    r = s_g[mm:mm + 1, :]
                    ge = jnp.where(r >= sub, 1.0, 0.0)
                    gt = jnp.where(r > sub, 1.0, 0.0)
                    if j0 > mm:
                        cnt = cnt + ge
                    elif j0 + rg - 1 <= mm:
                        cnt = cnt + gt
                    else:
                        cnt = cnt + jnp.where(n_idx > mm, ge, gt)
                sel[g * nb + j0:g * nb + j0 + rg, :] = jnp.where(
                    cnt < n_sel, jnp.where(sub >= 0.0, 1.0, 0.0), 0.0)
        bpt = nk // NSA_BLOCK
        for kt in range(nb // bpt):
            hit = jnp.maximum(sel[kt * bpt:(kt + 1) * bpt, :], sel[nb + kt * bpt:nb + (kt + 1) * bpt, :])
            tile_hit[kt] = (jnp.max(hit) > 0.5).astype(jnp.int32)
        for ref in (m_s, m_w):
            ref[...] = jnp.full_like(ref, NEG)
        for ref in (l_s, acc_s, l_w, acc_w):
            ref[...] = jnp.zeros_like(ref)

    def aug_keys(k_ref):
        k = k_ref[...]
        c = _iota((nk, 2 * hd), 0)
        lane = _iota((nk, 2 * hd), 1)
        c_hi = ((c // 256) * 256).astype(F32)
        c_lo = (c % 256).astype(F32)
        k_g0 = jnp.where(lane < hd, k, jnp.where(lane == hd, c_hi, jnp.where(lane == hd + 1, c_lo, 0.0)))
        k_g1 = jnp.where(lane >= hd, k, jnp.where(lane == 0, c_hi, jnp.where(lane == 1, c_lo, 0.0)))
        return k_g0.astype(BF16), k_g1.astype(BF16)

    def attend(g, k_aug, v_bf, bias, m_ref, l_ref, acc_ref):
        head = NSA_GROUP * g + _iota((1, rows4), 1) // nq
        shift = jnp.exp2(-(head + 1).astype(F32) * (8.0 / N_HEADS)) * (k0 - q0).astype(F32)
        s = _dot_nt(k_aug, qs[g])
        s = jnp.concatenate([s[:, i * nq:(i + 1) * nq] + bias for i in range(NSA_GROUP)], axis=1)
        m_prev = m_ref[g]
        m_new = jnp.maximum(m_prev, jnp.max(s, axis=0, keepdims=True) + shift)
        alpha = jnp.exp(m_prev - m_new)
        p = jnp.exp(s - (m_new - shift))
        l_ref[g] = alpha * l_ref[g] + jnp.sum(p, axis=0, keepdims=True)
        acc_ref[g] = alpha * acc_ref[g] + _dot_tn(v_bf, p.astype(BF16))
        m_ref[g] = m_new

    def dist_tile():
        return (q0 + _iota((nk, nq), 1)) - (k0 + _iota((nk, nq), 0))

    @pl.when((ki <= k_last) & (tile_hit[ki] > 0))
    def _():
        causal = dist_tile() >= 0
        k_aug = aug_keys(ks_ref)
        v_bf = vs_ref[...].astype(BF16)
        sel_bf = sel[...].astype(BF16)
        key_blk = (k0 + _iota((nk, KVH_NSA * nb), 0)) // NSA_BLOCK
        col = _iota((nk, KVH_NSA * nb), 1)
        for g in range(KVH_NSA):
            onehot = (col == g * nb + key_blk).astype(BF16)
            picked = _dot(onehot, sel_bf) > 0.5
            bias = jnp.where(causal, jnp.where(picked, 0.0, NEG), NEG)
            attend(g, k_aug[g], v_bf, bias, m_s, l_s, acc_s)

    @pl.when((ki <= k_last) & (ki >= k_last - (NSA_WINDOW + nk - 1) // nk))
    def _():
        dist = dist_tile()
        bias = jnp.where(dist >= 0, jnp.where(dist < NSA_WINDOW, 0.0, NEG), NEG)
        k_aug = aug_keys(kw_ref)
        v_bf = vw_ref[...].astype(BF16)
        for g in range(KVH_NSA):
            attend(g, k_aug[g], v_bf, bias, m_w, l_w, acc_w)

    @pl.when(ki == k_last)
    def _():
        gates = jax.nn.sigmoid(sm_ref[...])
        outs = []
        for g in range(KVH_NSA):
            o_s = (acc_s[g] / l_s[g]).T[:, g * hd:(g + 1) * hd]
            o_w = (acc_w[g] / l_w[g]).T[:, g * hd:(g + 1) * hd]
            o_c = oc[g]
            for i in range(NSA_GROUP):
                rows = slice(i * nq, (i + 1) * nq)
                c = L_NG + 3 * (NSA_GROUP * g + i)
                o = gates[:, c:c + 1] * o_c[rows]
                outs.append(o + (gates[:, c + 1:c + 2] * o_s[rows] + gates[:, c + 2:c + 3] * o_w[rows]))
        o_ref[...] = jnp.concatenate(outs, axis=1)


def _nsa_prompt(proj, cmp, b, t):
    nq = _pick(t, (256, 128))
    nk = _pick(t, (512, 256, 128))
    n_q = t // nq
    n_k = t // nk
    nb = t // NSA_BLOCK
    last = lambda q: (q * nq + nq - 1) // nk
    win_tiles = (NSA_WINDOW + nk - 1) // nk
    pairs = [(q, k) for q in range(n_q) for k in range(last(q) + 1)]
    qi = jnp.asarray([p[0] for p in pairs], jnp.int32)
    ki = jnp.asarray([p[1] for p in pairs], jnp.int32)
    rows4 = NSA_GROUP * nq

    def widx(q, k):
        kl = (q * nq + nq - 1) // nk
        return jnp.clip(k, jnp.maximum(kl - win_tiles, 0), kl)

    return pl.pallas_call(
        functools.partial(_nsa_prompt_kernel, nq=nq, nk=nk, nb=nb),
        grid_spec=pltpu.PrefetchScalarGridSpec(
            num_scalar_prefetch=2, grid=(b, len(pairs)),
            in_specs=[pl.BlockSpec((nq, 512), lambda i, s, qr, kr: (i * n_q + qr[s], C_NQ // 512)),
                      pl.BlockSpec((nk, 128), lambda i, s, qr, kr: (i * n_k + kr[s], C_NKV // 128 + 2)),
                      pl.BlockSpec((nk, 128), lambda i, s, qr, kr: (i * n_k + kr[s], C_NKV // 128 + 3)),
                      pl.BlockSpec((nk, 128), lambda i, s, qr, kr: (i * n_k + widx(qr[s], kr[s]), C_NWIN // 128)),
                      pl.BlockSpec((nk, 128), lambda i, s, qr, kr: (i * n_k + widx(qr[s], kr[s]), C_NWIN // 128 + 1)),
                      pl.BlockSpec((None, nb, 256), lambda i, s, qr, kr: (i, 0, 0)),
                      pl.BlockSpec((nq, 128), lambda i, s, qr, kr: (i * n_q + qr[s], C_SMALL // 128))],
            out_specs=pl.BlockSpec((nq, 512), lambda i, s, qr, kr: (i * n_q + qr[s], 0)),
            scratch_shapes=[pltpu.VMEM((KVH_NSA, rows4, 2 * HEAD_DIM), BF16),
                            pltpu.VMEM((KVH_NSA, rows4, HEAD_DIM), F32),
                            pltpu.VMEM((KVH_NSA * nb, nq), F32),
                            pltpu.VMEM((KVH_NSA, 1, rows4), F32), pltpu.VMEM((KVH_NSA, 1, rows4), F32),
                            pltpu.VMEM((KVH_NSA, 2 * HEAD_DIM, rows4), F32),
                            pltpu.VMEM((KVH_NSA, 1, rows4), F32), pltpu.VMEM((KVH_NSA, 1, rows4), F32),
                            pltpu.VMEM((KVH_NSA, 2 * HEAD_DIM, rows4), F32),
                            pltpu.SMEM((n_k,), jnp.int32)]),
        out_shape=jax.ShapeDtypeStruct((b * t, GROUP_WIDTH), F32),
        compiler_params=_cparams(("parallel", "arbitrary")),
        name="nsa_prompt",
    )(qi, ki, proj, proj, proj, proj, proj, cmp, proj)


def _col_to_row(col, eye):
    return jnp.sum(jnp.where(eye, col, 0.0), axis=0, keepdims=True)


def _row_to_col(row, eye):
    return jnp.sum(jnp.where(eye, row, 0.0), axis=1, keepdims=True)


def _gdn_qkv(act_q, act_k, h):
    sl = slice(h * GDN_DK, (h + 1) * GDN_DK)
    q = act_q[:, sl]
    k = act_k[:, sl]
    q = q * lax.rsqrt(jnp.sum(q * q, axis=-1, keepdims=True) + 1e-6) * (GDN_DK ** -0.5)
    k = k * lax.rsqrt(jnp.sum(k * k, axis=-1, keepdims=True) + 1e-6)
    return q, k


def _gdn_out(o, nw, gate):
    return _rms(o, nw) * (gate * jax.nn.sigmoid(gate))


def _silu(x):
    return x * jax.nn.sigmoid(x)


def _gdn_pre_kernel(q_ref, k_ref, v_ref, hq_ref, hk_ref, hv_ref, sm_ref, cw_ref, al_ref, dt_ref,
                    u_ref, w_ref, qe_ref, kd_ref, qk_ref, egl_ref, xbuf, act, *, rows):
    c = GDN_CHUNK
    w5 = GROUP_WIDTH
    first = pl.program_id(1) == 0
    for seg, (ref, halo) in enumerate(((q_ref, hq_ref), (k_ref, hk_ref), (v_ref, hv_ref))):
        xbuf[seg, 0:8, :] = jnp.where(first, 0.0, halo[...])
        xbuf[seg, 8:8 + rows, :] = ref[...]
        cols = slice(seg * w5, (seg + 1) * w5)
        act[seg] = _silu(_conv4(cw_ref, xbuf[seg, 5:5 + rows, :], xbuf[seg, 6:6 + rows, :],
                                xbuf[seg, 7:7 + rows, :], xbuf[seg, 8:8 + rows, :], cols))
    ri = _iota((c, c), 0)
    ci = _iota((c, c), 1)
    incl = ri >= ci
    strict = ri > ci
    eye = ri == ci
    incl_f = incl.astype(F32)
    hc = H_GDN * c
    eye_bd = (_iota((hc, hc), 0) == _iota((hc, hc), 1)).astype(F32)

    def chunk(idx, carry):
        r0 = pl.multiple_of(idx * c, c)
        rs = pl.ds(r0, c)
        sm = sm_ref[rs, :]
        beta_all = jax.nn.sigmoid(sm)
        la_all = -jnp.exp(al_ref[...]) * _softplus(sm + dt_ref[...])
        g_all = _dot(incl_f, la_all, precision=HI)
        egl_ref[rs, :] = jnp.broadcast_to(jnp.exp(g_all[c - 1:c, :]), (c, 128))
        act_q = act[0, rs, :]
        act_k = act[1, rs, :]
        act_v = act[2, rs, :]
        qks, a_rows, vbs, kbes = [], [], [], []
        zero = jnp.zeros((c, c), F32)
        for h in range(H_GDN):
            sl = slice(h * GDN_DK, (h + 1) * GDN_DK)
            q, k = _gdn_qkv(act_q, act_k, h)
            v = act_v[:, sl]
            beta = beta_all[:, L_DB + h:L_DB + h + 1]
            g = g_all[:, L_DA + h:L_DA + h + 1]
            g_row = _col_to_row(g, eye)
            decay = jnp.exp(jnp.where(incl, g - g_row, -jnp.inf))
            kb = k * beta
            a = jnp.where(strict, _dot_nt(kb, k) * decay, 0.0)
            a_rows.append(jnp.concatenate([a if j == h else zero for j in range(H_GDN)], axis=1))
            eg = jnp.exp(g)
            vbs.append(v * beta)
            kbes.append(kb * eg)
            qe_ref[rs, sl] = q * eg
            kd_ref[rs, sl] = k * jnp.exp(g[c - 1:c, :] - g)
            qks.append(_dot_nt(q, k) * decay)
        qk_ref[rs, :] = jnp.concatenate(qks, axis=1)
        a_bd = jnp.concatenate(a_rows, axis=0)
        tm = eye_bd - a_bd
        pw = _dot3(a_bd, a_bd)
        for it in range(5):
            tm = tm + _dot3(tm, pw)
            if it < 4:
                pw = _dot3(pw, pw)
        u_all = _dot(tm, jnp.concatenate(vbs, axis=0))
        w_all = _dot(tm, jnp.concatenate(kbes, axis=0))
        for h in range(H_GDN):
            sl = slice(h * GDN_DK, (h + 1) * GDN_DK)
            u_ref[rs, sl] = u_all[h * c:(h + 1) * c, :]
            w_ref[rs, sl] = w_all[h * c:(h + 1) * c, :]
        return carry

    lax.fori_loop(0, rows // c, chunk, 0, unroll=2)


def _gdn_seq_kernel(u_ref, w_ref, qe_ref, kd_ref, qk_ref, egl_ref, g_ref, nw_ref, o_ref, s_ref, *, rows):
    c = GDN_CHUNK

    @pl.when(pl.program_id(1) == 0)
    def _():
        s_ref[...] = jnp.zeros_like(s_ref)

    nw = nw_ref[...]

    def chunk(idx, carry):
        r0 = pl.multiple_of(idx * c, c)
        rs = pl.ds(r0, c)
        egl = egl_ref[pl.ds(r0, 1), :]
        for h in range(H_GDN):
            sl = slice(h * GDN_DK, (h + 1) * GDN_DK)
            s0 = s_ref[h]
            u_new = u_ref[rs, sl] - _dot(w_ref[rs, sl], s0)
            o = _dot(qe_ref[rs, sl], s0) + _dot(qk_ref[rs, h * c:(h + 1) * c], u_new)
            s_ref[h] = s0 * egl[:, L_DA + h:L_DA + h + 1] + _dot_tn(kd_ref[rs, sl], u_new)
            o_ref[rs, sl] = _gdn_out(o, nw, g_ref[rs, sl])
        return carry

    lax.fori_loop(0, rows // c, chunk, 0)


def _gdn_prompt(proj, prm, b, t):
    c = GDN_CHUNK
    w = GROUP_WIDTH
    rows = _pick(t, (512, 256, 128, 64))
    nr = t // rows
    full = lambda shape: pl.BlockSpec(shape, lambda i, j: (0,) * len(shape))
    seg = lambda col: pl.BlockSpec((rows, w), lambda i, j: (i * nr + j, col // w))
    halo = lambda col: pl.BlockSpec(
        (8, w), lambda i, j: (jnp.maximum((i * nr + j) * (rows // 8) - 1, 0), col // w))
    blk = lambda width: pl.BlockSpec((rows, width), lambda i, j: (i * nr + j, 0))
    f32 = lambda width: jax.ShapeDtypeStruct((b * t, width), F32)
    u, wk, qe, kd, qk, egl = pl.pallas_call(
        functools.partial(_gdn_pre_kernel, rows=rows),
        grid=(b, nr),
        in_specs=[seg(C_DQKV), seg(C_DQKV + w), seg(C_DQKV + 2 * w),
                  halo(C_DQKV), halo(C_DQKV + w), halo(C_DQKV + 2 * w),
                  pl.BlockSpec((rows, 128), lambda i, j: (i * nr + j, C_SMALL // 128)),
                  full((CONV_W, 3 * w)), full((1, 128)), full((1, 128))],
        out_specs=[blk(w), blk(w), blk(w), blk(w), blk(H_GDN * c), blk(128)],
        out_shape=[f32(w), f32(w), f32(w), f32(w), f32(H_GDN * c), f32(128)],
        scratch_shapes=[pltpu.VMEM((3, rows + 8, w), F32), pltpu.VMEM((3, rows, w), F32)],
        compiler_params=_cparams(("parallel", "parallel")),
        name="gdn_pre",
    )(proj, proj, proj, proj, proj, proj, proj, prm['gdn_conv_w'], prm['gdn_alog_row'], prm['gdn_dt_row'])
    return pl.pallas_call(
        functools.partial(_gdn_seq_kernel, rows=rows),
        grid=(b, nr),
        in_specs=[blk(w), blk(w), blk(w), blk(w), blk(H_GDN * c), blk(128), seg(C_DG), full((1, GDN_DK))],
        out_specs=[blk(w), pl.BlockSpec((None, H_GDN, GDN_DK, GDN_DK), lambda i, j: (i, 0, 0, 0))],
        out_shape=[f32(w), jax.ShapeDtypeStruct((b, H_GDN, GDN_DK, GDN_DK), F32)],
        compiler_params=_cparams(("parallel", "arbitrary")),
        name="gdn_seq",
    )(u, wk, qe, kd, qk, egl, proj, prm['gdn_norm_w'])


def _gdn_step_kernel(x_ref, g_ref, sm_ref, buf_ref, s0_ref, cw_ref, al_ref, dt_ref, nw_ref,
                     o_ref, s_ref):
    eye = _iota((GDN_DK, GDN_DK), 0) == _iota((GDN_DK, GDN_DK), 1)
    w = GROUP_WIDTH
    for smp in range(x_ref.shape[0]):
        buf = buf_ref[smp]
        act = _silu(_conv4(cw_ref, buf[0:1], buf[1:2], buf[2:3], x_ref[smp]))
        sm = sm_ref[smp]
        beta_all = jax.nn.sigmoid(sm)
        alpha_all = jnp.exp(-jnp.exp(al_ref[...]) * _softplus(sm + dt_ref[...]))
        gate = g_ref[smp]
        outs = []
        for h in range(H_GDN):
            sl = slice(h * GDN_DK, (h + 1) * GDN_DK)
            q, k = _gdn_qkv(act[:, 0:w], act[:, w:2 * w], h)
            v = act[:, 2 * w + h * GDN_DK:2 * w + (h + 1) * GDN_DK]
            beta = beta_all[:, L_DB + h:L_DB + h + 1]
            alpha = alpha_all[:, L_DA + h:L_DA + h + 1]
            k_col = _row_to_col(k, eye)
            q_col = _row_to_col(q, eye)
            s0 = s0_ref[smp, h]
            ks = jnp.sum(s0 * k_col, axis=0, keepdims=True)
            u = beta * v - (beta * alpha) * ks
            s1 = alpha * s0 + k_col * u
            s_ref[smp, h] = s1
            o = jnp.sum(s1 * q_col, axis=0, keepdims=True)
            outs.append(_gdn_out(o, nw_ref[...], gate[:, sl]))
        o_ref[smp] = jnp.concatenate(outs, axis=1)


def _gdn_step(x, gate, small, conv_buf, s0, prm):
    n = x.shape[0]
    w = GROUP_WIDTH
    ns = 1
    full = lambda shape: pl.BlockSpec(shape, lambda i: (0,) * len(shape))
    return pl.pallas_call(
        _gdn_step_kernel,
        grid=(n // ns,),
        in_specs=[pl.BlockSpec((ns, 1, 3 * w), lambda i: (i, 0, 0)),
                  pl.BlockSpec((ns, 1, w), lambda i: (i, 0, 0)),
                  pl.BlockSpec((ns, 1, 128), lambda i: (i, 0, 0)),
                  pl.BlockSpec((ns, CONV_W - 1, 3 * w), lambda i: (i, 0, 0)),
                  pl.BlockSpec((ns, H_GDN, GDN_DK, GDN_DK), lambda i: (i, 0, 0, 0)),
                  full((CONV_W, 3 * w)), full((1, 128)), full((1, 128)), full((1, GDN_DK))],
        out_specs=[pl.BlockSpec((ns, 1, w), lambda i: (i, 0, 0)),
                   pl.BlockSpec((ns, H_GDN, GDN_DK, GDN_DK), lambda i: (i, 0, 0, 0))],
        out_shape=[jax.ShapeDtypeStruct((n, 1, w), F32),
                   jax.ShapeDtypeStruct((n, H_GDN, GDN_DK, GDN_DK), F32)],
        compiler_params=_cparams(("parallel",)),
        name="gdn_step",
    )(x, gate, small, conv_buf, s0, prm['gdn_conv_w'], prm['gdn_alog_row'], prm['gdn_dt_row'],
      prm['gdn_norm_w'])


def _dot_row(row, mat, **kw):
    return _dot(jnp.broadcast_to(row, (8, row.shape[1])), mat, **kw)[0:1]


def _fox_step_kernel(pt_ref, *refs, n_pages):
    kv = refs[:n_pages]
    lf = refs[n_pages:2 * n_pages]
    x_ref, sm_ref, bf_ref, o_ref, lf_out_ref = refs[2 * n_pages:]
    w = GROUP_WIDTH
    pg = PAGE_SIZE
    x = x_ref[...]
    q, kn, vn = x[:, 0:w], x[:, w:2 * w], x[:, 2 * w:3 * w]
    logf_new = _log_sigmoid(sm_ref[...] + bf_ref[...])
    lf_out_ref[...] = logf_new
    head_cols = (_iota((w, N_HEADS), 0) // HEAD_DIM == _iota((w, N_HEADS), 1)).astype(BF16)
    head_rows = _iota((N_HEADS, w), 1) // HEAD_DIM == _iota((N_HEADS, w), 0)
    eye = _iota((w, w), 0) == _iota((w, w), 1)
    qblk = _dot(jnp.where(eye, q, 0.0).astype(BF16), head_cols).astype(BF16)
    tri = (_iota((pg, pg), 1) <= _iota((pg, pg), 0)).astype(F32)
    carry = jnp.zeros((1, N_HEADS), F32)
    cs = []
    for p in range(n_pages):
        c = _dot(tri, lf[p][...], precision=HI) + carry
        carry = c[pg - 1:pg, :]
        cs.append(c)
    ctot = carry + logf_new[:, 0:N_HEADS]
    s_n = _dot_row(kn.astype(BF16), qblk) * ATT_SCALE
    m = s_n
    ss = []
    stride = 2 * N_HEADS
    for p in range(n_pages):
        s = None
        for h in range(N_HEADS):
            k_h = kv[p][pl.ds(h, pg, stride=stride), :].astype(BF16)
            part = _dot(k_h, qblk[h * HEAD_DIM:(h + 1) * HEAD_DIM, :])
            s = part if s is None else s + part
        s = s * ATT_SCALE + (ctot - cs[p])
        ss.append(s)
        m = jnp.maximum(m, jnp.max(s, axis=0, keepdims=True))
    e_n = jnp.exp(s_n - m)
    l = e_n
    acc = [jnp.zeros((8, HEAD_DIM), F32) for _ in range(N_HEADS)]
    for p in range(n_pages):
        e = jnp.exp(ss[p] - m)
        l = l + jnp.sum(e, axis=0, keepdims=True)
        for h in range(N_HEADS):
            v_h = kv[p][pl.ds(N_HEADS + h, pg, stride=stride), :]
            acc[h] = acc[h] + jnp.sum((e[:, h:h + 1] * v_h).reshape(pg // 8, 8, HEAD_DIM), axis=0)
    tot = jnp.concatenate([jnp.sum(a, axis=0, keepdims=True) for a in acc], axis=1)
    tot = tot + _dot_row(e_n.astype(BF16), head_rows.astype(BF16)) * vn
    o_ref[...] = tot * _dot_row(1.0 / l, head_rows.astype(F32), precision=HI)


def _fox_step(pt, kv_pool, lf_pool, x, small, bf_row):
    n, n_pages = pt.shape
    w = GROUP_WIDTH
    page = lambda i, rows, width: pl.BlockSpec((None, rows, width), lambda b, t: (t[b, i], 0, 0))
    in_specs = ([page(i, PAGE_SIZE * 2 * N_HEADS, HEAD_DIM) for i in range(n_pages)] +
                [page(i, PAGE_SIZE, N_HEADS) for i in range(n_pages)] +
                [pl.BlockSpec((None, 1, 3 * w), lambda b, t: (b, 0, 0)),
                 pl.BlockSpec((None, 1, 128), lambda b, t: (b, 0, 0)),
                 pl.BlockSpec((1, 128), lambda b, t: (0, 0))])
    return pl.pallas_call(
        functools.partial(_fox_step_kernel, n_pages=n_pages),
        grid_spec=pltpu.PrefetchScalarGridSpec(
            num_scalar_prefetch=1, grid=(n,), in_specs=in_specs,
            out_specs=[pl.BlockSpec((None, 1, w), lambda b, t: (b, 0, 0)),
                       pl.BlockSpec((None, 1, 128), lambda b, t: (b, 0, 0))]),
        out_shape=[jax.ShapeDtypeStruct((n, 1, w), F32), jax.ShapeDtypeStruct((n, 1, 128), F32)],
        compiler_params=_cparams(("parallel",)),
        name="fox_step",
    )(pt, *([kv_pool] * n_pages), *([lf_pool] * n_pages), x, small, bf_row)


def _softmax_cols(scores):
    m = None
    for s in scores:
        mx = jnp.max(s, axis=0, keepdims=True)
        m = mx if m is None else jnp.maximum(m, mx)
    es = [jnp.exp(s - m) for s in scores]
    l = None
    for e in es:
        sm = jnp.sum(e, axis=0, keepdims=True)
        l = sm if l is None else l + sm
    return es, l


def _nsa_step_kernel(pt_ref, *refs, n_pages):
    pgs = refs[:n_pages]
    win_ref, q_ref, kv_ref, wn_ref, sm_ref, wc_ref, o_ref = refs[n_pages:]
    pg = PAGE_SIZE
    w = GROUP_WIDTH
    kvw = KVH_NSA * HEAD_DIM
    past = n_pages * pg
    nbp = past // NSA_BLOCK
    q = q_ref[...]
    r = _iota((kvw, w), 0)
    c = _iota((kvw, w), 1)
    fold = (c % HEAD_DIM == r % HEAD_DIM) & (c // (NSA_GROUP * HEAD_DIM) == r // HEAD_DIM)
    head_cols = (_iota((w, N_HEADS), 0) // HEAD_DIM == _iota((w, N_HEADS), 1)).astype(BF16)
    qn = _dot(jnp.where(fold, q, 0.0).astype(BF16), head_cols).astype(BF16)
    hl = _iota((1, N_HEADS), 1)
    slopes = jnp.exp2(-(hl + 1).astype(F32) * (8.0 / N_HEADS))
    spread = [_iota((N_HEADS, kvw), 0) == NSA_GROUP * (_iota((N_HEADS, kvw), 1) // HEAD_DIM) + i
              for i in range(NSA_GROUP)]
    spread_bf = [s.astype(BF16) for s in spread]
    spread_f = [s.astype(F32) for s in spread]

    def weighted_rows(es, vals):
        out = []
        for i in range(NSA_GROUP):
            tot = None
            for e, v in zip(es, vals):
                if e.shape[0] == 1:
                    part = _dot_row(e.astype(BF16), spread_bf[i]) * v
                else:
                    part = jnp.sum(_dot(e.astype(BF16), spread_bf[i]) * v, axis=0, keepdims=True)
                tot = part if tot is None else tot + part
            out.append(tot)
        return out

    wc = wc_ref[...]
    kc_rows, vc_rows = [], []
    for p in range(n_pages):
        rows = pgs[p][:, 0:2 * kvw]
        for half in range(pg // NSA_BLOCK):
            cm = jnp.sum(rows[half * NSA_BLOCK:(half + 1) * NSA_BLOCK] * wc, axis=0, keepdims=True)
            kc_rows.append(cm[:, 0:kvw])
            vc_rows.append(cm[:, kvw:2 * kvw])
    kcmp = jnp.concatenate(kc_rows, axis=0)
    vcmp = jnp.concatenate(vc_rows, axis=0)
    blk = _iota((nbp, 1), 0)
    dist_c = (past - ((blk + 1) * NSA_BLOCK - 1)).astype(F32)
    s_c = _dot(kcmp.astype(BF16), qn) * ATT_SCALE - slopes * dist_c
    (e_c,), l_c = _softmax_cols([s_c])
    p_c = e_c / l_c
    acc_c = weighted_rows([p_c], [vcmp])

    cur = past // NSA_BLOCK
    forced = (blk == 0) | (blk >= cur - 1)
    eye_b = _iota((nbp, nbp), 0) == _iota((nbp, nbp), 1)
    lower = _iota((nbp, nbp), 1) < _iota((nbp, nbp), 0)
    n_sel = min(NSA_TOPN, nbp + 1)
    sels = []
    for g in range(KVH_NSA):
        imp = p_c[:, NSA_GROUP * g:NSA_GROUP * g + 1]
        for i in range(1, NSA_GROUP):
            imp = imp + p_c[:, NSA_GROUP * g + i:NSA_GROUP * g + i + 1]
        score = jnp.where(forced, FORCED_SCORE, imp)
        score_row = _col_to_row(score, eye_b)
        beats = (score_row > score) | ((score_row == score) & lower)
        cnt = jnp.sum(jnp.where(beats, 1.0, 0.0), axis=1, keepdims=True)
        cnt = cnt + jnp.where(score < FORCED_SCORE, 1.0, 0.0)
        sels.append(jnp.where(cnt < n_sel, 1.0, 0.0))
    sel8 = jnp.where(hl < NSA_GROUP, sels[0], sels[1])

    kv_new = kv_ref[...]
    first_half = _iota((pg, 1), 0) < NSA_BLOCK
    scores, vals = [], []
    for p in range(n_pages):
        dist = (past - (p * pg + _iota((pg, 1), 0))).astype(F32)
        s = _dot(pgs[p][:, 2 * kvw:3 * kvw].astype(BF16), qn) * ATT_SCALE - slopes * dist
        chosen = jnp.where(first_half, sel8[2 * p:2 * p + 1, :], sel8[2 * p + 1:2 * p + 2, :])
        scores.append(jnp.where(chosen > 0.5, s, NEG))
        vals.append(pgs[p][:, 3 * kvw:4 * kvw])
    scores.append(_dot_row(kv_new[:, 2 * kvw:3 * kvw].astype(BF16), qn) * ATT_SCALE)
    vals.append(kv_new[:, 3 * kvw:4 * kvw])
    e_s, l_s = _softmax_cols(scores)
    acc_s = weighted_rows(e_s, vals)

    nw = win_ref.shape[0]
    wrow = _iota((nw, 1), 0)
    dist_w = nw - wrow
    s_w = _dot(win_ref[:, 0:kvw].astype(BF16), qn) * ATT_SCALE - slopes * dist_w.astype(F32)
    s_w = jnp.where((dist_w < NSA_WINDOW) & (past - dist_w >= 0), s_w, NEG)
    wn = wn_ref[...]
    s_wn = _dot_row(wn[:, 0:kvw].astype(BF16), qn) * ATT_SCALE
    e_w, l_w = _softmax_cols([s_w, s_wn])
    acc_w = weighted_rows(e_w, [win_ref[:, kvw:2 * kvw], wn[:, kvw:2 * kvw]])

    gates = jax.nn.sigmoid(sm_ref[...])
    lane = _iota((128, N_HEADS), 0)
    head = _iota((128, N_HEADS), 1)
    coef = []
    for k in range(3):
        pick = (lane == L_NG + 3 * head + k).astype(F32)
        coef.append(_dot_row(gates, pick, precision=HI))
    coef[1] = coef[1] / l_s
    coef[2] = coef[2] / l_w
    tots = []
    for i in range(NSA_GROUP):
        t = _dot_row(coef[0], spread_f[i], precision=HI) * acc_c[i]
        t = t + (_dot_row(coef[1], spread_f[i], precision=HI) * acc_s[i]
                 + _dot_row(coef[2], spread_f[i], precision=HI) * acc_w[i])
        tots.append(t)
    o_ref[...] = jnp.concatenate([t[:, 0:HEAD_DIM] for t in tots] + [t[:, HEAD_DIM:kvw] for t in tots], axis=1)


def _nsa_step(pt, kv_pool, win, q, kv_new, win_new, small, wcmp):
    n, n_pages = pt.shape
    w = GROUP_WIDTH
    nw = win.shape[1]
    row = lambda width: pl.BlockSpec((None, 1, width), lambda b, t: (b, 0, 0))
    in_specs = ([pl.BlockSpec((None, PAGE_SIZE, w), (lambda i: (lambda b, t: (t[b, i], 0, 0)))(i))
                 for i in range(n_pages)] +
                [pl.BlockSpec((None, nw, 256), lambda b, t: (b, 0, 0)),
                 row(w), row(w), row(256), row(128),
                 pl.BlockSpec((NSA_BLOCK, 256), lambda b, t: (0, 0))])
    return pl.pallas_call(
        functools.partial(_nsa_step_kernel, n_pages=n_pages),
        grid_spec=pltpu.PrefetchScalarGridSpec(
            num_scalar_prefetch=1, grid=(n,), in_specs=in_specs,
            out_specs=pl.BlockSpec((None, 1, w), lambda b, t: (b, 0, 0))),
        out_shape=jax.ShapeDtypeStruct((n, 1, w), F32),
        compiler_params=_cparams(("parallel",)),
        name="nsa_step",
    )(pt, *([kv_pool] * n_pages), win, q, kv_new, win_new, small, wcmp)


def _head_cols(row, eye):
    return [_row_to_col(row[:, h * HEAD_DIM:(h + 1) * HEAD_DIM], eye) for h in range(N_HEADS)]


def _rows_to_block(rows):
    sub = _iota((N_HEADS, rows[0].shape[1]), 0)
    out = jnp.broadcast_to(rows[0], sub.shape)
    for h in range(1, N_HEADS):
        out = jnp.where(sub == h, rows[h], out)
    return out


def _tile_scores(tile_of_head, q_cols):
    return _rows_to_block([jnp.sum(tile_of_head(h) * q_cols[h], axis=0, keepdims=True)
                           for h in range(N_HEADS)])


def _fox_step_kernel_t(pt_ref, *refs, n_pages):
    kv = refs[:n_pages]
    lf = refs[n_pages:2 * n_pages]
    x_ref, sm_ref, bf_ref, o_ref, lf_out_ref = refs[2 * n_pages:]
    w = GROUP_WIDTH
    pg = PAGE_SIZE
    hd = HEAD_DIM
    x = x_ref[...]
    q, kn, vn = x[:, 0:w], x[:, w:2 * w], x[:, 2 * w:3 * w]
    logf_new = _log_sigmoid(sm_ref[...] + bf_ref[...])
    lf_out_ref[...] = logf_new
    eye = _iota((hd, hd), 0) == _iota((hd, hd), 1)
    eye8 = _iota((N_HEADS, N_HEADS), 0) == _iota((N_HEADS, N_HEADS), 1)
    q_cols = _head_cols(q, eye)
    vn_cols = _head_cols(vn, eye)
    sub = _iota((N_HEADS, 1), 0)
    upper = (_iota((pg, pg), 0) <= _iota((pg, pg), 1)).astype(F32)
    carry = jnp.zeros((N_HEADS, 1), F32)
    cs = []
    for p in range(n_pages):
        c = _dot(lf[p][...], upper, precision=HI) + carry
        carry = c[:, pg - 1:pg]
        cs.append(c)
    ctot = carry + _row_to_col(logf_new[:, 0:N_HEADS], eye8)
    prod = kn * q
    s_n = jnp.zeros((N_HEADS, 1), F32)
    for h in range(N_HEADS):
        s_n = jnp.where(sub == h, jnp.sum(prod[:, h * hd:(h + 1) * hd], axis=1, keepdims=True), s_n)
    s_n = s_n * ATT_SCALE
    m = s_n
    ss = []
    for p in range(n_pages):
        s = _tile_scores(lambda h: kv[p][0, h], q_cols) * ATT_SCALE + (ctot - cs[p])
        ss.append(s)
        m = jnp.maximum(m, jnp.max(s, axis=1, keepdims=True))
    e_n = jnp.exp(s_n - m)
    l = e_n
    es = []
    for p in range(n_pages):
        e = jnp.exp(ss[p] - m)
        es.append(e)
        l = l + jnp.sum(e, axis=1, keepdims=True)
    inv_l = 1.0 / l
    outs = []
    for h in range(N_HEADS):
        acc = kv[0][1, h] * es[0][h:h + 1, :]
        for p in range(1, n_pages):
            acc = acc + kv[p][1, h] * es[p][h:h + 1, :]
        o_col = jnp.sum(acc, axis=1, keepdims=True) + vn_cols[h] * e_n[h:h + 1, :]
        outs.append(_col_to_row(o_col * inv_l[h:h + 1, :], eye))
    o_ref[...] = jnp.concatenate(outs, axis=1)


def _fox_step_t(pt, layer, kv_t, lf_t, x, small, bf_row):
    n, n_pages = pt.shape
    w = GROUP_WIDTH
    kv_spec = lambda i: pl.BlockSpec((None, None, 2, N_HEADS, HEAD_DIM, PAGE_SIZE),
                                     lambda b, t: (layer, t[b, i], 0, 0, 0, 0))
    lf_spec = lambda i: pl.BlockSpec((None, None, N_HEADS, PAGE_SIZE), lambda b, t: (layer, t[b, i], 0, 0))
    in_specs = ([kv_spec(i) for i in range(n_pages)] + [lf_spec(i) for i in range(n_pages)] +
                [pl.BlockSpec((None, 1, 3 * w), lambda b, t: (b, 0, 0)),
                 pl.BlockSpec((None, 1, 128), lambda b, t: (b, 0, 0)),
                 pl.BlockSpec((1, 128), lambda b, t: (0, 0))])
    return pl.pallas_call(
        functools.partial(_fox_step_kernel_t, n_pages=n_pages),
        grid_spec=pltpu.PrefetchScalarGridSpec(
            num_scalar_prefetch=1, grid=(n,), in_specs=in_specs,
            out_specs=[pl.BlockSpec((None, 1, w), lambda b, t: (b, 0, 0)),
                       pl.BlockSpec((None, 1, 128), lambda b, t: (b, 0, 0))]),
        out_shape=[jax.ShapeDtypeStruct((n, 1, w), F32), jax.ShapeDtypeStruct((n, 1, 128), F32)],
        compiler_params=_cparams(("parallel",)),
        name="fox_step",
    )(pt, *([kv_t] * n_pages), *([lf_t] * n_pages), x, small, bf_row)


def _nsa_step_kernel_t(pt_ref, *refs, n_pages):
    pgs = refs[:n_pages]
    win_ref, q_ref, kv_ref, wn_ref, sm_ref, wc_ref, o_ref = refs[n_pages:]
    pg = PAGE_SIZE
    hd = HEAD_DIM
    nh = N_HEADS
    past = n_pages * pg
    nbp = past // NSA_BLOCK
    bpp = pg // NSA_BLOCK
    eye = _iota((hd, hd), 0) == _iota((hd, hd), 1)
    q_cols = _head_cols(q_ref[...], eye)
    sub = _iota((nh, 1), 0)
    lane = _iota((1, pg), 1)
    slopes = jnp.exp2(-(sub + 1).astype(F32) * (8.0 / nh))
    grp = lambda h: h // NSA_GROUP
    kv_new = kv_ref[...]
    wn = wn_ref[...]
    wc = wc_ref[...]

    def new_scores(row):
        out = jnp.zeros((nh, 1), F32)
        for h in range(nh):
            k_col = _row_to_col(row[:, grp(h) * hd:(grp(h) + 1) * hd], eye)
            out = jnp.where(sub == h, jnp.sum(k_col * q_cols[h], axis=0, keepdims=True), out)
        return out * ATT_SCALE

    def softmax_lanes(scores, extra=None):
        m = extra
        for s in scores:
            mx = jnp.max(s, axis=1, keepdims=True)
            m = mx if m is None else jnp.maximum(m, mx)
        es = [jnp.exp(s - m) for s in scores]
        l = None if extra is None else jnp.exp(extra - m)
        e_extra = l
        for e in es:
            sm = jnp.sum(e, axis=1, keepdims=True)
            l = sm if l is None else l + sm
        return es, e_extra, l

    def weighted_values(tiles, es):
        cols = []
        for h in range(nh):
            acc = None
            for tile, e in zip(tiles, es):
                part = tile(grp(h)) * e[h:h + 1, :]
                acc = part if acc is None else acc + part
            cols.append(jnp.sum(acc, axis=1, keepdims=True))
        return cols

    half = [lane // NSA_BLOCK == i for i in range(bpp)]
    blk_lane = _iota((1, pg), 1)
    s_c = jnp.full((nh, pg), NEG, F32)
    for p in range(n_pages):
        raw = _tile_scores(lambda h: pgs[p][0, grp(h)], q_cols) * wc[0:1, :]
        for i in range(bpp):
            r = jnp.sum(jnp.where(half[i], raw, 0.0), axis=1, keepdims=True)
            s_c = jnp.where(blk_lane == bpp * p + i, r, s_c)
    dist_c = (past - ((blk_lane + 1) * NSA_BLOCK - 1)).astype(F32)
    s_c = jnp.where(blk_lane < nbp, s_c * ATT_SCALE - slopes * dist_c, NEG)
    (e_c,), _, l_c = softmax_lanes([s_c])
    p_c = e_c / l_c
    pes = []
    for p in range(n_pages):
        pe = jnp.zeros((nh, pg), F32)
        for i in range(bpp):
            pe = jnp.where(half[i], p_c[:, bpp * p + i:bpp * p + i + 1], pe)
        pes.append(pe * wc[1:2, :])
    o_c = weighted_values([(lambda g, p=p: pgs[p][1, g]) for p in range(n_pages)], pes)

    nbl = 32 * ((nbp + 31) // 32)
    eye_b = _iota((nbl, nbl), 0) == _iota((nbl, nbl), 1)
    m_lt_n = _iota((nbl, nbl), 0) < _iota((nbl, nbl), 1)
    blk_row = _iota((1, nbl), 1)
    n_sel = min(NSA_TOPN, nbp + 1)
    sel_rows = []
    for g in range(KVH_NSA):
        imp = jnp.sum(jnp.where(sub // NSA_GROUP == g, p_c, 0.0), axis=0, keepdims=True)[:, 0:nbl]
        forced = (blk_row == 0) | (blk_row >= nbp - 1)
        score = jnp.where(blk_row < nbp, jnp.where(forced, FORCED_SCORE, imp), -1.0)
        score_col = _row_to_col(score, eye_b)
        beats = jnp.where(score_col > score, 1.0, jnp.where((score_col == score) & m_lt_n, 1.0, 0.0))
        cnt = jnp.sum(beats, axis=0, keepdims=True)
        cnt = cnt + jnp.where(score < FORCED_SCORE, 1.0, 0.0)
        sel_rows.append(jnp.where((cnt < n_sel) & (score >= 0.0), 1.0, 0.0))

    scores = []
    for p in range(n_pages):
        dist = (past - (p * pg + lane)).astype(F32)
        s = _tile_scores(lambda h: pgs[p][2, grp(h)], q_cols) * ATT_SCALE - slopes * dist
        chosen = []
        for g in range(KVH_NSA):
            row = jnp.zeros((1, pg), F32)
            for i in range(bpp):
                row = jnp.where(half[i], sel_rows[g][:, bpp * p + i:bpp * p + i + 1], row)
            chosen.append(row)
        mask = jnp.where(sub // NSA_GROUP == 0, chosen[0], chosen[1])
        scores.append(jnp.where(mask > 0.5, s, NEG))
    s_new = new_scores(kv_new[:, 2 * 128:3 * 128])
    e_s, e_sn, l_s = softmax_lanes(scores, s_new)
    o_s = weighted_values([(lambda g, p=p: pgs[p][3, g]) for p in range(n_pages)], e_s)
    v_new = kv_new[:, 3 * 128:4 * 128]
    o_s = [o_s[h] + _row_to_col(v_new[:, grp(h) * hd:(grp(h) + 1) * hd], eye) * e_sn[h:h + 1, :]
           for h in range(nh)]

    nw = win_ref.shape[-1]
    scores = []
    for c0 in range(0, nw, pg):
        dist = nw - (c0 + lane)
        s = _tile_scores(lambda h: win_ref[0, grp(h), :, c0:c0 + pg], q_cols) * ATT_SCALE
        s = s - slopes * dist.astype(F32)
        scores.append(jnp.where((dist < NSA_WINDOW) & (past - dist >= 0), s, NEG))
    s_new = new_scores(wn[:, 0:128])
    e_w, e_wn, l_w = softmax_lanes(scores, s_new)
    o_w = weighted_values([(lambda g, c0=c0: win_ref[1, g, :, c0:c0 + pg]) for c0 in range(0, nw, pg)], e_w)
    vw_new = wn[:, 128:256]
    o_w = [o_w[h] + _row_to_col(vw_new[:, grp(h) * hd:(grp(h) + 1) * hd], eye) * e_wn[h:h + 1, :]
           for h in range(nh)]

    gates = jax.nn.sigmoid(sm_ref[...])
    outs = []
    for h in range(nh):
        c = L_NG + 3 * h
        o = gates[:, c:c + 1] * o_c[h]
        o = o + ((gates[:, c + 1:c + 2] / l_s[h:h + 1, :]) * o_s[h]
                 + (gates[:, c + 2:c + 3] / l_w[h:h + 1, :]) * o_w[h])
        outs.append(_col_to_row(o, eye))
    o_ref[...] = jnp.concatenate(outs, axis=1)


def _nsa_step_t(pt, layer, kv_t, win_t, q, kv_new, win_new, small, wc_rows):
    n, n_pages = pt.shape
    w = GROUP_WIDTH
    nw = win_t.shape[-1]
    row = lambda width: pl.BlockSpec((None, 1, width), lambda b, t: (b, 0, 0))
    page = lambda i: pl.BlockSpec((None, None, 4, KVH_NSA, HEAD_DIM, PAGE_SIZE),
                                  lambda b, t: (layer, t[b, i], 0, 0, 0, 0))
    in_specs = ([page(i) for i in range(n_pages)] +
                [pl.BlockSpec((None, None, 2, KVH_NSA, HEAD_DIM, nw), lambda b, t: (layer, b, 0, 0, 0, 0)),
                 row(w), row(w), row(256), row(128),
                 pl.BlockSpec((2, PAGE_SIZE), lambda b, t: (0, 0))])
    return pl.pallas_call(
        functools.partial(_nsa_step_kernel_t, n_pages=n_pages),
        grid_spec=pltpu.PrefetchScalarGridSpec(
            num_scalar_prefetch=1, grid=(n,), in_specs=in_specs,
            out_specs=pl.BlockSpec((None, 1, w), lambda b, t: (b, 0, 0))),
        out_shape=jax.ShapeDtypeStruct((n, 1, w), F32),
        compiler_params=_cparams(("parallel",)),
        name="nsa_step",
    )(pt, *([kv_t] * n_pages), win_t, q, kv_new, win_new, small, wc_rows)


_O_FF, _O_LX, _O_NQ, _O_NKC, _O_NG, _O_DQKV, _O_DG, _O_DA, _O_END = (
    1536, 1544, 2568, 3080, 3848, 3872, 5408, 5920, 5928)


def _pack_w_in(w):
    d = w.shape[0]
    zeros = lambda n: jnp.zeros((d, n), w.dtype)
    parts = [w[:, 0:_O_FF], w[:, _O_LX:_O_NQ], w[:, _O_NQ:_O_NKC], w[:, _O_NKC:_O_NG],
             w[:, _O_FF:_O_LX], w[:, _O_NG:_O_DQKV], w[:, _O_DA:_O_END], zeros(128 - 40), zeros(128),
             w[:, _O_DQKV:_O_DG], w[:, _O_DG:_O_DA]]
    out = jnp.concatenate(parts, axis=1)
    assert out.shape[1] == N_PROJ
    return out.astype(BF16)


def _block_diag(w):
    n, d, e = w.shape
    eye = jnp.eye(n, dtype=w.dtype)
    return (eye[:, None, :, None] * w[:, :, None, :]).reshape(n * d, n * e)


def _lane_row(vals, start):
    return jnp.zeros((1, 128), F32).at[0, start:start + vals.shape[0]].set(vals)


def _layer_params(l, p):
    row = lambda a: a[l][None, :]
    return dict(
        norm_pre_mix=row(p['norm_pre_mix']), norm_post_mix=row(p['norm_post_mix']),
        norm_pre_mlp=row(p['norm_pre_mlp']), norm_post_mlp=row(p['norm_post_mlp']),
        w_in=_pack_w_in(p['w_in'][l]), w_out=p['w_out'][l].astype(BF16),
        w_up=p['w_up'][l].astype(BF16), w_down=p['w_down'][l].astype(BF16),
        fox_bf_row=_lane_row(p['fox_b_f'][l], L_FF),
        lru_conv_w=p['lru_conv_w'][l], lru_conv_b=row(p['lru_conv_b']),
        lru_wr=_block_diag(p['lru_w_r'][l]).astype(BF16), lru_b_r=row(p['lru_b_r']),
        lru_wi=_block_diag(p['lru_w_i'][l]).astype(BF16), lru_b_i=row(p['lru_b_i']),
        lru_lambda=row(p['lru_lambda']),
        nsa_wcmp=jnp.repeat(p['nsa_w_cmp'][l].T, 128, axis=1),
        gdn_conv_w=p['gdn_conv_w'][l],
        gdn_alog_row=_lane_row(p['gdn_A_log'][l], L_DA), gdn_dt_row=_lane_row(p['gdn_dt_bias'][l], L_DA),
        gdn_norm_w=row(p['gdn_norm_w']))


def kernel(x_prompt, x_sample, cache_fox_kv, cache_fox_logf, cache_nsa_kv, cache_nsa_win,
           state_rglru_conv, state_rglru_h, state_gdn_conv, state_gdn_S, page_table,
           norm_pre_mix, norm_post_mix, norm_pre_mlp, norm_post_mlp, w_in, w_out, w_up, w_down,
           fox_b_f, lru_conv_w, lru_conv_b, lru_w_r, lru_b_r, lru_w_i, lru_b_i, lru_lambda,
           nsa_w_cmp, gdn_conv_w, gdn_A_log, gdn_dt_bias, gdn_norm_w):
    params = dict(norm_pre_mix=norm_pre_mix, norm_post_mix=norm_post_mix, norm_pre_mlp=norm_pre_mlp,
                  norm_post_mlp=norm_post_mlp, w_in=w_in, w_out=w_out, w_up=w_up, w_down=w_down,
                  fox_b_f=fox_b_f, lru_conv_w=lru_conv_w, lru_conv_b=lru_conv_b, lru_w_r=lru_w_r,
                  lru_b_r=lru_b_r, lru_w_i=lru_w_i, lru_b_i=lru_b_i, lru_lambda=lru_lambda,
                  nsa_w_cmp=nsa_w_cmp, gdn_conv_w=gdn_conv_w, gdn_A_log=gdn_A_log,
                  gdn_dt_bias=gdn_dt_bias, gdn_norm_w=gdn_norm_w)
    b, t, d = x_prompt.shape
    ns = x_sample.shape[0]
    depth, n_pool = cache_fox_kv.shape[:2]
    bt = b * t
    w = GROUP_WIDTH
    xp = x_prompt.reshape(bt, d)
    xs = x_sample.reshape(ns, d)
    fox_t = jnp.transpose(cache_fox_kv, (0, 1, 3, 4, 5, 2))
    logf_t = jnp.transpose(cache_fox_logf, (0, 1, 3, 2))
    nsa_t = jnp.transpose(cache_nsa_kv, (0, 1, 3, 4, 5, 2))
    win_t = jnp.transpose(cache_nsa_win, (0, 1, 3, 4, 5, 2))
    outs = [[] for _ in range(16)]
    for l in range(depth):
        prm = _layer_params(l, params)
        proj = _in_proj(xp, prm['norm_pre_mix'], prm['w_in'])
        ps = _in_proj(xs, prm['norm_pre_mix'], prm['w_in'])
        pp = proj.reshape(b, t, N_PROJ)
        row3 = lambda c0, c1: ps[:, None, c0:c1]

        qa, ka, va, logf_p = _fox_prep(proj, prm['fox_bf_row'], b, t)
        oa_p = _fox_flash(qa, ka, va, b, t)
        oa_s, logf_s = _fox_step_t(page_table, l, fox_t, logf_t, row3(C_FQ, C_FQ + 3 * w),
                                   row3(C_SMALL, C_SMALL + 128), prm['fox_bf_row'])

        ob_p, h_p = _lru_prompt(proj, prm, b, t)
        ob_s, h_s = _lru_step(ps[:, C_LX:C_LX + w], ps[:, C_LG:C_LG + w],
                              jnp.moveaxis(state_rglru_conv[l], 1, 0), state_rglru_h[l], prm)

        cmp = _nsa_cmp(proj, prm['nsa_wcmp'], b, t)
        oc_p = _nsa_prompt(proj, cmp, b, t)
        oc_s = _nsa_step_t(page_table, l, nsa_t, win_t, row3(C_NQ, C_NQ + w), row3(C_NKV, C_NKV + w),
                           row3(C_NWIN, C_NWIN + 256), row3(C_SMALL, C_SMALL + 128),
                           jnp.tile(params['nsa_w_cmp'][l], (1, PAGE_SIZE // NSA_BLOCK)))

        od_p, s_p = _gdn_prompt(proj, prm, b, t)
        od_s, s_s = _gdn_step(row3(C_DQKV, C_DQKV + 3 * w), row3(C_DG, C_DG + w),
                              row3(C_SMALL, C_SMALL + 128), state_gdn_conv[l], state_gdn_S[l], prm)

        xp = _out_proj(oa_p, ob_p, oc_p, od_p, xp, prm['w_out'], prm['norm_post_mix'])
        xp = _mlp(xp, prm['norm_pre_mlp'], prm['w_up'], prm['w_down'], prm['norm_post_mlp'])
        flat = lambda a: a.reshape(ns, w)
        xs = _out_proj(flat(oa_s), ob_s, flat(oc_s), flat(od_s), xs, prm['w_out'], prm['norm_post_mix'])
        xs = _mlp(xs, prm['norm_pre_mlp'], prm['w_up'], prm['w_down'], prm['norm_post_mlp'])

        new_win_s = ps[:, C_NWIN:C_NWIN + 256].reshape(ns, 1, 2, KVH_NSA, HEAD_DIM)
        win_all = jnp.concatenate([cache_nsa_win[l], new_win_s], axis=1)
        layer_out = [
            pp[:, :, C_FK:C_FK + 2 * w].reshape(b, t, 2, N_HEADS, HEAD_DIM),
            ps[:, C_FK:C_FK + 2 * w].reshape(ns, 1, 2, N_HEADS, HEAD_DIM),
            logf_p[:, :N_HEADS].reshape(b, t, N_HEADS),
            logf_s[:, :, :N_HEADS],
            pp[:, :, C_NKV:C_NKV + w].reshape(b, t, 4, KVH_NSA, HEAD_DIM),
            ps[:, C_NKV:C_NKV + w].reshape(ns, 1, 4, KVH_NSA, HEAD_DIM),
            pp[:, t - min(NSA_WINDOW, t):, C_NWIN:C_NWIN + 256].reshape(b, min(NSA_WINDOW, t), 2, KVH_NSA, HEAD_DIM),
            win_all[:, -NSA_WINDOW:],
            pp[:, t - (CONV_W - 1):, C_LX:C_LX + w],
            jnp.concatenate([state_rglru_conv[l][:, 1:], ps[:, None, C_LX:C_LX + w]], axis=1),
            h_p[:, 0],
            h_s,
            pp[:, t - (CONV_W - 1):, C_DQKV:C_DQKV + 3 * w],
            jnp.concatenate([state_gdn_conv[l][:, 1:], ps[:, None, C_DQKV:C_DQKV + 3 * w]], axis=1),
            s_p,
            s_s,
        ]
        for i, a in enumerate(layer_out):
            outs[i].append(a)
    y_prompt = xp.reshape(b, t, d)
    y_sample = xs.reshape(ns, 1, d)
    return (y_prompt, y_sample) + tuple(jnp.stack(o) for o in outs)
```

```python
import functools

import jax
import jax.numpy as jnp
from jax import lax
from jax.experimental import pallas as pl
from jax.experimental.pallas import tpu as pltpu

F32 = jnp.float32
BF16 = jnp.bfloat16
HI = lax.Precision.HIGHEST

D_MODEL = 2048
GROUP_WIDTH = 512
HEAD_DIM = 64
N_HEADS = 8
KVH_NSA = 2
NSA_GROUP = 4
NSA_BLOCK = 64
NSA_TOPN = 16
NSA_WINDOW = 512
FORCED_SCORE = 1e4
H_GDN = 4
GDN_DK = 128
GDN_CHUNK = 64
LRU_C = 8.0
CONV_W = 4
D_FF = 4 * D_MODEL
RMS_EPS = 1e-6
NEG = -1e30
ATT_SCALE = HEAD_DIM ** -0.5
PAGE_SIZE = 128

C_FQ, C_FK, C_FV = 0, 512, 1024
C_LX, C_LG = 1536, 2048
C_NQ = 2560
C_NKV = 3072
C_NWIN = 3584
C_SMALL = 3840
C_DQKV = 4096
C_DG = 5632
N_PROJ = 6144
L_FF, L_NG, L_DA, L_DB = 0, 8, 32, 36

VMEM_LIMIT = 56 * 1024 * 1024


def _cparams(sem):
    return pltpu.CompilerParams(dimension_semantics=sem, vmem_limit_bytes=VMEM_LIMIT)


def _pick(n, prefs):
    for p in prefs:
        if n % p == 0:
            return p
    return n


def _rms(xf, g):
    return xf * lax.rsqrt(jnp.mean(xf * xf, axis=-1, keepdims=True) + RMS_EPS) * g


def _softplus(x):
    return jnp.maximum(x, 0.0) + jnp.log1p(jnp.exp(-jnp.abs(x)))


def _log_sigmoid(x):
    return -_softplus(-x)


def _iota(shape, dim):
    return lax.broadcasted_iota(jnp.int32, shape, dim)


def _dot(a, b, **kw):
    return jnp.dot(a, b, preferred_element_type=F32, **kw)


def _dot3(a, b):
    a_hi = a.astype(BF16)
    b_hi = b.astype(BF16)
    a_lo = (a - a_hi.astype(F32)).astype(BF16)
    b_lo = (b - b_hi.astype(F32)).astype(BF16)
    return _dot(a_hi, b_hi) + (_dot(a_hi, b_lo) + _dot(a_lo, b_hi))


def _dot_nt(a, b):
    return lax.dot_general(a, b, (((1,), (1,)), ((), ())), preferred_element_type=F32)


def _dot_tn(a, b):
    return lax.dot_general(a, b, (((0,), (0,)), ((), ())), preferred_element_type=F32)


def _in_proj_kernel(x_ref, g_ref, w_ref, o_ref, h_ref):
    @pl.when(pl.program_id(1) == 0)
    def _():
        h_ref[...] = _rms(x_ref[...], g_ref[...]).astype(BF16)

    o_ref[...] = _dot(h_ref[...], w_ref[...])


def _in_proj(x, g, w):
    m, d = x.shape
    n = w.shape[1]
    tm = _pick(m, (1024, 512, 256, 128, 64, 32, 16, 8))
    tn = _pick(n, (1536, 1024, 768, 512, 256, 128))
    return pl.pallas_call(
        _in_proj_kernel,
        grid=(m // tm, n // tn),
        in_specs=[pl.BlockSpec((tm, d), lambda i, j: (i, 0)),
                  pl.BlockSpec((1, d), lambda i, j: (0, 0)),
                  pl.BlockSpec((d, tn), lambda i, j: (0, j))],
        out_specs=pl.BlockSpec((tm, tn), lambda i, j: (i, j)),
        out_shape=jax.ShapeDtypeStruct((m, n), F32),
        scratch_shapes=[pltpu.VMEM((tm, d), BF16)],
        compiler_params=_cparams(("parallel", "arbitrary")),
        name="in_proj",
    )(x, g, w)


def _out_proj_kernel(oa_ref, ob_ref, oc_ref, od_ref, x_ref, w_ref, g_ref, o_ref):
    y = _dot(oa_ref[...].astype(BF16), w_ref[0:512, :])
    y = y + _dot(ob_ref[...].astype(BF16), w_ref[512:1024, :])
    y = y + _dot(oc_ref[...].astype(BF16), w_ref[1024:1536, :])
    y = y + _dot(od_ref[...].astype(BF16), w_ref[1536:2048, :])
    o_ref[...] = x_ref[...] + _rms(y, g_ref[...])


def _out_proj(oa, ob, oc, od, x, w, g):
    m, d = x.shape
    tm = _pick(m, (640, 512, 256, 128, 64, 32, 16, 8))
    gw = GROUP_WIDTH
    return pl.pallas_call(
        _out_proj_kernel,
        grid=(m // tm,),
        in_specs=[pl.BlockSpec((tm, gw), lambda i: (i, 0))] * 4 + [
            pl.BlockSpec((tm, d), lambda i: (i, 0)),
            pl.BlockSpec((d, d), lambda i: (0, 0)),
            pl.BlockSpec((1, d), lambda i: (0, 0))],
        out_specs=pl.BlockSpec((tm, d), lambda i: (i, 0)),
        out_shape=jax.ShapeDtypeStruct((m, d), F32),
        compiler_params=_cparams(("parallel",)),
        name="out_proj",
    )(oa, ob, oc, od, x, w, g)


def _mlp_kernel(x_ref, g1_ref, wu_ref, wd_ref, g2_ref, o_ref, h_ref, acc_ref):
    f = pl.program_id(1)

    @pl.when(f == 0)
    def _():
        h_ref[...] = _rms(x_ref[...], g1_ref[...]).astype(BF16)
        acc_ref[...] = jnp.zeros_like(acc_ref)

    u = jnp.maximum(_dot(h_ref[...], wu_ref[...]), 0.0)
    acc_ref[...] += _dot((u * u).astype(BF16), wd_ref[...])

    @pl.when(f == pl.num_programs(1) - 1)
    def _():
        o_ref[...] = x_ref[...] + _rms(acc_ref[...], g2_ref[...])


def _mlp(x, g1, wu, wd, g2):
    m, d = x.shape
    ff = wu.shape[1]
    tm = _pick(m, (640, 512, 256, 128, 64, 32, 16, 8))
    tf = _pick(ff, (512, 256, 128))
    return pl.pallas_call(
        _mlp_kernel,
        grid=(m // tm, ff // tf),
        in_specs=[pl.BlockSpec((tm, d), lambda i, f: (i, 0)),
                  pl.BlockSpec((1, d), lambda i, f: (0, 0)),
                  pl.BlockSpec((d, tf), lambda i, f: (0, f)),
                  pl.BlockSpec((tf, d), lambda i, f: (f, 0)),
                  pl.BlockSpec((1, d), lambda i, f: (0, 0))],
        out_specs=pl.BlockSpec((tm, d), lambda i, f: (i, 0)),
        out_shape=jax.ShapeDtypeStruct((m, d), F32),
        scratch_shapes=[pltpu.VMEM((tm, d), BF16), pltpu.VMEM((tm, d), F32)],
        compiler_params=_cparams(("parallel", "arbitrary")),
        name="mlp",
    )(x, g1, wu, wd, g2)


def _fox_prep_kernel(x_ref, s_ref, bf_ref, qa_ref, ka_ref, v_ref, logf_ref, carry_ref, *, tb):
    @pl.when(pl.program_id(1) == 0)
    def _():
        carry_ref[...] = jnp.zeros_like(carry_ref)

    logf = _log_sigmoid(s_ref[...] + bf_ref[...])
    logf_ref[...] = logf
    tri = (_iota((tb, tb), 1) <= _iota((tb, tb), 0)).astype(F32)
    c = _dot(tri, logf, precision=HI) + carry_ref[...]
    carry_ref[...] = c[tb - 1:tb, :]
    hi = c.astype(BF16).astype(F32)
    r1 = c - hi
    mid = r1.astype(BF16).astype(F32)
    lo = r1 - mid
    lane = _iota((tb, HEAD_DIM), 1)
    q_tail = jnp.where(lane < 3, 1.0, 0.0).astype(BF16)
    x = x_ref[...]
    w = GROUP_WIDTH
    for h in range(N_HEADS):
        sl = slice(h * HEAD_DIM, (h + 1) * HEAD_DIM)
        qa_ref[h] = jnp.concatenate([(x[:, sl] * ATT_SCALE).astype(BF16), q_tail], axis=1)
        k_tail = jnp.where(lane == 0, -hi[:, h:h + 1],
                           jnp.where(lane == 1, -mid[:, h:h + 1],
                                     jnp.where(lane == 2, -lo[:, h:h + 1], 0.0)))
        ka_ref[h] = jnp.concatenate([x[:, w + h * HEAD_DIM:w + (h + 1) * HEAD_DIM].astype(BF16),
                                     k_tail.astype(BF16)], axis=1)
    for hp in range(N_HEADS // 2):
        v_ref[hp] = x[:, 2 * w + hp * 128:2 * w + (hp + 1) * 128].astype(BF16)


def _fox_prep(proj, bf_row, b, t):
    tb = _pick(t, (512, 256, 128, 64))
    nt = t // tb
    w = GROUP_WIDTH
    aug = jax.ShapeDtypeStruct((b, N_HEADS, t, 128), BF16)
    return pl.pallas_call(
        functools.partial(_fox_prep_kernel, tb=tb),
        grid=(b, nt),
        in_specs=[pl.BlockSpec((tb, 3 * w), lambda i, j: (i * nt + j, C_FQ // (3 * w))),
                  pl.BlockSpec((tb, 128), lambda i, j: (i * nt + j, C_SMALL // 128)),
                  pl.BlockSpec((1, 128), lambda i, j: (0, 0))],
        out_specs=[pl.BlockSpec((None, N_HEADS, tb, 128), lambda i, j: (i, 0, j, 0)),
                   pl.BlockSpec((None, N_HEADS, tb, 128), lambda i, j: (i, 0, j, 0)),
                   pl.BlockSpec((None, N_HEADS // 2, tb, 128), lambda i, j: (i, 0, j, 0)),
                   pl.BlockSpec((tb, 128), lambda i, j: (i * nt + j, 0))],
        out_shape=[aug, aug, jax.ShapeDtypeStruct((b, N_HEADS // 2, t, 128), BF16),
                   jax.ShapeDtypeStruct((b * t, 128), F32)],
        scratch_shapes=[pltpu.VMEM((1, 128), F32)],
        compiler_params=_cparams(("parallel", "arbitrary")),
        name="fox_prep",
    )(proj, proj, bf_row)


def _fox_flash_kernel(qi_ref, ki_ref, q_ref, k_ref, v_ref, o_ref, m_ref, l_ref, acc_ref, *, nt):
    step_id = pl.program_id(2)
    qi = qi_ref[step_id]
    ki = ki_ref[step_id]

    @pl.when(ki == 0)
    def _():
        m_ref[...] = jnp.full_like(m_ref, NEG)
        l_ref[...] = jnp.zeros_like(l_ref)
        acc_ref[...] = jnp.zeros_like(acc_ref)

    def step(diag):
        v2 = v_ref[...]
        for j in range(2):
            s = _dot_nt(k_ref[j], q_ref[j])
            if diag:
                s = jnp.where(_iota((nt, nt), 0) <= _iota((nt, nt), 1), s, NEG)
            m_prev = m_ref[j]
            m_new = jnp.maximum(m_prev, jnp.max(s, axis=0, keepdims=True))
            alpha = jnp.exp(m_prev - m_new)
            p = jnp.exp(s - m_new)
            l_ref[j] = alpha * l_ref[j] + jnp.sum(p, axis=0, keepdims=True)
            pv = _dot_tn(v2, p.astype(BF16))[j * HEAD_DIM:(j + 1) * HEAD_DIM, :]
            acc_ref[j] = alpha * acc_ref[j] + pv
            m_ref[j] = m_new

    @pl.when(ki < qi)
    def _():
        step(False)

    @pl.when(ki == qi)
    def _():
        step(True)
        o_ref[...] = jnp.concatenate([acc_ref[j] / l_ref[j] for j in range(2)], axis=0).T


def _causal_pairs(n):
    qs = [q for q in range(n) for _ in range(q + 1)]
    ks = [k for q in range(n) for k in range(q + 1)]
    return jnp.asarray(qs, jnp.int32), jnp.asarray(ks, jnp.int32)


def _fox_flash(qa, ka, v, b, t):
    nt = _pick(t, (1024, 512, 256, 128))
    n = t // nt
    qi, ki = _causal_pairs(n)
    hp = N_HEADS // 2
    return pl.pallas_call(
        functools.partial(_fox_flash_kernel, nt=nt),
        grid_spec=pltpu.PrefetchScalarGridSpec(
            num_scalar_prefetch=2, grid=(b, hp, qi.shape[0]),
            in_specs=[pl.BlockSpec((None, 2, nt, 128), lambda i, h, s, qr, kr: (i, h, qr[s], 0)),
                      pl.BlockSpec((None, 2, nt, 128), lambda i, h, s, qr, kr: (i, h, kr[s], 0)),
                      pl.BlockSpec((None, None, nt, 128), lambda i, h, s, qr, kr: (i, h, kr[s], 0))],
            out_specs=pl.BlockSpec((nt, 128), lambda i, h, s, qr, kr: (i * n + qr[s], h)),
            scratch_shapes=[pltpu.VMEM((2, 1, nt), F32), pltpu.VMEM((2, 1, nt), F32),
                            pltpu.VMEM((2, HEAD_DIM, nt), F32)]),
        out_shape=jax.ShapeDtypeStruct((b * t, GROUP_WIDTH), F32),
        compiler_params=_cparams(("parallel", "parallel", "arbitrary")),
        name="fox_flash",
    )(qi, ki, qa, ka, v)


def _lru_gates(xc, wr, br, wi, bi, lam):
    xb = xc.astype(BF16)
    r = jax.nn.sigmoid(_dot(xb, wr) + br)
    ig = jax.nn.sigmoid(_dot(xb, wi) + bi)
    log_a = -LRU_C * r * _softplus(-lam)
    a = jnp.exp(log_a)
    th = jnp.tanh(log_a)
    u = jnp.sqrt(-2.0 * th / (1.0 - th)) * (ig * xc)
    return a, u


def _conv4(w_ref, x0, x1, x2, x3, cols=slice(None)):
    y = 0.0 + w_ref[0:1, cols] * x0
    y = y + w_ref[1:2, cols] * x1
    y = y + w_ref[2:3, cols] * x2
    return y + w_ref[3:4, cols] * x3


def _lru_prompt_kernel(x_ref, g_ref, cw_ref, cb_ref, wr_ref, br_ref, wi_ref, bi_ref, lam_ref,
                       y_ref, hfin_ref, xbuf, a_s, u_s, hs, h_s, *, tb):
    @pl.when(pl.program_id(1) == 0)
    def _():
        xbuf[0:8, :] = jnp.zeros((8, GROUP_WIDTH), F32)
        h_s[...] = jnp.zeros_like(h_s)

    xbuf[8:8 + tb, :] = x_ref[...]
    xc = _conv4(cw_ref, xbuf[5:5 + tb, :], xbuf[6:6 + tb, :], xbuf[7:7 + tb, :], xbuf[8:8 + tb, :])
    xc = xc + cb_ref[...]
    xbuf[0:8, :] = xbuf[tb:tb + 8, :]
    a, u = _lru_gates(xc, wr_ref[...], br_ref[...], wi_ref[...], bi_ref[...], lam_ref[...])
    a_s[...] = a
    u_s[...] = u

    def body(t, h):
        h = a_s[pl.ds(t, 1), :] * h + u_s[pl.ds(t, 1), :]
        hs[pl.ds(t, 1), :] = h
        return h

    h = lax.fori_loop(0, tb, body, h_s[...], unroll=8)
    h_s[...] = h
    y_ref[...] = hs[...] * jax.nn.gelu(g_ref[...])
    hfin_ref[...] = jnp.broadcast_to(h, (8, GROUP_WIDTH))


def _lru_prompt(proj, prm, b, t):
    tb = _pick(t, (512, 256, 128, 64))
    nt = t // tb
    w = GROUP_WIDTH
    full = lambda shape: pl.BlockSpec(shape, lambda i, j: (0,) * len(shape))
    return pl.pallas_call(
        functools.partial(_lru_prompt_kernel, tb=tb),
        grid=(b, nt),
        in_specs=[pl.BlockSpec((tb, w), lambda i, j: (i * nt + j, C_LX // w)),
                  pl.BlockSpec((tb, w), lambda i, j: (i * nt + j, C_LG // w)),
                  full((CONV_W, w)), full((1, w)), full((w, w)), full((1, w)), full((w, w)),
                  full((1, w)), full((1, w))],
        out_specs=[pl.BlockSpec((tb, w), lambda i, j: (i * nt + j, 0)),
                   pl.BlockSpec((None, 8, w), lambda i, j: (i, 0, 0))],
        out_shape=[jax.ShapeDtypeStruct((b * t, w), F32), jax.ShapeDtypeStruct((b, 8, w), F32)],
        scratch_shapes=[pltpu.VMEM((tb + 8, w), F32), pltpu.VMEM((tb, w), F32), pltpu.VMEM((tb, w), F32),
                        pltpu.VMEM((tb, w), F32), pltpu.VMEM((1, w), F32)],
        compiler_params=_cparams(("parallel", "arbitrary")),
        name="lru_prompt",
    )(proj, proj, prm['lru_conv_w'], prm['lru_conv_b'], prm['lru_wr'], prm['lru_b_r'], prm['lru_wi'],
      prm['lru_b_i'], prm['lru_lambda'])


def _lru_step_kernel(x_ref, g_ref, buf_ref, h0_ref, cw_ref, cb_ref, wr_ref, br_ref, wi_ref, bi_ref,
                     lam_ref, y_ref, h_ref):
    xc = _conv4(cw_ref, buf_ref[0], buf_ref[1], buf_ref[2], x_ref[...]) + cb_ref[...]
    a, u = _lru_gates(xc, wr_ref[...], br_ref[...], wi_ref[...], bi_ref[...], lam_ref[...])
    h = a * h0_ref[...] + u
    h_ref[...] = h
    y_ref[...] = h * jax.nn.gelu(g_ref[...])


def _lru_step(x, g, buf, h0, prm):
    n, w = x.shape
    out = jax.ShapeDtypeStruct((n, w), F32)
    return pl.pallas_call(
        _lru_step_kernel, out_shape=[out, out], name="lru_step",
        compiler_params=pltpu.CompilerParams(vmem_limit_bytes=VMEM_LIMIT),
    )(x, g, buf, h0, prm['lru_conv_w'], prm['lru_conv_b'], prm['lru_wr'], prm['lru_b_r'], prm['lru_wi'],
      prm['lru_b_i'], prm['lru_lambda'])


def _nsa_cmp_kernel(kv_ref, w_ref, o_ref, *, tb):
    x = kv_ref[...].reshape(tb // NSA_BLOCK, NSA_BLOCK, 256)
    o_ref[...] = jnp.sum(x * w_ref[...][None], axis=1)


def _nsa_cmp(proj, wcmp, b, t):
    tb = _pick(t, (512,))
    nt = t // tb
    nb = tb // NSA_BLOCK
    return pl.pallas_call(
        functools.partial(_nsa_cmp_kernel, tb=tb),
        grid=(b, nt),
        in_specs=[pl.BlockSpec((tb, 256), lambda i, j: (i * nt + j, C_NKV // 256)),
                  pl.BlockSpec((NSA_BLOCK, 256), lambda i, j: (0, 0))],
        out_specs=pl.BlockSpec((None, nb, 256), lambda i, j: (i, j, 0)),
        out_shape=jax.ShapeDtypeStruct((b, t // NSA_BLOCK, 256), F32),
        compiler_params=_cparams(("parallel", "parallel")),
        name="nsa_cmp",
    )(proj, wcmp)


def _slope(h):
    return float(2.0 ** (-8.0 * (h + 1) / N_HEADS))


def _nsa_prompt_kernel(qi_ref, ki_ref, q_ref, ks_ref, vs_ref, kw_ref, vw_ref, cmp_ref, sm_ref, o_ref,
                       qs, oc, sel, m_s, l_s, acc_s, m_w, l_w, acc_w, tile_hit, *, nq, nk, nb):
    step_id = pl.program_id(1)
    qi = qi_ref[step_id]
    ki = ki_ref[step_id]
    q0 = qi * nq
    k0 = ki * nk
    k_last = (q0 + nq - 1) // nk
    n_sel = min(NSA_TOPN, nb)
    rows4 = NSA_GROUP * nq
    hd = HEAD_DIM
    head_in_group = _iota((rows4, 1), 0) // nq

    def slope_col(g):
        return jnp.exp2(-(NSA_GROUP * g + head_in_group + 1).astype(F32) * (8.0 / N_HEADS))

    @pl.when(ki == 0)
    def _():
        q = q_ref[...] * ATT_SCALE
        cmp = cmp_ref[...]
        tpos = q0 + (_iota((rows4, nb), 0) % nq)
        blk = _iota((rows4, nb), 1)
        dist_c = tpos - ((blk + 1) * NSA_BLOCK - 1)
        valid = dist_c >= 0
        dist_cf = dist_c.astype(F32)
        lane = _iota((rows4, hd), 1)
        scores = []
        for g in range(KVH_NSA):
            qg = jnp.concatenate([q[:, (NSA_GROUP * g + i) * hd:(NSA_GROUP * g + i + 1) * hd]
                                  for i in range(NSA_GROUP)], axis=0)
            tail = jnp.where(lane < 2, slope_col(g), 0.0)
            qs[g] = jnp.concatenate([qg, tail] if g == 0 else [tail, qg], axis=1).astype(BF16)
            kc = cmp[:, g * hd:(g + 1) * hd].astype(BF16)
            vc = cmp[:, 128 + g * hd:128 + (g + 1) * hd].astype(BF16)
            s = _dot_nt(qg.astype(BF16), kc) - slope_col(g) * dist_cf
            s = jnp.where(valid, s, NEG)
            m = jnp.max(s, axis=-1, keepdims=True)
            e = jnp.where(valid, jnp.exp(s - m), 0.0)
            l = jnp.sum(e, axis=-1, keepdims=True)
            p = e / jnp.where(l > 0.0, l, 1.0)
            oc[g] = _dot(p.astype(BF16), vc)
            imp = p[0:nq]
            for i in range(1, NSA_GROUP):
                imp = imp + p[i * nq:(i + 1) * nq]
            scores.append(imp)
        imp2 = jnp.concatenate(scores, axis=1)
        lane2 = _iota((nq, KVH_NSA * nb), 1)
        blk2 = lane2 % nb
        cur = (q0 + _iota((nq, KVH_NSA * nb), 0)) // NSA_BLOCK
        forced = (blk2 == 0) | (blk2 >= cur - 1)
        score = jnp.where(blk2 <= cur, jnp.where(forced, FORCED_SCORE, imp2), -1.0)
        score_t = score.T
        rg = 8 if nb % 8 == 0 else nb
        for g in range(KVH_NSA):
            s_g = score_t[g * nb:(g + 1) * nb, :]
            for j0 in range(0, nb, rg):
                sub = s_g[j0:j0 + rg, :]
                n_idx = j0 + _iota((rg, nq), 0)
                cnt = jnp.zeros((rg, nq), F32)
                for mm in range(nb):
                    r = s_g[mm:mm + 1, :]
                    ge = jnp.where(r >= sub, 1.0, 0.0)
                    gt = jnp.where(r > sub, 1.0, 0.0)
                    if j0 > mm:
                        cnt = cnt + ge
                    elif j0 + rg - 1 <= mm:
                        cnt = cnt + gt
                    else:
                        cnt = cnt + jnp.where(n_idx > mm, ge, gt)
                sel[g * nb + j0:g * nb + j0 + rg, :] = jnp.where(
                    cnt < n_sel, jnp.where(sub >= 0.0, 1.0, 0.0), 0.0)
        bpt = nk // NSA_BLOCK
        for kt in range(nb // bpt):
            hit = jnp.maximum(sel[kt * bpt:(kt + 1) * bpt, :], sel[nb + kt * bpt:nb + (kt + 1) * bpt, :])
            tile_hit[kt] = (jnp.max(hit) > 0.5).astype(jnp.int32)
        for ref in (m_s, m_w):
            ref[...] = jnp.full_like(ref, NEG)
        for ref in (l_s, acc_s, l_w, acc_w):
            ref[...] = jnp.zeros_like(ref)

    def aug_keys(k_ref):
        k = k_ref[...]
        c = _iota((nk, 2 * hd), 0)
        lane = _iota((nk, 2 * hd), 1)
        c_hi = ((c // 256) * 256).astype(F32)
        c_lo = (c % 256).astype(F32)
        k_g0 = jnp.where(lane < hd, k, jnp.where(lane == hd, c_hi, jnp.where(lane == hd + 1, c_lo, 0.0)))
        k_g1 = jnp.where(lane >= hd, k, jnp.where(lane == 0, c_hi, jnp.where(lane == 1, c_lo, 0.0)))
        return k_g0.astype(BF16), k_g1.astype(BF16)

    def attend(g, k_aug, v_bf, bias, m_ref, l_ref, acc_ref):
        head = NSA_GROUP * g + _iota((1, rows4), 1) // nq
        shift = jnp.exp2(-(head + 1).astype(F32) * (8.0 / N_HEADS)) * (k0 - q0).astype(F32)
        s = _dot_nt(k_aug, qs[g])
        s = jnp.concatenate([s[:, i * nq:(i + 1) * nq] + bias for i in range(NSA_GROUP)], axis=1)
        m_prev = m_ref[g]
        m_new = jnp.maximum(m_prev, jnp.max(s, axis=0, keepdims=True) + shift)
        alpha = jnp.exp(m_prev - m_new)
        p = jnp.exp(s - (m_new - shift))
        l_ref[g] = alpha * l_ref[g] + jnp.sum(p, axis=0, keepdims=True)
        acc_ref[g] = alpha * acc_ref[g] + _dot_tn(v_bf, p.astype(BF16))
        m_ref[g] = m_new

    def dist_tile():
        return (q0 + _iota((nk, nq), 1)) - (k0 + _iota((nk, nq), 0))

    @pl.when((ki <= k_last) & (tile_hit[ki] > 0))
    def _():
        causal = dist_tile() >= 0
        k_aug = aug_keys(ks_ref)
        v_bf = vs_ref[...].astype(BF16)
        sel_bf = sel[...].astype(BF16)
        key_blk = (k0 + _iota((nk, KVH_NSA * nb), 0)) // NSA_BLOCK
        col = _iota((nk, KVH_NSA * nb), 1)
        for g in range(KVH_NSA):
            onehot = (col == g * nb + key_blk).astype(BF16)
            picked = _dot(onehot, sel_bf) > 0.5
            bias = jnp.where(causal, jnp.where(picked, 0.0, NEG), NEG)
            attend(g, k_aug[g], v_bf, bias, m_s, l_s, acc_s)

    @pl.when((ki <= k_last) & (ki >= k_last - (NSA_WINDOW + nk - 1) // nk))
    def _():
        dist = dist_tile()
        bias = jnp.where(dist >= 0, jnp.where(dist < NSA_WINDOW, 0.0, NEG), NEG)
        k_aug = aug_keys(kw_ref)
        v_bf = vw_ref[...].astype(BF16)
        for g in range(KVH_NSA):
            attend(g, k_aug[g], v_bf, bias, m_w, l_w, acc_w)

    @pl.when(ki == k_last)
    def _():
        gates = jax.nn.sigmoid(sm_ref[...])
        outs = []
        for g in range(KVH_NSA):
            o_s = (acc_s[g] / l_s[g]).T[:, g * hd:(g + 1) * hd]
            o_w = (acc_w[g] / l_w[g]).T[:, g * hd:(g + 1) * hd]
            o_c = oc[g]
            for i in range(NSA_GROUP):
                rows = slice(i * nq, (i + 1) * nq)
                c = L_NG + 3 * (NSA_GROUP * g + i)
                o = gates[:, c:c + 1] * o_c[rows]
                outs.append(o + (gates[:, c + 1:c + 2] * o_s[rows] + gates[:, c + 2:c + 3] * o_w[rows]))
        o_ref[...] = jnp.concatenate(outs, axis=1)


def _nsa_prompt(proj, cmp, b, t):
    nq = _pick(t, (256, 128))
    nk = _pick(t, (512, 256, 128))
    n_q = t // nq
    n_k = t // nk
    nb = t // NSA_BLOCK
    last = lambda q: (q * nq + nq - 1) // nk
    win_tiles = (NSA_WINDOW + nk - 1) // nk
    pairs = [(q, k) for q in range(n_q) for k in range(last(q) + 1)]
    qi = jnp.asarray([p[0] for p in pairs], jnp.int32)
    ki = jnp.asarray([p[1] for p in pairs], jnp.int32)
    rows4 = NSA_GROUP * nq

    def widx(q, k):
        kl = (q * nq + nq - 1) // nk
        return jnp.clip(k, jnp.maximum(kl - win_tiles, 0), kl)

    return pl.pallas_call(
        functools.partial(_nsa_prompt_kernel, nq=nq, nk=nk, nb=nb),
        grid_spec=pltpu.PrefetchScalarGridSpec(
            num_scalar_prefetch=2, grid=(b, len(pairs)),
            in_specs=[pl.BlockSpec((nq, 512), lambda i, s, qr, kr: (i * n_q + qr[s], C_NQ // 512)),
                      pl.BlockSpec((nk, 128), lambda i, s, qr, kr: (i * n_k + kr[s], C_NKV // 128 + 2)),
                      pl.BlockSpec((nk, 128), lambda i, s, qr, kr: (i * n_k + kr[s], C_NKV // 128 + 3)),
                      pl.BlockSpec((nk, 128), lambda i, s, qr, kr: (i * n_k + widx(qr[s], kr[s]), C_NWIN // 128)),
                      pl.BlockSpec((nk, 128), lambda i, s, qr, kr: (i * n_k + widx(qr[s], kr[s]), C_NWIN // 128 + 1)),
                      pl.BlockSpec((None, nb, 256), lambda i, s, qr, kr: (i, 0, 0)),
                      pl.BlockSpec((nq, 128), lambda i, s, qr, kr: (i * n_q + qr[s], C_SMALL // 128))],
            out_specs=pl.BlockSpec((nq, 512), lambda i, s, qr, kr: (i * n_q + qr[s], 0)),
            scratch_shapes=[pltpu.VMEM((KVH_NSA, rows4, 2 * HEAD_DIM), BF16),
                            pltpu.VMEM((KVH_NSA, rows4, HEAD_DIM), F32),
                            pltpu.VMEM((KVH_NSA * nb, nq), F32),
                            pltpu.VMEM((KVH_NSA, 1, rows4), F32), pltpu.VMEM((KVH_NSA, 1, rows4), F32),
                            pltpu.VMEM((KVH_NSA, 2 * HEAD_DIM, rows4), F32),
                            pltpu.VMEM((KVH_NSA, 1, rows4), F32), pltpu.VMEM((KVH_NSA, 1, rows4), F32),
                            pltpu.VMEM((KVH_NSA, 2 * HEAD_DIM, rows4), F32),
                            pltpu.SMEM((n_k,), jnp.int32)]),
        out_shape=jax.ShapeDtypeStruct((b * t, GROUP_WIDTH), F32),
        compiler_params=_cparams(("parallel", "arbitrary")),
        name="nsa_prompt",
    )(qi, ki, proj, proj, proj, proj, proj, cmp, proj)


def _col_to_row(col, eye):
    return jnp.sum(jnp.where(eye, col, 0.0), axis=0, keepdims=True)


def _row_to_col(row, eye):
    return jnp.sum(jnp.where(eye, row, 0.0), axis=1, keepdims=True)


def _gdn_qkv(act_q, act_k, h):
    sl = slice(h * GDN_DK, (h + 1) * GDN_DK)
    q = act_q[:, sl]
    k = act_k[:, sl]
    q = q * lax.rsqrt(jnp.sum(q * q, axis=-1, keepdims=True) + 1e-6) * (GDN_DK ** -0.5)
    k = k * lax.rsqrt(jnp.sum(k * k, axis=-1, keepdims=True) + 1e-6)
    return q, k


def _gdn_out(o, nw, gate):
    return _rms(o, nw) * (gate * jax.nn.sigmoid(gate))


def _silu(x):
    return x * jax.nn.sigmoid(x)


def _gdn_pre_kernel(q_ref, k_ref, v_ref, hq_ref, hk_ref, hv_ref, sm_ref, cw_ref, al_ref, dt_ref,
                    u_ref, w_ref, qe_ref, kd_ref, qk_ref, egl_ref, xbuf, act, *, rows):
    c = GDN_CHUNK
    w5 = GROUP_WIDTH
    first = pl.program_id(1) == 0
    for seg, (ref, halo) in enumerate(((q_ref, hq_ref), (k_ref, hk_ref), (v_ref, hv_ref))):
        xbuf[seg, 0:8, :] = jnp.where(first, 0.0, halo[...])
        xbuf[seg, 8:8 + rows, :] = ref[...]
        cols = slice(seg * w5, (seg + 1) * w5)
        act[seg] = _silu(_conv4(cw_ref, xbuf[seg, 5:5 + rows, :], xbuf[seg, 6:6 + rows, :],
                                xbuf[seg, 7:7 + rows, :], xbuf[seg, 8:8 + rows, :], cols))
    ri = _iota((c, c), 0)
    ci = _iota((c, c), 1)
    incl = ri >= ci
    strict = ri > ci
    eye = ri == ci
    incl_f = incl.astype(F32)
    hc = H_GDN * c
    eye_bd = (_iota((hc, hc), 0) == _iota((hc, hc), 1)).astype(F32)

    def chunk(idx, carry):
        r0 = pl.multiple_of(idx * c, c)
        rs = pl.ds(r0, c)
        sm = sm_ref[rs, :]
        beta_all = jax.nn.sigmoid(sm)
        la_all = -jnp.exp(al_ref[...]) * _softplus(sm + dt_ref[...])
        g_all = _dot(incl_f, la_all, precision=HI)
        egl_ref[rs, :] = jnp.broadcast_to(jnp.exp(g_all[c - 1:c, :]), (c, 128))
        act_q = act[0, rs, :]
        act_k = act[1, rs, :]
        act_v = act[2, rs, :]
        qks, a_rows, vbs, kbes = [], [], [], []
        zero = jnp.zeros((c, c), F32)
        for h in range(H_GDN):
            sl = slice(h * GDN_DK, (h + 1) * GDN_DK)
            q, k = _gdn_qkv(act_q, act_k, h)
            v = act_v[:, sl]
            beta = beta_all[:, L_DB + h:L_DB + h + 1]
            g = g_all[:, L_DA + h:L_DA + h + 1]
            g_row = _col_to_row(g, eye)
            decay = jnp.exp(jnp.where(incl, g - g_row, -jnp.inf))
            kb = k * beta
            a = jnp.where(strict, _dot_nt(kb, k) * decay, 0.0)
            a_rows.append(jnp.concatenate([a if j == h else zero for j in range(H_GDN)], axis=1))
            eg = jnp.exp(g)
            vbs.append(v * beta)
            kbes.append(kb * eg)
            qe_ref[rs, sl] = q * eg
            kd_ref[rs, sl] = k * jnp.exp(g[c - 1:c, :] - g)
            qks.append(_dot_nt(q, k) * decay)
        qk_ref[rs, :] = jnp.concatenate(qks, axis=1)
        a_bd = jnp.concatenate(a_rows, axis=0)
        tm = eye_bd - a_bd
        pw = _dot3(a_bd, a_bd)
        for it in range(5):
            tm = tm + _dot3(tm, pw)
            if it < 4:
                pw = _dot3(pw, pw)
        u_all = _dot(tm, jnp.concatenate(vbs, axis=0))
        w_all = _dot(tm, jnp.concatenate(kbes, axis=0))
        for h in range(H_GDN):
            sl = slice(h * GDN_DK, (h + 1) * GDN_DK)
            u_ref[rs, sl] = u_all[h * c:(h + 1) * c, :]
            w_ref[rs, sl] = w_all[h * c:(h + 1) * c, :]
        return carry

    lax.fori_loop(0, rows // c, chunk, 0, unroll=2)


def _gdn_seq_kernel(u_ref, w_ref, qe_ref, kd_ref, qk_ref, egl_ref, g_ref, nw_ref, o_ref, s_ref, *, rows):
    c = GDN_CHUNK

    @pl.when(pl.program_id(1) == 0)
    def _():
        s_ref[...] = jnp.zeros_like(s_ref)

    nw = nw_ref[...]

    def chunk(idx, carry):
        r0 = pl.multiple_of(idx * c, c)
        rs = pl.ds(r0, c)
        egl = egl_ref[pl.ds(r0, 1), :]
        for h in range(H_GDN):
            sl = slice(h * GDN_DK, (h + 1) * GDN_DK)
            s0 = s_ref[h]
            u_new = u_ref[rs, sl] - _dot(w_ref[rs, sl], s0)
            o = _dot(qe_ref[rs, sl], s0) + _dot(qk_ref[rs, h * c:(h + 1) * c], u_new)
            s_ref[h] = s0 * egl[:, L_DA + h:L_DA + h + 1] + _dot_tn(kd_ref[rs, sl], u_new)
            o_ref[rs, sl] = _gdn_out(o, nw, g_ref[rs, sl])
        return carry

    lax.fori_loop(0, rows // c, chunk, 0)


def _gdn_prompt(proj, prm, b, t):
    c = GDN_CHUNK
    w = GROUP_WIDTH
    rows = _pick(t, (512, 256, 128, 64))
    nr = t // rows
    full = lambda shape: pl.BlockSpec(shape, lambda i, j: (0,) * len(shape))
    seg = lambda col: pl.BlockSpec((rows, w), lambda i, j: (i * nr + j, col // w))
    halo = lambda col: pl.BlockSpec(
        (8, w), lambda i, j: (jnp.maximum((i * nr + j) * (rows // 8) - 1, 0), col // w))
    blk = lambda width: pl.BlockSpec((rows, width), lambda i, j: (i * nr + j, 0))
    f32 = lambda width: jax.ShapeDtypeStruct((b * t, width), F32)
    u, wk, qe, kd, qk, egl = pl.pallas_call(
        functools.partial(_gdn_pre_kernel, rows=rows),
        grid=(b, nr),
        in_specs=[seg(C_DQKV), seg(C_DQKV + w), seg(C_DQKV + 2 * w),
                  halo(C_DQKV), halo(C_DQKV + w), halo(C_DQKV + 2 * w),
                  pl.BlockSpec((rows, 128), lambda i, j: (i * nr + j, C_SMALL // 128)),
                  full((CONV_W, 3 * w)), full((1, 128)), full((1, 128))],
        out_specs=[blk(w), blk(w), blk(w), blk(w), blk(H_GDN * c), blk(128)],
        out_shape=[f32(w), f32(w), f32(w), f32(w), f32(H_GDN * c), f32(128)],
        scratch_shapes=[pltpu.VMEM((3, rows + 8, w), F32), pltpu.VMEM((3, rows, w), F32)],
        compiler_params=_cparams(("parallel", "parallel")),
        name="gdn_pre",
    )(proj, proj, proj, proj, proj, proj, proj, prm['gdn_conv_w'], prm['gdn_alog_row'], prm['gdn_dt_row'])
    return pl.pallas_call(
        functools.partial(_gdn_seq_kernel, rows=rows),
        grid=(b, nr),
        in_specs=[blk(w), blk(w), blk(w), blk(w), blk(H_GDN * c), blk(128), seg(C_DG), full((1, GDN_DK))],
        out_specs=[blk(w), pl.BlockSpec((None, H_GDN, GDN_DK, GDN_DK), lambda i, j: (i, 0, 0, 0))],
        out_shape=[f32(w), jax.ShapeDtypeStruct((b, H_GDN, GDN_DK, GDN_DK), F32)],
        compiler_params=_cparams(("parallel", "arbitrary")),
        name="gdn_seq",
    )(u, wk, qe, kd, qk, egl, proj, prm['gdn_norm_w'])


def _gdn_step_kernel(x_ref, g_ref, sm_ref, buf_ref, s0_ref, cw_ref, al_ref, dt_ref, nw_ref,
                     o_ref, s_ref):
    eye = _iota((GDN_DK, GDN_DK), 0) == _iota((GDN_DK, GDN_DK), 1)
    w = GROUP_WIDTH
    for smp in range(x_ref.shape[0]):
        buf = buf_ref[smp]
        act = _silu(_conv4(cw_ref, buf[0:1], buf[1:2], buf[2:3], x_ref[smp]))
        sm = sm_ref[smp]
        beta_all = jax.nn.sigmoid(sm)
        alpha_all = jnp.exp(-jnp.exp(al_ref[...]) * _softplus(sm + dt_ref[...]))
        gate = g_ref[smp]
        outs = []
        for h in range(H_GDN):
            sl = slice(h * GDN_DK, (h + 1) * GDN_DK)
            q, k = _gdn_qkv(act[:, 0:w], act[:, w:2 * w], h)
            v = act[:, 2 * w + h * GDN_DK:2 * w + (h + 1) * GDN_DK]
            beta = beta_all[:, L_DB + h:L_DB + h + 1]
            alpha = alpha_all[:, L_DA + h:L_DA + h + 1]
            k_col = _row_to_col(k, eye)
            q_col = _row_to_col(q, eye)
            s0 = s0_ref[smp, h]
            ks = jnp.sum(s0 * k_col, axis=0, keepdims=True)
            u = beta * v - (beta * alpha) * ks
            s1 = alpha * s0 + k_col * u
            s_ref[smp, h] = s1
            o = jnp.sum(s1 * q_col, axis=0, keepdims=True)
            outs.append(_gdn_out(o, nw_ref[...], gate[:, sl]))
        o_ref[smp] = jnp.concatenate(outs, axis=1)


def _gdn_step(x, gate, small, conv_buf, s0, prm):
    n = x.shape[0]
    w = GROUP_WIDTH
    ns = 1
    full = lambda shape: pl.BlockSpec(shape, lambda i: (0,) * len(shape))
    return pl.pallas_call(
        _gdn_step_kernel,
        grid=(n // ns,),
        in_specs=[pl.BlockSpec((ns, 1, 3 * w), lambda i: (i, 0, 0)),
                  pl.BlockSpec((ns, 1, w), lambda i: (i, 0, 0)),
                  pl.BlockSpec((ns, 1, 128), lambda i: (i, 0, 0)),
                  pl.BlockSpec((ns, CONV_W - 1, 3 * w), lambda i: (i, 0, 0)),
                  pl.BlockSpec((ns, H_GDN, GDN_DK, GDN_DK), lambda i: (i, 0, 0, 0)),
                  full((CONV_W, 3 * w)), full((1, 128)), full((1, 128)), full((1, GDN_DK))],
        out_specs=[pl.BlockSpec((ns, 1, w), lambda i: (i, 0, 0)),
                   pl.BlockSpec((ns, H_GDN, GDN_DK, GDN_DK), lambda i: (i, 0, 0, 0))],
        out_shape=[jax.ShapeDtypeStruct((n, 1, w), F32),
                   jax.ShapeDtypeStruct((n, H_GDN, GDN_DK, GDN_DK), F32)],
        compiler_params=_cparams(("parallel",)),
        name="gdn_step",
    )(x, gate, small, conv_buf, s0, prm['gdn_conv_w'], prm['gdn_alog_row'], prm['gdn_dt_row'],
      prm['gdn_norm_w'])


def _dot_row(row, mat, **kw):
    return _dot(jnp.broadcast_to(row, (8, row.shape[1])), mat, **kw)[0:1]


def _fox_step_kernel(pt_ref, *refs, n_pages):
    kv = refs[:n_pages]
    lf = refs[n_pages:2 * n_pages]
    x_ref, sm_ref, bf_ref, o_ref, lf_out_ref = refs[2 * n_pages:]
    w = GROUP_WIDTH
    pg = PAGE_SIZE
    x = x_ref[...]
    q, kn, vn = x[:, 0:w], x[:, w:2 * w], x[:, 2 * w:3 * w]
    logf_new = _log_sigmoid(sm_ref[...] + bf_ref[...])
    lf_out_ref[...] = logf_new
    head_cols = (_iota((w, N_HEADS), 0) // HEAD_DIM == _iota((w, N_HEADS), 1)).astype(BF16)
    head_rows = _iota((N_HEADS, w), 1) // HEAD_DIM == _iota((N_HEADS, w), 0)
    eye = _iota((w, w), 0) == _iota((w, w), 1)
    qblk = _dot(jnp.where(eye, q, 0.0).astype(BF16), head_cols).astype(BF16)
    tri = (_iota((pg, pg), 1) <= _iota((pg, pg), 0)).astype(F32)
    carry = jnp.zeros((1, N_HEADS), F32)
    cs = []
    for p in range(n_pages):
        c = _dot(tri, lf[p][...], precision=HI) + carry
        carry = c[pg - 1:pg, :]
        cs.append(c)
    ctot = carry + logf_new[:, 0:N_HEADS]
    s_n = _dot_row(kn.astype(BF16), qblk) * ATT_SCALE
    m = s_n
    ss = []
    stride = 2 * N_HEADS
    for p in range(n_pages):
        s = None
        for h in range(N_HEADS):
            k_h = kv[p][pl.ds(h, pg, stride=stride), :].astype(BF16)
            part = _dot(k_h, qblk[h * HEAD_DIM:(h + 1) * HEAD_DIM, :])
            s = part if s is None else s + part
        s = s * ATT_SCALE + (ctot - cs[p])
        ss.append(s)
        m = jnp.maximum(m, jnp.max(s, axis=0, keepdims=True))
    e_n = jnp.exp(s_n - m)
    l = e_n
    acc = [jnp.zeros((8, HEAD_DIM), F32) for _ in range(N_HEADS)]
    for p in range(n_pages):
        e = jnp.exp(ss[p] - m)
        l = l + jnp.sum(e, axis=0, keepdims=True)
        for h in range(N_HEADS):
            v_h = kv[p][pl.ds(N_HEADS + h, pg, stride=stride), :]
            acc[h] = acc[h] + jnp.sum((e[:, h:h + 1] * v_h).reshape(pg // 8, 8, HEAD_DIM), axis=0)
    tot = jnp.concatenate([jnp.sum(a, axis=0, keepdims=True) for a in acc], axis=1)
    tot = tot + _dot_row(e_n.astype(BF16), head_rows.astype(BF16)) * vn
    o_ref[...] = tot * _dot_row(1.0 / l, head_rows.astype(F32), precision=HI)


def _fox_step(pt, kv_pool, lf_pool, x, small, bf_row):
    n, n_pages = pt.shape
    w = GROUP_WIDTH
    page = lambda i, rows, width: pl.BlockSpec((None, rows, width), lambda b, t: (t[b, i], 0, 0))
    in_specs = ([page(i, PAGE_SIZE * 2 * N_HEADS, HEAD_DIM) for i in range(n_pages)] +
                [page(i, PAGE_SIZE, N_HEADS) for i in range(n_pages)] +
                [pl.BlockSpec((None, 1, 3 * w), lambda b, t: (b, 0, 0)),
                 pl.BlockSpec((None, 1, 128), lambda b, t: (b, 0, 0)),
                 pl.BlockSpec((1, 128), lambda b, t: (0, 0))])
    return pl.pallas_call(
        functools.partial(_fox_step_kernel, n_pages=n_pages),
        grid_spec=pltpu.PrefetchScalarGridSpec(
            num_scalar_prefetch=1, grid=(n,), in_specs=in_specs,
            out_specs=[pl.BlockSpec((None, 1, w), lambda b, t: (b, 0, 0)),
                       pl.BlockSpec((None, 1, 128), lambda b, t: (b, 0, 0))]),
        out_shape=[jax.ShapeDtypeStruct((n, 1, w), F32), jax.ShapeDtypeStruct((n, 1, 128), F32)],
        compiler_params=_cparams(("parallel",)),
        name="fox_step",
    )(pt, *([kv_pool] * n_pages), *([lf_pool] * n_pages), x, small, bf_row)


def _softmax_cols(scores):
    m = None
    for s in scores:
        mx = jnp.max(s, axis=0, keepdims=True)
        m = mx if m is None else jnp.maximum(m, mx)
    es = [jnp.exp(s - m) for s in scores]
    l = None
    for e in es:
        sm = jnp.sum(e, axis=0, keepdims=True)
        l = sm if l is None else l + sm
    return es, l


def _nsa_step_kernel(pt_ref, *refs, n_pages):
    pgs = refs[:n_pages]
    win_ref, q_ref, kv_ref, wn_ref, sm_ref, wc_ref, o_ref = refs[n_pages:]
    pg = PAGE_SIZE
    w = GROUP_WIDTH
    kvw = KVH_NSA * HEAD_DIM
    past = n_pages * pg
    nbp = past // NSA_BLOCK
    q = q_ref[...]
    r = _iota((kvw, w), 0)
    c = _iota((kvw, w), 1)
    fold = (c % HEAD_DIM == r % HEAD_DIM) & (c // (NSA_GROUP * HEAD_DIM) == r // HEAD_DIM)
    head_cols = (_iota((w, N_HEADS), 0) // HEAD_DIM == _iota((w, N_HEADS), 1)).astype(BF16)
    qn = _dot(jnp.where(fold, q, 0.0).astype(BF16), head_cols).astype(BF16)
    hl = _iota((1, N_HEADS), 1)
    slopes = jnp.exp2(-(hl + 1).astype(F32) * (8.0 / N_HEADS))
    spread = [_iota((N_HEADS, kvw), 0) == NSA_GROUP * (_iota((N_HEADS, kvw), 1) // HEAD_DIM) + i
              for i in range(NSA_GROUP)]
    spread_bf = [s.astype(BF16) for s in spread]
    spread_f = [s.astype(F32) for s in spread]

    def weighted_rows(es, vals):
        out = []
        for i in range(NSA_GROUP):
            tot = None
            for e, v in zip(es, vals):
                if e.shape[0] == 1:
                    part = _dot_row(e.astype(BF16), spread_bf[i]) * v
                else:
                    part = jnp.sum(_dot(e.astype(BF16), spread_bf[i]) * v, axis=0, keepdims=True)
                tot = part if tot is None else tot + part
            out.append(tot)
        return out

    wc = wc_ref[...]
    kc_rows, vc_rows = [], []
    for p in range(n_pages):
        rows = pgs[p][:, 0:2 * kvw]
        for half in range(pg // NSA_BLOCK):
            cm = jnp.sum(rows[half * NSA_BLOCK:(half + 1) * NSA_BLOCK] * wc, axis=0, keepdims=True)
            kc_rows.append(cm[:, 0:kvw])
            vc_rows.append(cm[:, kvw:2 * kvw])
    kcmp = jnp.concatenate(kc_rows, axis=0)
    vcmp = jnp.concatenate(vc_rows, axis=0)
    blk = _iota((nbp, 1), 0)
    dist_c = (past - ((blk + 1) * NSA_BLOCK - 1)).astype(F32)
    s_c = _dot(kcmp.astype(BF16), qn) * ATT_SCALE - slopes * dist_c
    (e_c,), l_c = _softmax_cols([s_c])
    p_c = e_c / l_c
    acc_c = weighted_rows([p_c], [vcmp])

    cur = past // NSA_BLOCK
    forced = (blk == 0) | (blk >= cur - 1)
    eye_b = _iota((nbp, nbp), 0) == _iota((nbp, nbp), 1)
    lower = _iota((nbp, nbp), 1) < _iota((nbp, nbp), 0)
    n_sel = min(NSA_TOPN, nbp + 1)
    sels = []
    for g in range(KVH_NSA):
        imp = p_c[:, NSA_GROUP * g:NSA_GROUP * g + 1]
        for i in range(1, NSA_GROUP):
            imp = imp + p_c[:, NSA_GROUP * g + i:NSA_GROUP * g + i + 1]
        score = jnp.where(forced, FORCED_SCORE, imp)
        score_row = _col_to_row(score, eye_b)
        beats = (score_row > score) | ((score_row == score) & lower)
        cnt = jnp.sum(jnp.where(beats, 1.0, 0.0), axis=1, keepdims=True)
        cnt = cnt + jnp.where(score < FORCED_SCORE, 1.0, 0.0)
        sels.append(jnp.where(cnt < n_sel, 1.0, 0.0))
    sel8 = jnp.where(hl < NSA_GROUP, sels[0], sels[1])

    kv_new = kv_ref[...]
    first_half = _iota((pg, 1), 0) < NSA_BLOCK
    scores, vals = [], []
    for p in range(n_pages):
        dist = (past - (p * pg + _iota((pg, 1), 0))).astype(F32)
        s = _dot(pgs[p][:, 2 * kvw:3 * kvw].astype(BF16), qn) * ATT_SCALE - slopes * dist
        chosen = jnp.where(first_half, sel8[2 * p:2 * p + 1, :], sel8[2 * p + 1:2 * p + 2, :])
        scores.append(jnp.where(chosen > 0.5, s, NEG))
        vals.append(pgs[p][:, 3 * kvw:4 * kvw])
    scores.append(_dot_row(kv_new[:, 2 * kvw:3 * kvw].astype(BF16), qn) * ATT_SCALE)
    vals.append(kv_new[:, 3 * kvw:4 * kvw])
    e_s, l_s = _softmax_cols(scores)
    acc_s = weighted_rows(e_s, vals)

    nw = win_ref.shape[0]
    wrow = _iota((nw, 1), 0)
    dist_w = nw - wrow
    s_w = _dot(win_ref[:, 0:kvw].astype(BF16), qn) * ATT_SCALE - slopes * dist_w.astype(F32)
    s_w = jnp.where((dist_w < NSA_WINDOW) & (past - dist_w >= 0), s_w, NEG)
    wn = wn_ref[...]
    s_wn = _dot_row(wn[:, 0:kvw].astype(BF16), qn) * ATT_SCALE
    e_w, l_w = _softmax_cols([s_w, s_wn])
    acc_w = weighted_rows(e_w, [win_ref[:, kvw:2 * kvw], wn[:, kvw:2 * kvw]])

    gates = jax.nn.sigmoid(sm_ref[...])
    lane = _iota((128, N_HEADS), 0)
    head = _iota((128, N_HEADS), 1)
    coef = []
    for k in range(3):
        pick = (lane == L_NG + 3 * head + k).astype(F32)
        coef.append(_dot_row(gates, pick, precision=HI))
    coef[1] = coef[1] / l_s
    coef[2] = coef[2] / l_w
    tots = []
    for i in range(NSA_GROUP):
        t = _dot_row(coef[0], spread_f[i], precision=HI) * acc_c[i]
        t = t + (_dot_row(coef[1], spread_f[i], precision=HI) * acc_s[i]
                 + _dot_row(coef[2], spread_f[i], precision=HI) * acc_w[i])
        tots.append(t)
    o_ref[...] = jnp.concatenate([t[:, 0:HEAD_DIM] for t in tots] + [t[:, HEAD_DIM:kvw] for t in tots], axis=1)


def _nsa_step(pt, kv_pool, win, q, kv_new, win_new, small, wcmp):
    n, n_pages = pt.shape
    w = GROUP_WIDTH
    nw = win.shape[1]
    row = lambda width: pl.BlockSpec((None, 1, width), lambda b, t: (b, 0, 0))
    in_specs = ([pl.BlockSpec((None, PAGE_SIZE, w), (lambda i: (lambda b, t: (t[b, i], 0, 0)))(i))
                 for i in range(n_pages)] +
                [pl.BlockSpec((None, nw, 256), lambda b, t: (b, 0, 0)),
                 row(w), row(w), row(256), row(128),
                 pl.BlockSpec((NSA_BLOCK, 256), lambda b, t: (0, 0))])
    return pl.pallas_call(
        functools.partial(_nsa_step_kernel, n_pages=n_pages),
        grid_spec=pltpu.PrefetchScalarGridSpec(
            num_scalar_prefetch=1, grid=(n,), in_specs=in_specs,
            out_specs=pl.BlockSpec((None, 1, w), lambda b, t: (b, 0, 0))),
        out_shape=jax.ShapeDtypeStruct((n, 1, w), F32),
        compiler_params=_cparams(("parallel",)),
        name="nsa_step",
    )(pt, *([kv_pool] * n_pages), win, q, kv_new, win_new, small, wcmp)


def _head_cols(row, eye):
    return [_row_to_col(row[:, h * HEAD_DIM:(h + 1) * HEAD_DIM], eye) for h in range(N_HEADS)]


def _rows_to_block(rows):
    sub = _iota((N_HEADS, rows[0].shape[1]), 0)
    out = jnp.broadcast_to(rows[0], sub.shape)
    for h in range(1, N_HEADS):
        out = jnp.where(sub == h, rows[h], out)
    return out


def _tile_scores(tile_of_head, q_cols):
    return _rows_to_block([jnp.sum(tile_of_head(h) * q_cols[h], axis=0, keepdims=True)
                           for h in range(N_HEADS)])


def _fox_step_kernel_t(pt_ref, *refs, n_pages):
    kv = refs[:n_pages]
    lf = refs[n_pages:2 * n_pages]
    x_ref, sm_ref, bf_ref, o_ref, lf_out_ref = refs[2 * n_pages:]
    w = GROUP_WIDTH
    pg = PAGE_SIZE
    hd = HEAD_DIM
    x = x_ref[...]
    q, kn, vn = x[:, 0:w], x[:, w:2 * w], x[:, 2 * w:3 * w]
    logf_new = _log_sigmoid(sm_ref[...] + bf_ref[...])
    lf_out_ref[...] = logf_new
    eye = _iota((hd, hd), 0) == _iota((hd, hd), 1)
    eye8 = _iota((N_HEADS, N_HEADS), 0) == _iota((N_HEADS, N_HEADS), 1)
    q_cols = _head_cols(q, eye)
    vn_cols = _head_cols(vn, eye)
    sub = _iota((N_HEADS, 1), 0)
    upper = (_iota((pg, pg), 0) <= _iota((pg, pg), 1)).astype(F32)
    c_all = _dot(jnp.concatenate([lf[p][...] for p in range(n_pages)], axis=0), upper, precision=HI)
    carry = jnp.zeros((N_HEADS, 1), F32)
    cs = []
    for p in range(n_pages):
        c = c_all[p * N_HEADS:(p + 1) * N_HEADS, :]
        cs.append(c + carry)
        carry = carry + c[:, pg - 1:pg]
    ctot = carry + _row_to_col(logf_new[:, 0:N_HEADS], eye8)
    prod = kn * q
    s_n = jnp.zeros((N_HEADS, 1), F32)
    for h in range(N_HEADS):
        s_n = jnp.where(sub == h, jnp.sum(prod[:, h * hd:(h + 1) * hd], axis=1, keepdims=True), s_n)
    s_n = s_n * ATT_SCALE
    m = s_n
    ss = []
    for p in range(n_pages):
        s = _tile_scores(lambda h: kv[p][0, h], q_cols) * ATT_SCALE + (ctot - cs[p])
        ss.append(s)
        m = jnp.maximum(m, jnp.max(s, axis=1, keepdims=True))
    e_n = jnp.exp(s_n - m)
    l = e_n
    es = []
    for p in range(n_pages):
        e = jnp.exp(ss[p] - m)
        es.append(e)
        l = l + jnp.sum(e, axis=1, keepdims=True)
    inv_l = 1.0 / l
    outs = []
    for h in range(N_HEADS):
        acc = kv[0][1, h] * es[0][h:h + 1, :]
        for p in range(1, n_pages):
            acc = acc + kv[p][1, h] * es[p][h:h + 1, :]
        o_col = jnp.sum(acc, axis=1, keepdims=True) + vn_cols[h] * e_n[h:h + 1, :]
        outs.append(_col_to_row(o_col * inv_l[h:h + 1, :], eye))
    o_ref[...] = jnp.concatenate(outs, axis=1)


def _fox_step_t(pt, layer, kv_t, lf_t, x, small, bf_row):
    n, n_pages = pt.shape
    w = GROUP_WIDTH
    kv_spec = lambda i: pl.BlockSpec((None, None, 2, N_HEADS, HEAD_DIM, PAGE_SIZE),
                                     lambda b, t: (layer, t[b, i], 0, 0, 0, 0))
    lf_spec = lambda i: pl.BlockSpec((None, None, N_HEADS, PAGE_SIZE), lambda b, t: (layer, t[b, i], 0, 0))
    in_specs = ([kv_spec(i) for i in range(n_pages)] + [lf_spec(i) for i in range(n_pages)] +
                [pl.BlockSpec((None, 1, 3 * w), lambda b, t: (b, 0, 0)),
                 pl.BlockSpec((None, 1, 128), lambda b, t: (b, 0, 0)),
                 pl.BlockSpec((1, 128), lambda b, t: (0, 0))])
    return pl.pallas_call(
        functools.partial(_fox_step_kernel_t, n_pages=n_pages),
        grid_spec=pltpu.PrefetchScalarGridSpec(
            num_scalar_prefetch=1, grid=(n,), in_specs=in_specs,
            out_specs=[pl.BlockSpec((None, 1, w), lambda b, t: (b, 0, 0)),
                       pl.BlockSpec((None, 1, 128), lambda b, t: (b, 0, 0))]),
        out_shape=[jax.ShapeDtypeStruct((n, 1, w), F32), jax.ShapeDtypeStruct((n, 1, 128), F32)],
        compiler_params=_cparams(("parallel",)),
        name="fox_step",
    )(pt, *([kv_t] * n_pages), *([lf_t] * n_pages), x, small, bf_row)


def _nsa_step_kernel_t(pt_ref, *refs, n_pages):
    pgs = refs[:n_pages]
    win_ref, q_ref, kv_ref, wn_ref, sm_ref, wc_ref, o_ref = refs[n_pages:]
    pg = PAGE_SIZE
    hd = HEAD_DIM
    nh = N_HEADS
    past = n_pages * pg
    nbp = past // NSA_BLOCK
    bpp = pg // NSA_BLOCK
    eye = _iota((hd, hd), 0) == _iota((hd, hd), 1)
    q_cols = _head_cols(q_ref[...], eye)
    sub = _iota((nh, 1), 0)
    lane = _iota((1, pg), 1)
    slopes = jnp.exp2(-(sub + 1).astype(F32) * (8.0 / nh))
    grp = lambda h: h // NSA_GROUP
    kv_new = kv_ref[...]
    wn = wn_ref[...]
    wc = wc_ref[...]

    def new_scores(row):
        out = jnp.zeros((nh, 1), F32)
        for h in range(nh):
            k_col = _row_to_col(row[:, grp(h) * hd:(grp(h) + 1) * hd], eye)
            out = jnp.where(sub == h, jnp.sum(k_col * q_cols[h], axis=0, keepdims=True), out)
        return out * ATT_SCALE

    def softmax_lanes(scores, extra=None):
        m = extra
        for s in scores:
            mx = jnp.max(s, axis=1, keepdims=True)
            m = mx if m is None else jnp.maximum(m, mx)
        es = [jnp.exp(s - m) for s in scores]
        l = None if extra is None else jnp.exp(extra - m)
        e_extra = l
        for e in es:
            sm = jnp.sum(e, axis=1, keepdims=True)
            l = sm if l is None else l + sm
        return es, e_extra, l

    def weighted_values(tiles, es):
        cols = []
        for h in range(nh):
            acc = None
            for tile, e in zip(tiles, es):
                part = tile(grp(h)) * e[h:h + 1, :]
                acc = part if acc is None else acc + part
            cols.append(jnp.sum(acc, axis=1, keepdims=True))
        return cols

    half = [lane // NSA_BLOCK == i for i in range(bpp)]
    blk_lane = _iota((1, pg), 1)
    s_c = jnp.full((nh, pg), NEG, F32)
    for p in range(n_pages):
        raw = _tile_scores(lambda h: pgs[p][0, grp(h)], q_cols) * wc[0:1, :]
        for i in range(bpp):
            r = jnp.sum(jnp.where(half[i], raw, 0.0), axis=1, keepdims=True)
            s_c = jnp.where(blk_lane == bpp * p + i, r, s_c)
    dist_c = (past - ((blk_lane + 1) * NSA_BLOCK - 1)).astype(F32)
    s_c = jnp.where(blk_lane < nbp, s_c * ATT_SCALE - slopes * dist_c, NEG)
    (e_c,), _, l_c = softmax_lanes([s_c])
    p_c = e_c / l_c
    pes = []
    for p in range(n_pages):
        pe = jnp.zeros((nh, pg), F32)
        for i in range(bpp):
            pe = jnp.where(half[i], p_c[:, bpp * p + i:bpp * p + i + 1], pe)
        pes.append(pe * wc[1:2, :])
    o_c = weighted_values([(lambda g, p=p: pgs[p][1, g]) for p in range(n_pages)], pes)

    nbl = 32 * ((nbp + 31) // 32)
    eye_b = _iota((nbl, nbl), 0) == _iota((nbl, nbl), 1)
    m_lt_n = _iota((nbl, nbl), 0) < _iota((nbl, nbl), 1)
    blk_row = _iota((1, nbl), 1)
    n_sel = min(NSA_TOPN, nbp + 1)
    sel_rows = []
    for g in range(KVH_NSA):
        imp = jnp.sum(jnp.where(sub // NSA_GROUP == g, p_c, 0.0), axis=0, keepdims=True)[:, 0:nbl]
        forced = (blk_row == 0) | (blk_row >= nbp - 1)
        score = jnp.where(blk_row < nbp, jnp.where(forced, FORCED_SCORE, imp), -1.0)
        score_col = _row_to_col(score, eye_b)
        beats = jnp.where(score_col > score, 1.0, jnp.where((score_col == score) & m_lt_n, 1.0, 0.0))
        cnt = jnp.sum(beats, axis=0, keepdims=True)
        cnt = cnt + jnp.where(score < FORCED_SCORE, 1.0, 0.0)
        sel_rows.append(jnp.where((cnt < n_sel) & (score >= 0.0), 1.0, 0.0))

    scores = []
    for p in range(n_pages):
        dist = (past - (p * pg + lane)).astype(F32)
        s = _tile_scores(lambda h: pgs[p][2, grp(h)], q_cols) * ATT_SCALE - slopes * dist
        chosen = []
        for g in range(KVH_NSA):
            row = jnp.zeros((1, pg), F32)
            for i in range(bpp):
                row = jnp.where(half[i], sel_rows[g][:, bpp * p + i:bpp * p + i + 1], row)
            chosen.append(row)
        mask = jnp.where(sub // NSA_GROUP == 0, chosen[0], chosen[1])
        scores.append(jnp.where(mask > 0.5, s, NEG))
    s_new = new_scores(kv_new[:, 2 * 128:3 * 128])
    e_s, e_sn, l_s = softmax_lanes(scores, s_new)
    o_s = weighted_values([(lambda g, p=p: pgs[p][3, g]) for p in range(n_pages)], e_s)
    v_new = kv_new[:, 3 * 128:4 * 128]
    o_s = [o_s[h] + _row_to_col(v_new[:, grp(h) * hd:(grp(h) + 1) * hd], eye) * e_sn[h:h + 1, :]
           for h in range(nh)]

    nw = win_ref.shape[-1]
    scores = []
    for c0 in range(0, nw, pg):
        dist = nw - (c0 + lane)
        s = _tile_scores(lambda h: win_ref[0, grp(h), :, c0:c0 + pg], q_cols) * ATT_SCALE
        s = s - slopes * dist.astype(F32)
        scores.append(jnp.where((dist < NSA_WINDOW) & (past - dist >= 0), s, NEG))
    s_new = new_scores(wn[:, 0:128])
    e_w, e_wn, l_w = softmax_lanes(scores, s_new)
    o_w = weighted_values([(lambda g, c0=c0: win_ref[1, g, :, c0:c0 + pg]) for c0 in range(0, nw, pg)], e_w)
    vw_new = wn[:, 128:256]
    o_w = [o_w[h] + _row_to_col(vw_new[:, grp(h) * hd:(grp(h) + 1) * hd], eye) * e_wn[h:h + 1, :]
           for h in range(nh)]

    gates = jax.nn.sigmoid(sm_ref[...])
    outs = []
    for h in range(nh):
        c = L_NG + 3 * h
        o = gates[:, c:c + 1] * o_c[h]
        o = o + ((gates[:, c + 1:c + 2] / l_s[h:h + 1, :]) * o_s[h]
                 + (gates[:, c + 2:c + 3] / l_w[h:h + 1, :]) * o_w[h])
        outs.append(_col_to_row(o, eye))
    o_ref[...] = jnp.concatenate(outs, axis=1)


def _nsa_step_t(pt, layer, kv_t, win_t, q, kv_new, win_new, small, wc_rows):
    n, n_pages = pt.shape
    w = GROUP_WIDTH
    nw = win_t.shape[-1]
    row = lambda width: pl.BlockSpec((None, 1, width), lambda b, t: (b, 0, 0))
    page = lambda i: pl.BlockSpec((None, None, 4, KVH_NSA, HEAD_DIM, PAGE_SIZE),
                                  lambda b, t: (layer, t[b, i], 0, 0, 0, 0))
    in_specs = ([page(i) for i in range(n_pages)] +
                [pl.BlockSpec((None, None, 2, KVH_NSA, HEAD_DIM, nw), lambda b, t: (layer, b, 0, 0, 0, 0)),
                 row(w), row(w), row(256), row(128),
                 pl.BlockSpec((2, PAGE_SIZE), lambda b, t: (0, 0))])
    return pl.pallas_call(
        functools.partial(_nsa_step_kernel_t, n_pages=n_pages),
        grid_spec=pltpu.PrefetchScalarGridSpec(
            num_scalar_prefetch=1, grid=(n,), in_specs=in_specs,
            out_specs=pl.BlockSpec((None, 1, w), lambda b, t: (b, 0, 0))),
        out_shape=jax.ShapeDtypeStruct((n, 1, w), F32),
        compiler_params=_cparams(("parallel",)),
        name="nsa_step",
    )(pt, *([kv_t] * n_pages), win_t, q, kv_new, win_new, small, wc_rows)


_O_FF, _O_LX, _O_NQ, _O_NKC, _O_NG, _O_DQKV, _O_DG, _O_DA, _O_END = (
    1536, 1544, 2568, 3080, 3848, 3872, 5408, 5920, 5928)


def _pack_w_in(w):
    d = w.shape[0]
    zeros = lambda n: jnp.zeros((d, n), w.dtype)
    parts = [w[:, 0:_O_FF], w[:, _O_LX:_O_NQ], w[:, _O_NQ:_O_NKC], w[:, _O_NKC:_O_NG],
             w[:, _O_FF:_O_LX], w[:, _O_NG:_O_DQKV], w[:, _O_DA:_O_END], zeros(128 - 40), zeros(128),
             w[:, _O_DQKV:_O_DG], w[:, _O_DG:_O_DA]]
    out = jnp.concatenate(parts, axis=1)
    assert out.shape[1] == N_PROJ
    return out.astype(BF16)


def _block_diag(w):
    n, d, e = w.shape
    eye = jnp.eye(n, dtype=w.dtype)
    return (eye[:, None, :, None] * w[:, :, None, :]).reshape(n * d, n * e)


def _lane_row(vals, start):
    return jnp.zeros((1, 128), F32).at[0, start:start + vals.shape[0]].set(vals)


def _layer_params(l, p):
    row = lambda a: a[l][None, :]
    return dict(
        norm_pre_mix=row(p['norm_pre_mix']), norm_post_mix=row(p['norm_post_mix']),
        norm_pre_mlp=row(p['norm_pre_mlp']), norm_post_mlp=row(p['norm_post_mlp']),
        w_in=_pack_w_in(p['w_in'][l]), w_out=p['w_out'][l].astype(BF16),
        w_up=p['w_up'][l].astype(BF16), w_down=p['w_down'][l].astype(BF16),
        fox_bf_row=_lane_row(p['fox_b_f'][l], L_FF),
        lru_conv_w=p['lru_conv_w'][l], lru_conv_b=row(p['lru_conv_b']),
        lru_wr=_block_diag(p['lru_w_r'][l]).astype(BF16), lru_b_r=row(p['lru_b_r']),
        lru_wi=_block_diag(p['lru_w_i'][l]).astype(BF16), lru_b_i=row(p['lru_b_i']),
        lru_lambda=row(p['lru_lambda']),
        nsa_wcmp=jnp.repeat(p['nsa_w_cmp'][l].T, 128, axis=1),
        gdn_conv_w=p['gdn_conv_w'][l],
        gdn_alog_row=_lane_row(p['gdn_A_log'][l], L_DA), gdn_dt_row=_lane_row(p['gdn_dt_bias'][l], L_DA),
        gdn_norm_w=row(p['gdn_norm_w']))


def kernel(x_prompt, x_sample, cache_fox_kv, cache_fox_logf, cache_nsa_kv, cache_nsa_win,
           state_rglru_conv, state_rglru_h, state_gdn_conv, state_gdn_S, page_table,
           norm_pre_mix, norm_post_mix, norm_pre_mlp, norm_post_mlp, w_in, w_out, w_up, w_down,
           fox_b_f, lru_conv_w, lru_conv_b, lru_w_r, lru_b_r, lru_w_i, lru_b_i, lru_lambda,
           nsa_w_cmp, gdn_conv_w, gdn_A_log, gdn_dt_bias, gdn_norm_w):
    params = dict(norm_pre_mix=norm_pre_mix, norm_post_mix=norm_post_mix, norm_pre_mlp=norm_pre_mlp,
                  norm_post_mlp=norm_post_mlp, w_in=w_in, w_out=w_out, w_up=w_up, w_down=w_down,
                  fox_b_f=fox_b_f, lru_conv_w=lru_conv_w, lru_conv_b=lru_conv_b, lru_w_r=lru_w_r,
                  lru_b_r=lru_b_r, lru_w_i=lru_w_i, lru_b_i=lru_b_i, lru_lambda=lru_lambda,
                  nsa_w_cmp=nsa_w_cmp, gdn_conv_w=gdn_conv_w, gdn_A_log=gdn_A_log,
                  gdn_dt_bias=gdn_dt_bias, gdn_norm_w=gdn_norm_w)
    b, t, d = x_prompt.shape
    ns = x_sample.shape[0]
    depth, n_pool = cache_fox_kv.shape[:2]
    bt = b * t
    w = GROUP_WIDTH
    xp = x_prompt.reshape(bt, d)
    xs = x_sample.reshape(ns, d)
    fox_t = jnp.transpose(cache_fox_kv, (0, 1, 3, 4, 5, 2))
    logf_t = jnp.transpose(cache_fox_logf, (0, 1, 3, 2))
    nsa_t = jnp.transpose(cache_nsa_kv, (0, 1, 3, 4, 5, 2))
    win_t = jnp.transpose(cache_nsa_win, (0, 1, 3, 4, 5, 2))
    outs = [[] for _ in range(16)]
    for l in range(depth):
        prm = _layer_params(l, params)
        proj = _in_proj(xp, prm['norm_pre_mix'], prm['w_in'])
        ps = _in_proj(xs, prm['norm_pre_mix'], prm['w_in'])
        pp = proj.reshape(b, t, N_PROJ)
        row3 = lambda c0, c1: ps[:, None, c0:c1]

        qa, ka, va, logf_p = _fox_prep(proj, prm['fox_bf_row'], b, t)
        oa_p = _fox_flash(qa, ka, va, b, t)
        oa_s, logf_s = _fox_step_t(page_table, l, fox_t, logf_t, row3(C_FQ, C_FQ + 3 * w),
                                   row3(C_SMALL, C_SMALL + 128), prm['fox_bf_row'])

        ob_p, h_p = _lru_prompt(proj, prm, b, t)
        ob_s, h_s = _lru_step(ps[:, C_LX:C_LX + w], ps[:, C_LG:C_LG + w],
                              jnp.moveaxis(state_rglru_conv[l], 1, 0), state_rglru_h[l], prm)

        cmp = _nsa_cmp(proj, prm['nsa_wcmp'], b, t)
        oc_p = _nsa_prompt(proj, cmp, b, t)
        oc_s = _nsa_step_t(page_table, l, nsa_t, win_t, row3(C_NQ, C_NQ + w), row3(C_NKV, C_NKV + w),
                           row3(C_NWIN, C_NWIN + 256), row3(C_SMALL, C_SMALL + 128),
                           jnp.tile(params['nsa_w_cmp'][l], (1, PAGE_SIZE // NSA_BLOCK)))

        od_p, s_p = _gdn_prompt(proj, prm, b, t)
        od_s, s_s = _gdn_step(row3(C_DQKV, C_DQKV + 3 * w), row3(C_DG, C_DG + w),
                              row3(C_SMALL, C_SMALL + 128), state_gdn_conv[l], state_gdn_S[l], prm)

        xp = _out_proj(oa_p, ob_p, oc_p, od_p, xp, prm['w_out'], prm['norm_post_mix'])
        xp = _mlp(xp, prm['norm_pre_mlp'], prm['w_up'], prm['w_down'], prm['norm_post_mlp'])
        flat = lambda a: a.reshape(ns, w)
        xs = _out_proj(flat(oa_s), ob_s, flat(oc_s), flat(od_s), xs, prm['w_out'], prm['norm_post_mix'])
        xs = _mlp(xs, prm['norm_pre_mlp'], prm['w_up'], prm['w_down'], prm['norm_post_mlp'])

        new_win_s = ps[:, C_NWIN:C_NWIN + 256].reshape(ns, 1, 2, KVH_NSA, HEAD_DIM)
        win_all = jnp.concatenate([cache_nsa_win[l], new_win_s], axis=1)
        layer_out = [
            pp[:, :, C_FK:C_FK + 2 * w].reshape(b, t, 2, N_HEADS, HEAD_DIM),
            ps[:, C_FK:C_FK + 2 * w].reshape(ns, 1, 2, N_HEADS, HEAD_DIM),
            logf_p[:, :N_HEADS].reshape(b, t, N_HEADS),
            logf_s[:, :, :N_HEADS],
            pp[:, :, C_NKV:C_NKV + w].reshape(b, t, 4, KVH_NSA, HEAD_DIM),
            ps[:, C_NKV:C_NKV + w].reshape(ns, 1, 4, KVH_NSA, HEAD_DIM),
            pp[:, t - min(NSA_WINDOW, t):, C_NWIN:C_NWIN + 256].reshape(b, min(NSA_WINDOW, t), 2, KVH_NSA, HEAD_DIM),
            win_all[:, -NSA_WINDOW:],
            pp[:, t - (CONV_W - 1):, C_LX:C_LX + w],
            jnp.concatenate([state_rglru_conv[l][:, 1:], ps[:, None, C_LX:C_LX + w]], axis=1),
            h_p[:, 0],
            h_s,
            pp[:, t - (CONV_W - 1):, C_DQKV:C_DQKV + 3 * w],
            jnp.concatenate([state_gdn_conv[l][:, 1:], ps[:, None, C_DQKV:C_DQKV + 3 * w]], axis=1),
            s_p,
            s_s,
        ]
        for i, a in enumerate(layer_out):
            outs[i].append(a)
    y_prompt = xp.reshape(b, t, d)
    y_sample = xs.reshape(ns, 1, d)
    return (y_prompt, y_sample) + tuple(jnp.stack(o) for o in outs)
```

```python
import functools

import jax
import jax.numpy as jnp
from jax import lax
from jax.experimental import pallas as pl
from jax.experimental.pallas import tpu as pltpu

F32 = jnp.float32
BF16 = jnp.bfloat16
HI = lax.Precision.HIGHEST

D_MODEL = 2048
GROUP_WIDTH = 512
HEAD_DIM = 64
N_HEADS = 8
KVH_NSA = 2
NSA_GROUP = 4
NSA_BLOCK = 64
NSA_TOPN = 16
NSA_WINDOW = 512
FORCED_SCORE = 1e4
H_GDN = 4
GDN_DK = 128
GDN_CHUNK = 64
LRU_C = 8.0
CONV_W = 4
D_FF = 4 * D_MODEL
RMS_EPS = 1e-6
NEG = -1e30
ATT_SCALE = HEAD_DIM ** -0.5
PAGE_SIZE = 128

C_FQ, C_FK, C_FV = 0, 512, 1024
C_LX, C_LG = 1536, 2048
C_NQ = 2560
C_NKV = 3072
C_NWIN = 3584
C_SMALL = 3840
C_DQKV = 4096
C_DG = 5632
N_PROJ = 6144
L_FF, L_NG, L_DA, L_DB = 0, 8, 32, 36

VMEM_LIMIT = 56 * 1024 * 1024


def _cparams(sem):
    return pltpu.CompilerParams(dimension_semantics=sem, vmem_limit_bytes=VMEM_LIMIT)


def _pick(n, prefs):
    for p in prefs:
        if n % p == 0:
            return p
    return n


def _rms(xf, g):
    return xf * lax.rsqrt(jnp.mean(xf * xf, axis=-1, keepdims=True) + RMS_EPS) * g


def _softplus(x):
    return jnp.maximum(x, 0.0) + jnp.log1p(jnp.exp(-jnp.abs(x)))


def _log_sigmoid(x):
    return -_softplus(-x)


def _iota(shape, dim):
    return lax.broadcasted_iota(jnp.int32, shape, dim)


def _dot(a, b, **kw):
    return jnp.dot(a, b, preferred_element_type=F32, **kw)


def _dot3(a, b):
    a_hi = a.astype(BF16)
    b_hi = b.astype(BF16)
    a_lo = (a - a_hi.astype(F32)).astype(BF16)
    b_lo = (b - b_hi.astype(F32)).astype(BF16)
    return _dot(a_hi, b_hi) + (_dot(a_hi, b_lo) + _dot(a_lo, b_hi))


def _dot_nt(a, b):
    return lax.dot_general(a, b, (((1,), (1,)), ((), ())), preferred_element_type=F32)


def _dot_tn(a, b):
    return lax.dot_general(a, b, (((0,), (0,)), ((), ())), preferred_element_type=F32)


def _in_proj_kernel(x_ref, g_ref, w_ref, o_ref, h_ref):
    @pl.when(pl.program_id(1) == 0)
    def _():
        h_ref[...] = _rms(x_ref[...], g_ref[...]).astype(BF16)

    o_ref[...] = _dot(h_ref[...], w_ref[...])


def _in_proj(x, g, w):
    m, d = x.shape
    n = w.shape[1]
    tm = _pick(m, (1024, 512, 256, 128, 64, 32, 16, 8))
    tn = _pick(n, (1536, 1024, 768, 512, 256, 128))
    return pl.pallas_call(
        _in_proj_kernel,
        grid=(m // tm, n // tn),
        in_specs=[pl.BlockSpec((tm, d), lambda i, j: (i, 0)),
                  pl.BlockSpec((1, d), lambda i, j: (0, 0)),
                  pl.BlockSpec((d, tn), lambda i, j: (0, j))],
        out_specs=pl.BlockSpec((tm, tn), lambda i, j: (i, j)),
        out_shape=jax.ShapeDtypeStruct((m, n), F32),
        scratch_shapes=[pltpu.VMEM((tm, d), BF16)],
        compiler_params=_cparams(("parallel", "arbitrary")),
        name="in_proj",
    )(x, g, w)


def _out_proj_kernel(oa_ref, ob_ref, oc_ref, od_ref, x_ref, w_ref, g_ref, o_ref):
    y = _dot(oa_ref[...].astype(BF16), w_ref[0:512, :])
    y = y + _dot(ob_ref[...].astype(BF16), w_ref[512:1024, :])
    y = y + _dot(oc_ref[...].astype(BF16), w_ref[1024:1536, :])
    y = y + _dot(od_ref[...].astype(BF16), w_ref[1536:2048, :])
    o_ref[...] = x_ref[...] + _rms(y, g_ref[...])


def _out_proj(oa, ob, oc, od, x, w, g):
    m, d = x.shape
    tm = _pick(m, (640, 512, 256, 128, 64, 32, 16, 8))
    gw = GROUP_WIDTH
    return pl.pallas_call(
        _out_proj_kernel,
        grid=(m // tm,),
        in_specs=[pl.BlockSpec((tm, gw), lambda i: (i, 0))] * 4 + [
            pl.BlockSpec((tm, d), lambda i: (i, 0)),
            pl.BlockSpec((d, d), lambda i: (0, 0)),
            pl.BlockSpec((1, d), lambda i: (0, 0))],
        out_specs=pl.BlockSpec((tm, d), lambda i: (i, 0)),
        out_shape=jax.ShapeDtypeStruct((m, d), F32),
        compiler_params=_cparams(("parallel",)),
        name="out_proj",
    )(oa, ob, oc, od, x, w, g)


def _mlp_kernel(x_ref, g1_ref, wu_ref, wd_ref, g2_ref, o_ref, h_ref, acc_ref):
    f = pl.program_id(1)

    @pl.when(f == 0)
    def _():
        h_ref[...] = _rms(x_ref[...], g1_ref[...]).astype(BF16)
        acc_ref[...] = jnp.zeros_like(acc_ref)

    u = jnp.maximum(_dot(h_ref[...], wu_ref[...]), 0.0)
    acc_ref[...] += _dot((u * u).astype(BF16), wd_ref[...])

    @pl.when(f == pl.num_programs(1) - 1)
    def _():
        o_ref[...] = x_ref[...] + _rms(acc_ref[...], g2_ref[...])


def _mlp(x, g1, wu, wd, g2):
    m, d = x.shape
    ff = wu.shape[1]
    tm = _pick(m, (640, 512, 256, 128, 64, 32, 16, 8))
    tf = _pick(ff, (512, 256, 128))
    return pl.pallas_call(
        _mlp_kernel,
        grid=(m // tm, ff // tf),
        in_specs=[pl.BlockSpec((tm, d), lambda i, f: (i, 0)),
                  pl.BlockSpec((1, d), lambda i, f: (0, 0)),
                  pl.BlockSpec((d, tf), lambda i, f: (0, f)),
                  pl.BlockSpec((tf, d), lambda i, f: (f, 0)),
                  pl.BlockSpec((1, d), lambda i, f: (0, 0))],
        out_specs=pl.BlockSpec((tm, d), lambda i, f: (i, 0)),
        out_shape=jax.ShapeDtypeStruct((m, d), F32),
        scratch_shapes=[pltpu.VMEM((tm, d), BF16), pltpu.VMEM((tm, d), F32)],
        compiler_params=_cparams(("parallel", "arbitrary")),
        name="mlp",
    )(x, g1, wu, wd, g2)


def _fox_prep_kernel(x_ref, s_ref, bf_ref, qa_ref, ka_ref, v_ref, logf_ref, carry_ref, *, tb):
    @pl.when(pl.program_id(1) == 0)
    def _():
        carry_ref[...] = jnp.zeros_like(carry_ref)

    logf = _log_sigmoid(s_ref[...] + bf_ref[...])
    logf_ref[...] = logf
    tri = (_iota((tb, tb), 1) <= _iota((tb, tb), 0)).astype(F32)
    c = _dot(tri, logf, precision=HI) + carry_ref[...]
    carry_ref[...] = c[tb - 1:tb, :]
    hi = c.astype(BF16).astype(F32)
    r1 = c - hi
    mid = r1.astype(BF16).astype(F32)
    lo = r1 - mid
    lane = _iota((tb, HEAD_DIM), 1)
    q_tail = jnp.where(lane < 3, 1.0, 0.0).astype(BF16)
    x = x_ref[...]
    w = GROUP_WIDTH
    for h in range(N_HEADS):
        sl = slice(h * HEAD_DIM, (h + 1) * HEAD_DIM)
        qa_ref[h] = jnp.concatenate([(x[:, sl] * ATT_SCALE).astype(BF16), q_tail], axis=1)
        k_tail = jnp.where(lane == 0, -hi[:, h:h + 1],
                           jnp.where(lane == 1, -mid[:, h:h + 1],
                                     jnp.where(lane == 2, -lo[:, h:h + 1], 0.0)))
        ka_ref[h] = jnp.concatenate([x[:, w + h * HEAD_DIM:w + (h + 1) * HEAD_DIM].astype(BF16),
                                     k_tail.astype(BF16)], axis=1)
    for hp in range(N_HEADS // 2):
        v_ref[hp] = x[:, 2 * w + hp * 128:2 * w + (hp + 1) * 128].astype(BF16)


def _fox_prep(proj, bf_row, b, t):
    tb = _pick(t, (512, 256, 128, 64))
    nt = t // tb
    w = GROUP_WIDTH
    aug = jax.ShapeDtypeStruct((b, N_HEADS, t, 128), BF16)
    return pl.pallas_call(
        functools.partial(_fox_prep_kernel, tb=tb),
        grid=(b, nt),
        in_specs=[pl.BlockSpec((tb, 3 * w), lambda i, j: (i * nt + j, C_FQ // (3 * w))),
                  pl.BlockSpec((tb, 128), lambda i, j: (i * nt + j, C_SMALL // 128)),
                  pl.BlockSpec((1, 128), lambda i, j: (0, 0))],
        out_specs=[pl.BlockSpec((None, N_HEADS, tb, 128), lambda i, j: (i, 0, j, 0)),
                   pl.BlockSpec((None, N_HEADS, tb, 128), lambda i, j: (i, 0, j, 0)),
                   pl.BlockSpec((None, N_HEADS // 2, tb, 128), lambda i, j: (i, 0, j, 0)),
                   pl.BlockSpec((tb, 128), lambda i, j: (i * nt + j, 0))],
        out_shape=[aug, aug, jax.ShapeDtypeStruct((b, N_HEADS // 2, t, 128), BF16),
                   jax.ShapeDtypeStruct((b * t, 128), F32)],
        scratch_shapes=[pltpu.VMEM((1, 128), F32)],
        compiler_params=_cparams(("parallel", "arbitrary")),
        name="fox_prep",
    )(proj, proj, bf_row)


def _fox_flash_kernel(qi_ref, ki_ref, q_ref, k_ref, v_ref, o_ref, m_ref, l_ref, acc_ref, *, nt):
    step_id = pl.program_id(2)
    qi = qi_ref[step_id]
    ki = ki_ref[step_id]

    @pl.when(ki == 0)
    def _():
        m_ref[...] = jnp.full_like(m_ref, NEG)
        l_ref[...] = jnp.zeros_like(l_ref)
        acc_ref[...] = jnp.zeros_like(acc_ref)

    def step(diag):
        v2 = v_ref[...]
        for j in range(2):
            s = _dot_nt(k_ref[j], q_ref[j])
            if diag:
                s = jnp.where(_iota((nt, nt), 0) <= _iota((nt, nt), 1), s, NEG)
            m_prev = m_ref[j]
            m_new = jnp.maximum(m_prev, jnp.max(s, axis=0, keepdims=True))
            alpha = jnp.exp(m_prev - m_new)
            p = jnp.exp(s - m_new)
            l_ref[j] = alpha * l_ref[j] + jnp.sum(p, axis=0, keepdims=True)
            pv = _dot_tn(v2, p.astype(BF16))[j * HEAD_DIM:(j + 1) * HEAD_DIM, :]
            acc_ref[j] = alpha * acc_ref[j] + pv
            m_ref[j] = m_new

    @pl.when(ki < qi)
    def _():
        step(False)

    @pl.when(ki == qi)
    def _():
        step(True)
        o_ref[...] = jnp.concatenate([acc_ref[j] / l_ref[j] for j in range(2)], axis=0).T


def _causal_pairs(n):
    qs = [q for q in range(n) for _ in range(q + 1)]
    ks = [k for q in range(n) for k in range(q + 1)]
    return jnp.asarray(qs, jnp.int32), jnp.asarray(ks, jnp.int32)


def _fox_flash(qa, ka, v, b, t):
    nt = _pick(t, (1024, 512, 256, 128))
    n = t // nt
    qi, ki = _causal_pairs(n)
    hp = N_HEADS // 2
    return pl.pallas_call(
        functools.partial(_fox_flash_kernel, nt=nt),
        grid_spec=pltpu.PrefetchScalarGridSpec(
            num_scalar_prefetch=2, grid=(b, hp, qi.shape[0]),
            in_specs=[pl.BlockSpec((None, 2, nt, 128), lambda i, h, s, qr, kr: (i, h, qr[s], 0)),
                      pl.BlockSpec((None, 2, nt, 128), lambda i, h, s, qr, kr: (i, h, kr[s], 0)),
                      pl.BlockSpec((None, None, nt, 128), lambda i, h, s, qr, kr: (i, h, kr[s], 0))],
            out_specs=pl.BlockSpec((nt, 128), lambda i, h, s, qr, kr: (i * n + qr[s], h)),
            scratch_shapes=[pltpu.VMEM((2, 1, nt), F32), pltpu.VMEM((2, 1, nt), F32),
                            pltpu.VMEM((2, HEAD_DIM, nt), F32)]),
        out_shape=jax.ShapeDtypeStruct((b * t, GROUP_WIDTH), F32),
        compiler_params=_cparams(("parallel", "parallel", "arbitrary")),
        name="fox_flash",
    )(qi, ki, qa, ka, v)


def _lru_gates(xc, wr, br, wi, bi, lam):
    xb = xc.astype(BF16)
    r = jax.nn.sigmoid(_dot(xb, wr) + br)
    ig = jax.nn.sigmoid(_dot(xb, wi) + bi)
    log_a = -LRU_C * r * _softplus(-lam)
    a = jnp.exp(log_a)
    th = jnp.tanh(log_a)
    u = jnp.sqrt(-2.0 * th / (1.0 - th)) * (ig * xc)
    return a, u


def _conv4(w_ref, x0, x1, x2, x3, cols=slice(None)):
    y = 0.0 + w_ref[0:1, cols] * x0
    y = y + w_ref[1:2, cols] * x1
    y = y + w_ref[2:3, cols] * x2
    return y + w_ref[3:4, cols] * x3


def _lru_prompt_kernel(x_ref, g_ref, cw_ref, cb_ref, wr_ref, br_ref, wi_ref, bi_ref, lam_ref,
                       y_ref, hfin_ref, xbuf, a_s, u_s, hs, h_s, *, tb):
    @pl.when(pl.program_id(1) == 0)
    def _():
        xbuf[0:8, :] = jnp.zeros((8, GROUP_WIDTH), F32)
        h_s[...] = jnp.zeros_like(h_s)

    xbuf[8:8 + tb, :] = x_ref[...]
    xc = _conv4(cw_ref, xbuf[5:5 + tb, :], xbuf[6:6 + tb, :], xbuf[7:7 + tb, :], xbuf[8:8 + tb, :])
    xc = xc + cb_ref[...]
    xbuf[0:8, :] = xbuf[tb:tb + 8, :]
    a, u = _lru_gates(xc, wr_ref[...], br_ref[...], wi_ref[...], bi_ref[...], lam_ref[...])
    a_s[...] = a
    u_s[...] = u

    def body(t, h):
        h = a_s[pl.ds(t, 1), :] * h + u_s[pl.ds(t, 1), :]
        hs[pl.ds(t, 1), :] = h
        return h

    h = lax.fori_loop(0, tb, body, h_s[...], unroll=8)
    h_s[...] = h
    y_ref[...] = hs[...] * jax.nn.gelu(g_ref[...])
    hfin_ref[...] = jnp.broadcast_to(h, (8, GROUP_WIDTH))


def _lru_prompt(proj, prm, b, t):
    tb = _pick(t, (512, 256, 128, 64))
    nt = t // tb
    w = GROUP_WIDTH
    full = lambda shape: pl.BlockSpec(shape, lambda i, j: (0,) * len(shape))
    return pl.pallas_call(
        functools.partial(_lru_prompt_kernel, tb=tb),
        grid=(b, nt),
        in_specs=[pl.BlockSpec((tb, w), lambda i, j: (i * nt + j, C_LX // w)),
                  pl.BlockSpec((tb, w), lambda i, j: (i * nt + j, C_LG // w)),
                  full((CONV_W, w)), full((1, w)), full((w, w)), full((1, w)), full((w, w)),
                  full((1, w)), full((1, w))],
        out_specs=[pl.BlockSpec((tb, w), lambda i, j: (i * nt + j, 0)),
                   pl.BlockSpec((None, 8, w), lambda i, j: (i, 0, 0))],
        out_shape=[jax.ShapeDtypeStruct((b * t, w), F32), jax.ShapeDtypeStruct((b, 8, w), F32)],
        scratch_shapes=[pltpu.VMEM((tb + 8, w), F32), pltpu.VMEM((tb, w), F32), pltpu.VMEM((tb, w), F32),
                        pltpu.VMEM((tb, w), F32), pltpu.VMEM((1, w), F32)],
        compiler_params=_cparams(("parallel", "arbitrary")),
        name="lru_prompt",
    )(proj, proj, prm['lru_conv_w'], prm['lru_conv_b'], prm['lru_wr'], prm['lru_b_r'], prm['lru_wi'],
      prm['lru_b_i'], prm['lru_lambda'])


def _lru_step_kernel(x_ref, g_ref, buf_ref, h0_ref, cw_ref, cb_ref, wr_ref, br_ref, wi_ref, bi_ref,
                     lam_ref, y_ref, h_ref):
    xc = _conv4(cw_ref, buf_ref[0], buf_ref[1], buf_ref[2], x_ref[...]) + cb_ref[...]
    a, u = _lru_gates(xc, wr_ref[...], br_ref[...], wi_ref[...], bi_ref[...], lam_ref[...])
    h = a * h0_ref[...] + u
    h_ref[...] = h
    y_ref[...] = h * jax.nn.gelu(g_ref[...])


def _lru_step(x, g, buf, h0, prm):
    n, w = x.shape
    out = jax.ShapeDtypeStruct((n, w), F32)
    return pl.pallas_call(
        _lru_step_kernel, out_shape=[out, out], name="lru_step",
        compiler_params=pltpu.CompilerParams(vmem_limit_bytes=VMEM_LIMIT),
    )(x, g, buf, h0, prm['lru_conv_w'], prm['lru_conv_b'], prm['lru_wr'], prm['lru_b_r'], prm['lru_wi'],
      prm['lru_b_i'], prm['lru_lambda'])


def _nsa_cmp_kernel(kv_ref, w_ref, o_ref, *, tb):
    x = kv_ref[...].reshape(tb // NSA_BLOCK, NSA_BLOCK, 256)
    o_ref[...] = jnp.sum(x * w_ref[...][None], axis=1)


def _nsa_cmp(proj, wcmp, b, t):
    tb = _pick(t, (512,))
    nt = t // tb
    nb = tb // NSA_BLOCK
    return pl.pallas_call(
        functools.partial(_nsa_cmp_kernel, tb=tb),
        grid=(b, nt),
        in_specs=[pl.BlockSpec((tb, 256), lambda i, j: (i * nt + j, C_NKV // 256)),
                  pl.BlockSpec((NSA_BLOCK, 256), lambda i, j: (0, 0))],
        out_specs=pl.BlockSpec((None, nb, 256), lambda i, j: (i, j, 0)),
        out_shape=jax.ShapeDtypeStruct((b, t // NSA_BLOCK, 256), F32),
        compiler_params=_cparams(("parallel", "parallel")),
        name="nsa_cmp",
    )(proj, wcmp)


def _nsa_prompt_kernel(qi_ref, ki_ref, q_ref, ks_ref, vs_ref, kw_ref, vw_ref, cmp_ref, sm_ref, o_ref,
                       qs, oc, sel, m_s, l_s, acc_s, m_w, l_w, acc_w, tile_hit, *, nq, nk, nb):
    step_id = pl.program_id(1)
    qi = qi_ref[step_id]
    ki = ki_ref[step_id]
    q0 = qi * nq
    k0 = ki * nk
    k_last = (q0 + nq - 1) // nk
    n_sel = min(NSA_TOPN, nb)
    rows4 = NSA_GROUP * nq
    hd = HEAD_DIM
    head_in_group = _iota((rows4, 1), 0) // nq

    def slope_col(g):
        return jnp.exp2(-(NSA_GROUP * g + head_in_group + 1).astype(F32) * (8.0 / N_HEADS))

    @pl.when(ki == 0)
    def _():
        q = q_ref[...] * ATT_SCALE
        cmp = cmp_ref[...]
        tpos = q0 + (_iota((rows4, nb), 0) % nq)
        blk = _iota((rows4, nb), 1)
        dist_c = tpos - ((blk + 1) * NSA_BLOCK - 1)
        valid = dist_c >= 0
        dist_cf = dist_c.astype(F32)
        lane = _iota((rows4, hd), 1)
        scores = []
        for g in range(KVH_NSA):
            qg = jnp.concatenate([q[:, (NSA_GROUP * g + i) * hd:(NSA_GROUP * g + i + 1) * hd]
                                  for i in range(NSA_GROUP)], axis=0)
            tail = jnp.where(lane < 2, slope_col(g), 0.0)
            qs[g] = jnp.concatenate([qg, tail] if g == 0 else [tail, qg], axis=1).astype(BF16)
            kc = cmp[:, g * hd:(g + 1) * hd].astype(BF16)
            vc = cmp[:, 128 + g * hd:128 + (g + 1) * hd].astype(BF16)
            s = _dot_nt(qg.astype(BF16), kc) - slope_col(g) * dist_cf
            s = jnp.where(valid, s, NEG)
            m = jnp.max(s, axis=-1, keepdims=True)
            e = jnp.where(valid, jnp.exp(s - m), 0.0)
            l = jnp.sum(e, axis=-1, keepdims=True)
            p = e / jnp.where(l > 0.0, l, 1.0)
            oc[g] = _dot(p.astype(BF16), vc)
            imp = p[0:nq]
            for i in range(1, NSA_GROUP):
                imp = imp + p[i * nq:(i + 1) * nq]
            scores.append(imp)
        imp2 = jnp.concatenate(scores, axis=1)
        lane2 = _iota((nq, KVH_NSA * nb), 1)
        blk2 = lane2 % nb
        cur = (q0 + _iota((nq, KVH_NSA * nb), 0)) // NSA_BLOCK
        forced = (blk2 == 0) | (blk2 >= cur - 1)
        score = jnp.where(blk2 <= cur, jnp.where(forced, FORCED_SCORE, imp2), -1.0)
        score_t = score.T
        rg = 8 if nb % 8 == 0 else nb
        for g in range(KVH_NSA):
            s_g = score_t[g * nb:(g + 1) * nb, :]
            for j0 in range(0, nb, rg):
                sub = s_g[j0:j0 + rg, :]
                n_idx = j0 + _iota((rg, nq), 0)
                cnt = jnp.zeros((rg, nq), F32)
                for mm in range(nb):
                    r = s_g[mm:mm + 1, :]
                    ge = jnp.where(r >= sub, 1.0, 0.0)
                    gt = jnp.where(r > sub, 1.0, 0.0)
                    if j0 > mm:
                        cnt = cnt + ge
                    elif j0 + rg - 1 <= mm:
                        cnt = cnt + gt
                    else:
                        cnt = cnt + jnp.where(n_idx > mm, ge, gt)
                sel[g * nb + j0:g * nb + j0 + rg, :] = jnp.where(
                    cnt < n_sel, jnp.where(sub >= 0.0, 1.0, 0.0), 0.0)
        bpt = nk // NSA_BLOCK
        for kt in range(nb // bpt):
            hit = jnp.maximum(sel[kt * bpt:(kt + 1) * bpt, :], sel[nb + kt * bpt:nb + (kt + 1) * bpt, :])
            tile_hit[kt] = (jnp.max(hit) > 0.5).astype(jnp.int32)
        for ref in (m_s, m_w):
            ref[...] = jnp.full_like(ref, NEG)
        for ref in (l_s, acc_s, l_w, acc_w):
            ref[...] = jnp.zeros_like(ref)

    def aug_keys(k_ref):
        k = k_ref[...]
        c = _iota((nk, 2 * hd), 0)
        lane = _iota((nk, 2 * hd), 1)
        c_hi = ((c // 256) * 256).astype(F32)
        c_lo = (c % 256).astype(F32)
        k_g0 = jnp.where(lane < hd, k, jnp.where(lane == hd, c_hi, jnp.where(lane == hd + 1, c_lo, 0.0)))
        k_g1 = jnp.where(lane >= hd, k, jnp.where(lane == 0, c_hi, jnp.where(lane == 1, c_lo, 0.0)))
        return k_g0.astype(BF16), k_g1.astype(BF16)

    def attend(g, k_aug, v_bf, bias, m_ref, l_ref, acc_ref):
        head = NSA_GROUP * g + _iota((1, rows4), 1) // nq
        shift = jnp.exp2(-(head + 1).astype(F32) * (8.0 / N_HEADS)) * (k0 - q0).astype(F32)
        s = _dot_nt(k_aug, qs[g])
        s = jnp.concatenate([s[:, i * nq:(i + 1) * nq] + bias for i in range(NSA_GROUP)], axis=1)
        m_prev = m_ref[g]
        m_new = jnp.maximum(m_prev, jnp.max(s, axis=0, keepdims=True) + shift)
        alpha = jnp.exp(m_prev - m_new)
        p = jnp.exp(s - (m_new - shift))
        l_ref[g] = alpha * l_ref[g] + jnp.sum(p, axis=0, keepdims=True)
        acc_ref[g] = alpha * acc_ref[g] + _dot_tn(v_bf, p.astype(BF16))
        m_ref[g] = m_new

    def dist_tile():
        return (q0 + _iota((nk, nq), 1)) - (k0 + _iota((nk, nq), 0))

    @pl.when((ki <= k_last) & (tile_hit[ki] > 0))
    def _():
        causal = dist_tile() >= 0
        k_aug = aug_keys(ks_ref)
        v_bf = vs_ref[...].astype(BF16)
        sel_bf = sel[...].astype(BF16)
        key_blk = (k0 + _iota((nk, KVH_NSA * nb), 0)) // NSA_BLOCK
        col = _iota((nk, KVH_NSA * nb), 1)
        for g in range(KVH_NSA):
            onehot = (col == g * nb + key_blk).astype(BF16)
            picked = _dot(onehot, sel_bf) > 0.5
            bias = jnp.where(causal, jnp.where(picked, 0.0, NEG), NEG)
            attend(g, k_aug[g], v_bf, bias, m_s, l_s, acc_s)

    @pl.when((ki <= k_last) & (ki >= k_last - (NSA_WINDOW + nk - 1) // nk))
    def _():
        dist = dist_tile()
        bias = jnp.where(dist >= 0, jnp.where(dist < NSA_WINDOW, 0.0, NEG), NEG)
        k_aug = aug_keys(kw_ref)
        v_bf = vw_ref[...].astype(BF16)
        for g in range(KVH_NSA):
            attend(g, k_aug[g], v_bf, bias, m_w, l_w, acc_w)

    @pl.when(ki == k_last)
    def _():
        gates = jax.nn.sigmoid(sm_ref[...])
        outs = []
        for g in range(KVH_NSA):
            o_s = (acc_s[g] / l_s[g]).T[:, g * hd:(g + 1) * hd]
            o_w = (acc_w[g] / l_w[g]).T[:, g * hd:(g + 1) * hd]
            o_c = oc[g]
            for i in range(NSA_GROUP):
                rows = slice(i * nq, (i + 1) * nq)
                c = L_NG + 3 * (NSA_GROUP * g + i)
                o = gates[:, c:c + 1] * o_c[rows]
                outs.append(o + (gates[:, c + 1:c + 2] * o_s[rows] + gates[:, c + 2:c + 3] * o_w[rows]))
        o_ref[...] = jnp.concatenate(outs, axis=1)


def _nsa_prompt(proj, cmp, b, t):
    nq = _pick(t, (256, 128))
    nk = _pick(t, (512, 256, 128))
    n_q = t // nq
    n_k = t // nk
    nb = t // NSA_BLOCK
    last = lambda q: (q * nq + nq - 1) // nk
    win_tiles = (NSA_WINDOW + nk - 1) // nk
    pairs = [(q, k) for q in range(n_q) for k in range(last(q) + 1)]
    qi = jnp.asarray([p[0] for p in pairs], jnp.int32)
    ki = jnp.asarray([p[1] for p in pairs], jnp.int32)
    rows4 = NSA_GROUP * nq

    def widx(q, k):
        kl = (q * nq + nq - 1) // nk
        return jnp.clip(k, jnp.maximum(kl - win_tiles, 0), kl)

    return pl.pallas_call(
        functools.partial(_nsa_prompt_kernel, nq=nq, nk=nk, nb=nb),
        grid_spec=pltpu.PrefetchScalarGridSpec(
            num_scalar_prefetch=2, grid=(b, len(pairs)),
            in_specs=[pl.BlockSpec((nq, 512), lambda i, s, qr, kr: (i * n_q + qr[s], C_NQ // 512)),
                      pl.BlockSpec((nk, 128), lambda i, s, qr, kr: (i * n_k + kr[s], C_NKV // 128 + 2)),
                      pl.BlockSpec((nk, 128), lambda i, s, qr, kr: (i * n_k + kr[s], C_NKV // 128 + 3)),
                      pl.BlockSpec((nk, 128), lambda i, s, qr, kr: (i * n_k + widx(qr[s], kr[s]), C_NWIN // 128)),
                      pl.BlockSpec((nk, 128), lambda i, s, qr, kr: (i * n_k + widx(qr[s], kr[s]), C_NWIN // 128 + 1)),
                      pl.BlockSpec((None, nb, 256), lambda i, s, qr, kr: (i, 0, 0)),
                      pl.BlockSpec((nq, 128), lambda i, s, qr, kr: (i * n_q + qr[s], C_SMALL // 128))],
            out_specs=pl.BlockSpec((nq, 512), lambda i, s, qr, kr: (i * n_q + qr[s], 0)),
            scratch_shapes=[pltpu.VMEM((KVH_NSA, rows4, 2 * HEAD_DIM), BF16),
                            pltpu.VMEM((KVH_NSA, rows4, HEAD_DIM), F32),
                            pltpu.VMEM((KVH_NSA * nb, nq), F32),
                            pltpu.VMEM((KVH_NSA, 1, rows4), F32), pltpu.VMEM((KVH_NSA, 1, rows4), F32),
                            pltpu.VMEM((KVH_NSA, 2 * HEAD_DIM, rows4), F32),
                            pltpu.VMEM((KVH_NSA, 1, rows4), F32), pltpu.VMEM((KVH_NSA, 1, rows4), F32),
                            pltpu.VMEM((KVH_NSA, 2 * HEAD_DIM, rows4), F32),
                            pltpu.SMEM((n_k,), jnp.int32)]),
        out_shape=jax.ShapeDtypeStruct((b * t, GROUP_WIDTH), F32),
        compiler_params=_cparams(("parallel", "arbitrary")),
        name="nsa_prompt",
    )(qi, ki, proj, proj, proj, proj, proj, cmp, proj)


def _col_to_row(col, eye):
    return jnp.sum(jnp.where(eye, col, 0.0), axis=0, keepdims=True)


def _row_to_col(row, eye):
    return jnp.sum(jnp.where(eye, row, 0.0), axis=1, keepdims=True)


def _gdn_qkv(act_q, act_k, h):
    sl = slice(h * GDN_DK, (h + 1) * GDN_DK)
    q = act_q[:, sl]
    k = act_k[:, sl]
    q = q * lax.rsqrt(jnp.sum(q * q, axis=-1, keepdims=True) + 1e-6) * (GDN_DK ** -0.5)
    k = k * lax.rsqrt(jnp.sum(k * k, axis=-1, keepdims=True) + 1e-6)
    return q, k


def _gdn_out(o, nw, gate):
    return _rms(o, nw) * (gate * jax.nn.sigmoid(gate))


def _silu(x):
    return x * jax.nn.sigmoid(x)


def _gdn_pre_kernel(q_ref, k_ref, v_ref, hq_ref, hk_ref, hv_ref, sm_ref, cw_ref, al_ref, dt_ref,
                    u_ref, w_ref, qe_ref, kd_ref, qk_ref, egl_ref, xbuf, act, *, rows):
    c = GDN_CHUNK
    w5 = GROUP_WIDTH
    first = pl.program_id(1) == 0
    for seg, (ref, halo) in enumerate(((q_ref, hq_ref), (k_ref, hk_ref), (v_ref, hv_ref))):
        xbuf[seg, 0:8, :] = jnp.where(first, 0.0, halo[...])
        xbuf[seg, 8:8 + rows, :] = ref[...]
        cols = slice(seg * w5, (seg + 1) * w5)
        act[seg] = _silu(_conv4(cw_ref, xbuf[seg, 5:5 + rows, :], xbuf[seg, 6:6 + rows, :],
                                xbuf[seg, 7:7 + rows, :], xbuf[seg, 8:8 + rows, :], cols))
    ri = _iota((c, c), 0)
    ci = _iota((c, c), 1)
    incl = ri >= ci
    strict = ri > ci
    eye = ri == ci
    incl_f = incl.astype(F32)
    hc = H_GDN * c
    eye_bd = (_iota((hc, hc), 0) == _iota((hc, hc), 1)).astype(F32)

    def chunk(idx, carry):
        r0 = pl.multiple_of(idx * c, c)
        rs = pl.ds(r0, c)
        sm = sm_ref[rs, :]
        beta_all = jax.nn.sigmoid(sm)
        la_all = -jnp.exp(al_ref[...]) * _softplus(sm + dt_ref[...])
        g_all = _dot(incl_f, la_all, precision=HI)
        egl_ref[rs, :] = jnp.broadcast_to(jnp.exp(g_all[c - 1:c, :]), (c, 128))
        act_q = act[0, rs, :]
        act_k = act[1, rs, :]
        act_v = act[2, rs, :]
        qks, a_rows, vbs, kbes = [], [], [], []
        zero = jnp.zeros((c, c), F32)
        for h in range(H_GDN):
            sl = slice(h * GDN_DK, (h + 1) * GDN_DK)
            q, k = _gdn_qkv(act_q, act_k, h)
            v = act_v[:, sl]
            beta = beta_all[:, L_DB + h:L_DB + h + 1]
            g = g_all[:, L_DA + h:L_DA + h + 1]
            g_row = _col_to_row(g, eye)
            decay = jnp.exp(jnp.where(incl, g - g_row, -jnp.inf))
            kb = k * beta
            a = jnp.where(strict, _dot_nt(kb, k) * decay, 0.0)
            a_rows.append(jnp.concatenate([a if j == h else zero for j in range(H_GDN)], axis=1))
            eg = jnp.exp(g)
            vbs.append(v * beta)
            kbes.append(kb * eg)
            qe_ref[rs, sl] = q * eg
            kd_ref[rs, sl] = k * jnp.exp(g[c - 1:c, :] - g)
            qks.append(_dot_nt(q, k) * decay)
        qk_ref[rs, :] = jnp.concatenate(qks, axis=1)
        a_bd = jnp.concatenate(a_rows, axis=0)
        tm = eye_bd - a_bd
        pw = _dot3(a_bd, a_bd)
        for it in range(5):
            tm = tm + _dot3(tm, pw)
            if it < 4:
                pw = _dot3(pw, pw)
        u_all = _dot(tm, jnp.concatenate(vbs, axis=0))
        w_all = _dot(tm, jnp.concatenate(kbes, axis=0))
        for h in range(H_GDN):
            sl = slice(h * GDN_DK, (h + 1) * GDN_DK)
            u_ref[rs, sl] = u_all[h * c:(h + 1) * c, :]
            w_ref[rs, sl] = w_all[h * c:(h + 1) * c, :]
        return carry

    lax.fori_loop(0, rows // c, chunk, 0, unroll=2)


def _gdn_seq_kernel(u_ref, w_ref, qe_ref, kd_ref, qk_ref, egl_ref, g_ref, nw_ref, o_ref, s_ref, *, rows):
    c = GDN_CHUNK

    @pl.when(pl.program_id(0) == 0)
    def _():
        s_ref[...] = jnp.zeros_like(s_ref)

    nw = nw_ref[...]

    def chunk(idx, carry):
        r0 = pl.multiple_of(idx * c, c)
        rs = pl.ds(r0, c)
        for bi in range(u_ref.shape[0]):
            egl = egl_ref[bi, pl.ds(r0, 1), :]
            for h in range(H_GDN):
                sl = slice(h * GDN_DK, (h + 1) * GDN_DK)
                s0 = s_ref[bi, h]
                u_new = u_ref[bi, rs, sl] - _dot(w_ref[bi, rs, sl], s0)
                o = _dot(qe_ref[bi, rs, sl], s0) + _dot(qk_ref[bi, rs, h * c:(h + 1) * c], u_new)
                s_ref[bi, h] = s0 * egl[:, L_DA + h:L_DA + h + 1] + _dot_tn(kd_ref[bi, rs, sl], u_new)
                o_ref[bi, rs, sl] = _gdn_out(o, nw, g_ref[bi, rs, sl])
        return carry

    lax.fori_loop(0, rows // c, chunk, 0)


def _gdn_prompt(proj, prm, b, t):
    c = GDN_CHUNK
    w = GROUP_WIDTH
    rows = _pick(t, (512, 256, 128, 64))
    nr = t // rows
    full = lambda shape: pl.BlockSpec(shape, lambda i, j: (0,) * len(shape))
    seg = lambda col: pl.BlockSpec((rows, w), lambda i, j: (i * nr + j, col // w))
    halo = lambda col: pl.BlockSpec(
        (8, w), lambda i, j: (jnp.maximum((i * nr + j) * (rows // 8) - 1, 0), col // w))
    blk = lambda width: pl.BlockSpec((rows, width), lambda i, j: (i * nr + j, 0))
    f32 = lambda width: jax.ShapeDtypeStruct((b * t, width), F32)
    u, wk, qe, kd, qk, egl = pl.pallas_call(
        functools.partial(_gdn_pre_kernel, rows=rows),
        grid=(b, nr),
        in_specs=[seg(C_DQKV), seg(C_DQKV + w), seg(C_DQKV + 2 * w),
                  halo(C_DQKV), halo(C_DQKV + w), halo(C_DQKV + 2 * w),
                  pl.BlockSpec((rows, 128), lambda i, j: (i * nr + j, C_SMALL // 128)),
                  full((CONV_W, 3 * w)), full((1, 128)), full((1, 128))],
        out_specs=[blk(w), blk(w), blk(w), blk(w), blk(H_GDN * c), blk(128)],
        out_shape=[f32(w), f32(w), f32(w), f32(w), f32(H_GDN * c), f32(128)],
        scratch_shapes=[pltpu.VMEM((3, rows + 8, w), F32), pltpu.VMEM((3, rows, w), F32)],
        compiler_params=_cparams(("parallel", "parallel")),
        name="gdn_pre",
    )(proj, proj, proj, proj, proj, proj, proj, prm['gdn_conv_w'], prm['gdn_alog_row'], prm['gdn_dt_row'])
    bblk = lambda width, col=0: pl.BlockSpec((b, rows, width), lambda j: (0, j, col))
    per_seq = lambda a: a.reshape(b, t, a.shape[-1])
    o, s_fin = pl.pallas_call(
        functools.partial(_gdn_seq_kernel, rows=rows),
        grid=(nr,),
        in_specs=[bblk(w), bblk(w), bblk(w), bblk(w), bblk(H_GDN * c), bblk(128), bblk(w, C_DG // w),
                  pl.BlockSpec((1, GDN_DK), lambda j: (0, 0))],
        out_specs=[bblk(w), pl.BlockSpec((b, H_GDN, GDN_DK, GDN_DK), lambda j: (0, 0, 0, 0))],
        out_shape=[jax.ShapeDtypeStruct((b, t, w), F32),
                   jax.ShapeDtypeStruct((b, H_GDN, GDN_DK, GDN_DK), F32)],
        compiler_params=_cparams(("arbitrary",)),
        name="gdn_seq",
    )(per_seq(u), per_seq(wk), per_seq(qe), per_seq(kd), per_seq(qk), per_seq(egl), per_seq(proj),
      prm['gdn_norm_w'])
    return o.reshape(b * t, w), s_fin


def _gdn_step_kernel(x_ref, g_ref, sm_ref, buf_ref, s0_ref, cw_ref, al_ref, dt_ref, nw_ref,
                     o_ref, s_ref):
    eye = _iota((GDN_DK, GDN_DK), 0) == _iota((GDN_DK, GDN_DK), 1)
    w = GROUP_WIDTH
    for smp in range(x_ref.shape[0]):
        buf = buf_ref[smp]
        act = _silu(_conv4(cw_ref, buf[0:1], buf[1:2], buf[2:3], x_ref[smp]))
        sm = sm_ref[smp]
        beta_all = jax.nn.sigmoid(sm)
        alpha_all = jnp.exp(-jnp.exp(al_ref[...]) * _softplus(sm + dt_ref[...]))
        gate = g_ref[smp]
        outs = []
        for h in range(H_GDN):
            sl = slice(h * GDN_DK, (h + 1) * GDN_DK)
            q, k = _gdn_qkv(act[:, 0:w], act[:, w:2 * w], h)
            v = act[:, 2 * w + h * GDN_DK:2 * w + (h + 1) * GDN_DK]
            beta = beta_all[:, L_DB + h:L_DB + h + 1]
            alpha = alpha_all[:, L_DA + h:L_DA + h + 1]
            k_col = _row_to_col(k, eye)
            q_col = _row_to_col(q, eye)
            s0 = s0_ref[smp, h]
            ks = jnp.sum(s0 * k_col, axis=0, keepdims=True)
            u = beta * v - (beta * alpha) * ks
            s1 = alpha * s0 + k_col * u
            s_ref[smp, h] = s1
            o = jnp.sum(s1 * q_col, axis=0, keepdims=True)
            outs.append(_gdn_out(o, nw_ref[...], gate[:, sl]))
        o_ref[smp] = jnp.concatenate(outs, axis=1)


def _gdn_step(x, gate, small, conv_buf, s0, prm):
    n = x.shape[0]
    w = GROUP_WIDTH
    ns = 1
    full = lambda shape: pl.BlockSpec(shape, lambda i: (0,) * len(shape))
    return pl.pallas_call(
        _gdn_step_kernel,
        grid=(n // ns,),
        in_specs=[pl.BlockSpec((ns, 1, 3 * w), lambda i: (i, 0, 0)),
                  pl.BlockSpec((ns, 1, w), lambda i: (i, 0, 0)),
                  pl.BlockSpec((ns, 1, 128), lambda i: (i, 0, 0)),
                  pl.BlockSpec((ns, CONV_W - 1, 3 * w), lambda i: (i, 0, 0)),
                  pl.BlockSpec((ns, H_GDN, GDN_DK, GDN_DK), lambda i: (i, 0, 0, 0)),
                  full((CONV_W, 3 * w)), full((1, 128)), full((1, 128)), full((1, GDN_DK))],
        out_specs=[pl.BlockSpec((ns, 1, w), lambda i: (i, 0, 0)),
                   pl.BlockSpec((ns, H_GDN, GDN_DK, GDN_DK), lambda i: (i, 0, 0, 0))],
        out_shape=[jax.ShapeDtypeStruct((n, 1, w), F32),
                   jax.ShapeDtypeStruct((n, H_GDN, GDN_DK, GDN_DK), F32)],
        compiler_params=_cparams(("parallel",)),
        name="gdn_step",
    )(x, gate, small, conv_buf, s0, prm['gdn_conv_w'], prm['gdn_alog_row'], prm['gdn_dt_row'],
      prm['gdn_norm_w'])


def _dot_row(row, mat, **kw):
    return _dot(jnp.broadcast_to(row, (8, row.shape[1])), mat, **kw)[0:1]


def _fox_step_kernel(pt_ref, *refs, n_pages):
    kv = refs[:n_pages]
    lf = refs[n_pages:2 * n_pages]
    x_ref, sm_ref, bf_ref, o_ref, lf_out_ref = refs[2 * n_pages:]
    w = GROUP_WIDTH
    pg = PAGE_SIZE
    x = x_ref[...]
    q, kn, vn = x[:, 0:w], x[:, w:2 * w], x[:, 2 * w:3 * w]
    logf_new = _log_sigmoid(sm_ref[...] + bf_ref[...])
    lf_out_ref[...] = logf_new
    head_cols = (_iota((w, N_HEADS), 0) // HEAD_DIM == _iota((w, N_HEADS), 1)).astype(BF16)
    head_rows = _iota((N_HEADS, w), 1) // HEAD_DIM == _iota((N_HEADS, w), 0)
    eye = _iota((w, w), 0) == _iota((w, w), 1)
    qblk = _dot(jnp.where(eye, q, 0.0).astype(BF16), head_cols).astype(BF16)
    tri = (_iota((pg, pg), 1) <= _iota((pg, pg), 0)).astype(F32)
    carry = jnp.zeros((1, N_HEADS), F32)
    cs = []
    for p in range(n_pages):
        c = _dot(tri, lf[p][...], precision=HI) + carry
        carry = c[pg - 1:pg, :]
        cs.append(c)
    ctot = carry + logf_new[:, 0:N_HEADS]
    s_n = _dot_row(kn.astype(BF16), qblk) * ATT_SCALE
    m = s_n
    ss = []
    stride = 2 * N_HEADS
    for p in range(n_pages):
        s = None
        for h in range(N_HEADS):
            k_h = kv[p][pl.ds(h, pg, stride=stride), :].astype(BF16)
            part = _dot(k_h, qblk[h * HEAD_DIM:(h + 1) * HEAD_DIM, :])
            s = part if s is None else s + part
        s = s * ATT_SCALE + (ctot - cs[p])
        ss.append(s)
        m = jnp.maximum(m, jnp.max(s, axis=0, keepdims=True))
    e_n = jnp.exp(s_n - m)
    l = e_n
    acc = [jnp.zeros((8, HEAD_DIM), F32) for _ in range(N_HEADS)]
    for p in range(n_pages):
        e = jnp.exp(ss[p] - m)
        l = l + jnp.sum(e, axis=0, keepdims=True)
        for h in range(N_HEADS):
            v_h = kv[p][pl.ds(N_HEADS + h, pg, stride=stride), :]
            acc[h] = acc[h] + jnp.sum((e[:, h:h + 1] * v_h).reshape(pg // 8, 8, HEAD_DIM), axis=0)
    tot = jnp.concatenate([jnp.sum(a, axis=0, keepdims=True) for a in acc], axis=1)
    tot = tot + _dot_row(e_n.astype(BF16), head_rows.astype(BF16)) * vn
    o_ref[...] = tot * _dot_row(1.0 / l, head_rows.astype(F32), precision=HI)


def _fox_step(pt, kv_pool, lf_pool, x, small, bf_row):
    n, n_pages = pt.shape
    w = GROUP_WIDTH
    page = lambda i, rows, width: pl.BlockSpec((None, rows, width), lambda b, t: (t[b, i], 0, 0))
    in_specs = ([page(i, PAGE_SIZE * 2 * N_HEADS, HEAD_DIM) for i in range(n_pages)] +
                [page(i, PAGE_SIZE, N_HEADS) for i in range(n_pages)] +
                [pl.BlockSpec((None, 1, 3 * w), lambda b, t: (b, 0, 0)),
                 pl.BlockSpec((None, 1, 128), lambda b, t: (b, 0, 0)),
                 pl.BlockSpec((1, 128), lambda b, t: (0, 0))])
    return pl.pallas_call(
        functools.partial(_fox_step_kernel, n_pages=n_pages),
        grid_spec=pltpu.PrefetchScalarGridSpec(
            num_scalar_prefetch=1, grid=(n,), in_specs=in_specs,
            out_specs=[pl.BlockSpec((None, 1, w), lambda b, t: (b, 0, 0)),
                       pl.BlockSpec((None, 1, 128), lambda b, t: (b, 0, 0))]),
        out_shape=[jax.ShapeDtypeStruct((n, 1, w), F32), jax.ShapeDtypeStruct((n, 1, 128), F32)],
        compiler_params=_cparams(("parallel",)),
        name="fox_step",
    )(pt, *([kv_pool] * n_pages), *([lf_pool] * n_pages), x, small, bf_row)


def _softmax_cols(scores):
    m = None
    for s in scores:
        mx = jnp.max(s, axis=0, keepdims=True)
        m = mx if m is None else jnp.maximum(m, mx)
    es = [jnp.exp(s - m) for s in scores]
    l = None
    for e in es:
        sm = jnp.sum(e, axis=0, keepdims=True)
        l = sm if l is None else l + sm
    return es, l


def _nsa_step_kernel(pt_ref, *refs, n_pages):
    pgs = refs[:n_pages]
    win_ref, q_ref, kv_ref, wn_ref, sm_ref, wc_ref, o_ref = refs[n_pages:]
    pg = PAGE_SIZE
    w = GROUP_WIDTH
    kvw = KVH_NSA * HEAD_DIM
    past = n_pages * pg
    nbp = past // NSA_BLOCK
    q = q_ref[...]
    r = _iota((kvw, w), 0)
    c = _iota((kvw, w), 1)
    fold = (c % HEAD_DIM == r % HEAD_DIM) & (c // (NSA_GROUP * HEAD_DIM) == r // HEAD_DIM)
    head_cols = (_iota((w, N_HEADS), 0) // HEAD_DIM == _iota((w, N_HEADS), 1)).astype(BF16)
    qn = _dot(jnp.where(fold, q, 0.0).astype(BF16), head_cols).astype(BF16)
    hl = _iota((1, N_HEADS), 1)
    slopes = jnp.exp2(-(hl + 1).astype(F32) * (8.0 / N_HEADS))
    spread = [_iota((N_HEADS, kvw), 0) == NSA_GROUP * (_iota((N_HEADS, kvw), 1) // HEAD_DIM) + i
              for i in range(NSA_GROUP)]
    spread_bf = [s.astype(BF16) for s in spread]
    spread_f = [s.astype(F32) for s in spread]

    def weighted_rows(es, vals):
        out = []
        for i in range(NSA_GROUP):
            tot = None
            for e, v in zip(es, vals):
                if e.shape[0] == 1:
                    part = _dot_row(e.astype(BF16), spread_bf[i]) * v
                else:
                    part = jnp.sum(_dot(e.astype(BF16), spread_bf[i]) * v, axis=0, keepdims=True)
                tot = part if tot is None else tot + part
            out.append(tot)
        return out

    wc = wc_ref[...]
    kc_rows, vc_rows = [], []
    for p in range(n_pages):
        rows = pgs[p][:, 0:2 * kvw]
        for half in range(pg // NSA_BLOCK):
            cm = jnp.sum(rows[half * NSA_BLOCK:(half + 1) * NSA_BLOCK] * wc, axis=0, keepdims=True)
            kc_rows.append(cm[:, 0:kvw])
            vc_rows.append(cm[:, kvw:2 * kvw])
    kcmp = jnp.concatenate(kc_rows, axis=0)
    vcmp = jnp.concatenate(vc_rows, axis=0)
    blk = _iota((nbp, 1), 0)
    dist_c = (past - ((blk + 1) * NSA_BLOCK - 1)).astype(F32)
    s_c = _dot(kcmp.astype(BF16), qn) * ATT_SCALE - slopes * dist_c
    (e_c,), l_c = _softmax_cols([s_c])
    p_c = e_c / l_c
    acc_c = weighted_rows([p_c], [vcmp])

    cur = past // NSA_BLOCK
    forced = (blk == 0) | (blk >= cur - 1)
    eye_b = _iota((nbp, nbp), 0) == _iota((nbp, nbp), 1)
    lower = _iota((nbp, nbp), 1) < _iota((nbp, nbp), 0)
    n_sel = min(NSA_TOPN, nbp + 1)
    sels = []
    for g in range(KVH_NSA):
        imp = p_c[:, NSA_GROUP * g:NSA_GROUP * g + 1]
        for i in range(1, NSA_GROUP):
            imp = imp + p_c[:, NSA_GROUP * g + i:NSA_GROUP * g + i + 1]
        score = jnp.where(forced, FORCED_SCORE, imp)
        score_row = _col_to_row(score, eye_b)
        beats = (score_row > score) | ((score_row == score) & lower)
        cnt = jnp.sum(jnp.where(beats, 1.0, 0.0), axis=1, keepdims=True)
        cnt = cnt + jnp.where(score < FORCED_SCORE, 1.0, 0.0)
        sels.append(jnp.where(cnt < n_sel, 1.0, 0.0))
    sel8 = jnp.where(hl < NSA_GROUP, sels[0], sels[1])

    kv_new = kv_ref[...]
    first_half = _iota((pg, 1), 0) < NSA_BLOCK
    scores, vals = [], []
    for p in range(n_pages):
        dist = (past - (p * pg + _iota((pg, 1), 0))).astype(F32)
        s = _dot(pgs[p][:, 2 * kvw:3 * kvw].astype(BF16), qn) * ATT_SCALE - slopes * dist
        chosen = jnp.where(first_half, sel8[2 * p:2 * p + 1, :], sel8[2 * p + 1:2 * p + 2, :])
        scores.append(jnp.where(chosen > 0.5, s, NEG))
        vals.append(pgs[p][:, 3 * kvw:4 * kvw])
    scores.append(_dot_row(kv_new[:, 2 * kvw:3 * kvw].astype(BF16), qn) * ATT_SCALE)
    vals.append(kv_new[:, 3 * kvw:4 * kvw])
    e_s, l_s = _softmax_cols(scores)
    acc_s = weighted_rows(e_s, vals)

    nw = win_ref.shape[0]
    wrow = _iota((nw, 1), 0)
    dist_w = nw - wrow
    s_w = _dot(win_ref[:, 0:kvw].astype(BF16), qn) * ATT_SCALE - slopes * dist_w.astype(F32)
    s_w = jnp.where((dist_w < NSA_WINDOW) & (past - dist_w >= 0), s_w, NEG)
    wn = wn_ref[...]
    s_wn = _dot_row(wn[:, 0:kvw].astype(BF16), qn) * ATT_SCALE
    e_w, l_w = _softmax_cols([s_w, s_wn])
    acc_w = weighted_rows(e_w, [win_ref[:, kvw:2 * kvw], wn[:, kvw:2 * kvw]])

    gates = jax.nn.sigmoid(sm_ref[...])
    lane = _iota((128, N_HEADS), 0)
    head = _iota((128, N_HEADS), 1)
    coef = []
    for k in range(3):
        pick = (lane == L_NG + 3 * head + k).astype(F32)
        coef.append(_dot_row(gates, pick, precision=HI))
    coef[1] = coef[1] / l_s
    coef[2] = coef[2] / l_w
    tots = []
    for i in range(NSA_GROUP):
        t = _dot_row(coef[0], spread_f[i], precision=HI) * acc_c[i]
        t = t + (_dot_row(coef[1], spread_f[i], precision=HI) * acc_s[i]
                 + _dot_row(coef[2], spread_f[i], precision=HI) * acc_w[i])
        tots.append(t)
    o_ref[...] = jnp.concatenate([t[:, 0:HEAD_DIM] for t in tots] + [t[:, HEAD_DIM:kvw] for t in tots], axis=1)


def _nsa_step(pt, kv_pool, win, q, kv_new, win_new, small, wcmp):
    n, n_pages = pt.shape
    w = GROUP_WIDTH
    nw = win.shape[1]
    row = lambda width: pl.BlockSpec((None, 1, width), lambda b, t: (b, 0, 0))
    in_specs = ([pl.BlockSpec((None, PAGE_SIZE, w), (lambda i: (lambda b, t: (t[b, i], 0, 0)))(i))
                 for i in range(n_pages)] +
                [pl.BlockSpec((None, nw, 256), lambda b, t: (b, 0, 0)),
                 row(w), row(w), row(256), row(128),
                 pl.BlockSpec((NSA_BLOCK, 256), lambda b, t: (0, 0))])
    return pl.pallas_call(
        functools.partial(_nsa_step_kernel, n_pages=n_pages),
        grid_spec=pltpu.PrefetchScalarGridSpec(
            num_scalar_prefetch=1, grid=(n,), in_specs=in_specs,
            out_specs=pl.BlockSpec((None, 1, w), lambda b, t: (b, 0, 0))),
        out_shape=jax.ShapeDtypeStruct((n, 1, w), F32),
        compiler_params=_cparams(("parallel",)),
        name="nsa_step",
    )(pt, *([kv_pool] * n_pages), win, q, kv_new, win_new, small, wcmp)


def _head_cols(row, eye):
    return [_row_to_col(row[:, h * HEAD_DIM:(h + 1) * HEAD_DIM], eye) for h in range(N_HEADS)]


def _rows_to_block(rows):
    sub = _iota((N_HEADS, rows[0].shape[1]), 0)
    out = jnp.broadcast_to(rows[0], sub.shape)
    for h in range(1, N_HEADS):
        out = jnp.where(sub == h, rows[h], out)
    return out


def _tile_scores(tile_of_head, q_cols):
    return _rows_to_block([jnp.sum(tile_of_head(h) * q_cols[h], axis=0, keepdims=True)
                           for h in range(N_HEADS)])


def _fox_step_kernel_t(pt_ref, *refs, n_pages):
    kv = refs[:n_pages]
    lf = refs[n_pages:2 * n_pages]
    x_ref, sm_ref, bf_ref, o_ref, lf_out_ref = refs[2 * n_pages:]
    w = GROUP_WIDTH
    pg = PAGE_SIZE
    hd = HEAD_DIM
    x = x_ref[...]
    q, kn, vn = x[:, 0:w], x[:, w:2 * w], x[:, 2 * w:3 * w]
    logf_new = _log_sigmoid(sm_ref[...] + bf_ref[...])
    lf_out_ref[...] = logf_new
    eye = _iota((hd, hd), 0) == _iota((hd, hd), 1)
    eye8 = _iota((N_HEADS, N_HEADS), 0) == _iota((N_HEADS, N_HEADS), 1)
    q_cols = _head_cols(q, eye)
    vn_cols = _head_cols(vn, eye)
    sub = _iota((N_HEADS, 1), 0)
    upper = (_iota((pg, pg), 0) <= _iota((pg, pg), 1)).astype(F32)
    c_all = _dot(jnp.concatenate([lf[p][...] for p in range(n_pages)], axis=0), upper, precision=HI)
    carry = jnp.zeros((N_HEADS, 1), F32)
    cs = []
    for p in range(n_pages):
        c = c_all[p * N_HEADS:(p + 1) * N_HEADS, :]
        cs.append(c + carry)
        carry = carry + c[:, pg - 1:pg]
    ctot = carry + _row_to_col(logf_new[:, 0:N_HEADS], eye8)
    prod = kn * q
    s_n = jnp.zeros((N_HEADS, 1), F32)
    for h in range(N_HEADS):
        s_n = jnp.where(sub == h, jnp.sum(prod[:, h * hd:(h + 1) * hd], axis=1, keepdims=True), s_n)
    s_n = s_n * ATT_SCALE
    m = s_n
    ss = []
    for p in range(n_pages):
        s = _tile_scores(lambda h: kv[p][0, h], q_cols) * ATT_SCALE + (ctot - cs[p])
        ss.append(s)
        m = jnp.maximum(m, jnp.max(s, axis=1, keepdims=True))
    e_n = jnp.exp(s_n - m)
    l = e_n
    es = []
    for p in range(n_pages):
        e = jnp.exp(ss[p] - m)
        es.append(e)
        l = l + jnp.sum(e, axis=1, keepdims=True)
    inv_l = 1.0 / l
    outs = []
    for h in range(N_HEADS):
        acc = kv[0][1, h] * es[0][h:h + 1, :]
        for p in range(1, n_pages):
            acc = acc + kv[p][1, h] * es[p][h:h + 1, :]
        o_col = jnp.sum(acc, axis=1, keepdims=True) + vn_cols[h] * e_n[h:h + 1, :]
        outs.append(_col_to_row(o_col * inv_l[h:h + 1, :], eye))
    o_ref[...] = jnp.concatenate(outs, axis=1)


def _fox_step_t(pt, layer, kv_t, lf_t, x, small, bf_row):
    n, n_pages = pt.shape
    w = GROUP_WIDTH
    kv_spec = lambda i: pl.BlockSpec((None, None, 2, N_HEADS, HEAD_DIM, PAGE_SIZE),
                                     lambda b, t: (layer, t[b, i], 0, 0, 0, 0))
    lf_spec = lambda i: pl.BlockSpec((None, None, N_HEADS, PAGE_SIZE), lambda b, t: (layer, t[b, i], 0, 0))
    in_specs = ([kv_spec(i) for i in range(n_pages)] + [lf_spec(i) for i in range(n_pages)] +
                [pl.BlockSpec((None, 1, 3 * w), lambda b, t: (b, 0, 0)),
                 pl.BlockSpec((None, 1, 128), lambda b, t: (b, 0, 0)),
                 pl.BlockSpec((1, 128), lambda b, t: (0, 0))])
    return pl.pallas_call(
        functools.partial(_fox_step_kernel_t, n_pages=n_pages),
        grid_spec=pltpu.PrefetchScalarGridSpec(
            num_scalar_prefetch=1, grid=(n,), in_specs=in_specs,
            out_specs=[pl.BlockSpec((None, 1, w), lambda b, t: (b, 0, 0)),
                       pl.BlockSpec((None, 1, 128), lambda b, t: (b, 0, 0))]),
        out_shape=[jax.ShapeDtypeStruct((n, 1, w), F32), jax.ShapeDtypeStruct((n, 1, 128), F32)],
        compiler_params=_cparams(("parallel",)),
        name="fox_step",
    )(pt, *([kv_t] * n_pages), *([lf_t] * n_pages), x, small, bf_row)


def _nsa_step_kernel_t(pt_ref, *refs, n_pages):
    pgs = refs[:n_pages]
    win_ref, q_ref, kv_ref, wn_ref, sm_ref, wc_ref, o_ref = refs[n_pages:]
    pg = PAGE_SIZE
    hd = HEAD_DIM
    nh = N_HEADS
    past = n_pages * pg
    nbp = past // NSA_BLOCK
    bpp = pg // NSA_BLOCK
    eye = _iota((hd, hd), 0) == _iota((hd, hd), 1)
    q_cols = _head_cols(q_ref[...], eye)
    sub = _iota((nh, 1), 0)
    lane = _iota((1, pg), 1)
    slopes = jnp.exp2(-(sub + 1).astype(F32) * (8.0 / nh))
    grp = lambda h: h // NSA_GROUP
    kv_new = kv_ref[...]
    wn = wn_ref[...]
    wc = wc_ref[...]

    def new_scores(row):
        out = jnp.zeros((nh, 1), F32)
        for h in range(nh):
            k_col = _row_to_col(row[:, grp(h) * hd:(grp(h) + 1) * hd], eye)
            out = jnp.where(sub == h, jnp.sum(k_col * q_cols[h], axis=0, keepdims=True), out)
        return out * ATT_SCALE

    def softmax_lanes(scores, extra=None):
        m = extra
        for s in scores:
            mx = jnp.max(s, axis=1, keepdims=True)
            m = mx if m is None else jnp.maximum(m, mx)
        es = [jnp.exp(s - m) for s in scores]
        l = None if extra is None else jnp.exp(extra - m)
        e_extra = l
        for e in es:
            sm = jnp.sum(e, axis=1, keepdims=True)
            l = sm if l is None else l + sm
        return es, e_extra, l

    def weighted_values(tiles, es):
        cols = []
        for h in range(nh):
            acc = None
            for tile, e in zip(tiles, es):
                part = tile(grp(h)) * e[h:h + 1, :]
                acc = part if acc is None else acc + part
            cols.append(jnp.sum(acc, axis=1, keepdims=True))
        return cols

    half = [lane // NSA_BLOCK == i for i in range(bpp)]
    blk_lane = _iota((1, pg), 1)
    s_c = jnp.full((nh, pg), NEG, F32)
    for p in range(n_pages):
        raw = _tile_scores(lambda h: pgs[p][0, grp(h)], q_cols) * wc[0:1, :]
        for i in range(bpp):
            r = jnp.sum(jnp.where(half[i], raw, 0.0), axis=1, keepdims=True)
            s_c = jnp.where(blk_lane == bpp * p + i, r, s_c)
    dist_c = (past - ((blk_lane + 1) * NSA_BLOCK - 1)).astype(F32)
    s_c = jnp.where(blk_lane < nbp, s_c * ATT_SCALE - slopes * dist_c, NEG)
    (e_c,), _, l_c = softmax_lanes([s_c])
    p_c = e_c / l_c
    pes = []
    for p in range(n_pages):
        pe = jnp.zeros((nh, pg), F32)
        for i in range(bpp):
            pe = jnp.where(half[i], p_c[:, bpp * p + i:bpp * p + i + 1], pe)
        pes.append(pe * wc[1:2, :])
    o_c = weighted_values([(lambda g, p=p: pgs[p][1, g]) for p in range(n_pages)], pes)

    nbl = 32 * ((nbp + 31) // 32)
    eye_b = _iota((nbl, nbl), 0) == _iota((nbl, nbl), 1)
    m_lt_n = _iota((nbl, nbl), 0) < _iota((nbl, nbl), 1)
    blk_row = _iota((1, nbl), 1)
    n_sel = min(NSA_TOPN, nbp + 1)
    sel_rows = []
    for g in range(KVH_NSA):
        imp = jnp.sum(jnp.where(sub // NSA_GROUP == g, p_c, 0.0), axis=0, keepdims=True)[:, 0:nbl]
        forced = (blk_row == 0) | (blk_row >= nbp - 1)
        score = jnp.where(blk_row < nbp, jnp.where(forced, FORCED_SCORE, imp), -1.0)
        score_col = _row_to_col(score, eye_b)
        beats = jnp.where(score_col > score, 1.0, jnp.where((score_col == score) & m_lt_n, 1.0, 0.0))
        cnt = jnp.sum(beats, axis=0, keepdims=True)
        cnt = cnt + jnp.where(score < FORCED_SCORE, 1.0, 0.0)
        sel_rows.append(jnp.where((cnt < n_sel) & (score >= 0.0), 1.0, 0.0))

    scores = []
    for p in range(n_pages):
        dist = (past - (p * pg + lane)).astype(F32)
        s = _tile_scores(lambda h: pgs[p][2, grp(h)], q_cols) * ATT_SCALE - slopes * dist
        chosen = []
        for g in range(KVH_NSA):
            row = jnp.zeros((1, pg), F32)
            for i in range(bpp):
                row = jnp.where(half[i], sel_rows[g][:, bpp * p + i:bpp * p + i + 1], row)
            chosen.append(row)
        mask = jnp.where(sub // NSA_GROUP == 0, chosen[0], chosen[1])
        scores.append(jnp.where(mask > 0.5, s, NEG))
    s_new = new_scores(kv_new[:, 2 * 128:3 * 128])
    e_s, e_sn, l_s = softmax_lanes(scores, s_new)
    o_s = weighted_values([(lambda g, p=p: pgs[p][3, g]) for p in range(n_pages)], e_s)
    v_new = kv_new[:, 3 * 128:4 * 128]
    o_s = [o_s[h] + _row_to_col(v_new[:, grp(h) * hd:(grp(h) + 1) * hd], eye) * e_sn[h:h + 1, :]
           for h in range(nh)]

    nw = win_ref.shape[-1]
    scores = []
    for c0 in range(0, nw, pg):
        dist = nw - (c0 + lane)
        s = _tile_scores(lambda h: win_ref[0, grp(h), :, c0:c0 + pg], q_cols) * ATT_SCALE
        s = s - slopes * dist.astype(F32)
        scores.append(jnp.where((dist < NSA_WINDOW) & (past - dist >= 0), s, NEG))
    s_new = new_scores(wn[:, 0:128])
    e_w, e_wn, l_w = softmax_lanes(scores, s_new)
    o_w = weighted_values([(lambda g, c0=c0: win_ref[1, g, :, c0:c0 + pg]) for c0 in range(0, nw, pg)], e_w)
    vw_new = wn[:, 128:256]
    o_w = [o_w[h] + _row_to_col(vw_new[:, grp(h) * hd:(grp(h) + 1) * hd], eye) * e_wn[h:h + 1, :]
           for h in range(nh)]

    gates = jax.nn.sigmoid(sm_ref[...])
    outs = []
    for h in range(nh):
        c = L_NG + 3 * h
        o = gates[:, c:c + 1] * o_c[h]
        o = o + ((gates[:, c + 1:c + 2] / l_s[h:h + 1, :]) * o_s[h]
                 + (gates[:, c + 2:c + 3] / l_w[h:h + 1, :]) * o_w[h])
        outs.append(_col_to_row(o, eye))
    o_ref[...] = jnp.concatenate(outs, axis=1)


def _nsa_step_t(pt, layer, kv_t, win_t, q, kv_new, win_new, small, wc_rows):
    n, n_pages = pt.shape
    w = GROUP_WIDTH
    nw = win_t.shape[-1]
    row = lambda width: pl.BlockSpec((None, 1, width), lambda b, t: (b, 0, 0))
    page = lambda i: pl.BlockSpec((None, None, 4, KVH_NSA, HEAD_DIM, PAGE_SIZE),
                                  lambda b, t: (layer, t[b, i], 0, 0, 0, 0))
    in_specs = ([page(i) for i in range(n_pages)] +
                [pl.BlockSpec((None, None, 2, KVH_NSA, HEAD_DIM, nw), lambda b, t: (layer, b, 0, 0, 0, 0)),
                 row(w), row(w), row(256), row(128),
                 pl.BlockSpec((2, PAGE_SIZE), lambda b, t: (0, 0))])
    return pl.pallas_call(
        functools.partial(_nsa_step_kernel_t, n_pages=n_pages),
        grid_spec=pltpu.PrefetchScalarGridSpec(
            num_scalar_prefetch=1, grid=(n,), in_specs=in_specs,
            out_specs=pl.BlockSpec((None, 1, w), lambda b, t: (b, 0, 0))),
        out_shape=jax.ShapeDtypeStruct((n, 1, w), F32),
        compiler_params=_cparams(("parallel",)),
        name="nsa_step",
    )(pt, *([kv_t] * n_pages), win_t, q, kv_new, win_new, small, wc_rows)


_O_FF, _O_LX, _O_NQ, _O_NKC, _O_NG, _O_DQKV, _O_DG, _O_DA, _O_END = (
    1536, 1544, 2568, 3080, 3848, 3872, 5408, 5920, 5928)


def _pack_w_in(w):
    d = w.shape[0]
    zeros = lambda n: jnp.zeros((d, n), w.dtype)
    parts = [w[:, 0:_O_FF], w[:, _O_LX:_O_NQ], w[:, _O_NQ:_O_NKC], w[:, _O_NKC:_O_NG],
             w[:, _O_FF:_O_LX], w[:, _O_NG:_O_DQKV], w[:, _O_DA:_O_END], zeros(128 - 40), zeros(128),
             w[:, _O_DQKV:_O_DG], w[:, _O_DG:_O_DA]]
    out = jnp.concatenate(parts, axis=1)
    assert out.shape[1] == N_PROJ
    return out.astype(BF16)


def _block_diag(w):
    n, d, e = w.shape
    eye = jnp.eye(n, dtype=w.dtype)
    return (eye[:, None, :, None] * w[:, :, None, :]).reshape(n * d, n * e)


def _lane_row(vals, start):
    return jnp.zeros((1, 128), F32).at[0, start:start + vals.shape[0]].set(vals)


def _layer_params(l, p):
    row = lambda a: a[l][None, :]
    return dict(
        norm_pre_mix=row(p['norm_pre_mix']), norm_post_mix=row(p['norm_post_mix']),
        norm_pre_mlp=row(p['norm_pre_mlp']), norm_post_mlp=row(p['norm_post_mlp']),
        w_in=_pack_w_in(p['w_in'][l]), w_out=p['w_out'][l].astype(BF16),
        w_up=p['w_up'][l].astype(BF16), w_down=p['w_down'][l].astype(BF16),
        fox_bf_row=_lane_row(p['fox_b_f'][l], L_FF),
        lru_conv_w=p['lru_conv_w'][l], lru_conv_b=row(p['lru_conv_b']),
        lru_wr=_block_diag(p['lru_w_r'][l]).astype(BF16), lru_b_r=row(p['lru_b_r']),
        lru_wi=_block_diag(p['lru_w_i'][l]).astype(BF16), lru_b_i=row(p['lru_b_i']),
        lru_lambda=row(p['lru_lambda']),
        nsa_wcmp=jnp.repeat(p['nsa_w_cmp'][l].T, 128, axis=1),
        gdn_conv_w=p['gdn_conv_w'][l],
        gdn_alog_row=_lane_row(p['gdn_A_log'][l], L_DA), gdn_dt_row=_lane_row(p['gdn_dt_bias'][l], L_DA),
        gdn_norm_w=row(p['gdn_norm_w']))


def kernel(x_prompt, x_sample, cache_fox_kv, cache_fox_logf, cache_nsa_kv, cache_nsa_win,
           state_rglru_conv, state_rglru_h, state_gdn_conv, state_gdn_S, page_table,
           norm_pre_mix, norm_post_mix, norm_pre_mlp, norm_post_mlp, w_in, w_out, w_up, w_down,
           fox_b_f, lru_conv_w, lru_conv_b, lru_w_r, lru_b_r, lru_w_i, lru_b_i, lru_lambda,
           nsa_w_cmp, gdn_conv_w, gdn_A_log, gdn_dt_bias, gdn_norm_w):
    params = dict(norm_pre_mix=norm_pre_mix, norm_post_mix=norm_post_mix, norm_pre_mlp=norm_pre_mlp,
                  norm_post_mlp=norm_post_mlp, w_in=w_in, w_out=w_out, w_up=w_up, w_down=w_down,
                  fox_b_f=fox_b_f, lru_conv_w=lru_conv_w, lru_conv_b=lru_conv_b, lru_w_r=lru_w_r,
                  lru_b_r=lru_b_r, lru_w_i=lru_w_i, lru_b_i=lru_b_i, lru_lambda=lru_lambda,
                  nsa_w_cmp=nsa_w_cmp, gdn_conv_w=gdn_conv_w, gdn_A_log=gdn_A_log,
                  gdn_dt_bias=gdn_dt_bias, gdn_norm_w=gdn_norm_w)
    b, t, d = x_prompt.shape
    ns = x_sample.shape[0]
    depth, n_pool = cache_fox_kv.shape[:2]
    bt = b * t
    w = GROUP_WIDTH
    xp = x_prompt.reshape(bt, d)
    xs = x_sample.reshape(ns, d)
    fox_t = jnp.transpose(cache_fox_kv, (0, 1, 3, 4, 5, 2))
    logf_t = jnp.transpose(cache_fox_logf, (0, 1, 3, 2))
    nsa_t = jnp.transpose(cache_nsa_kv, (0, 1, 3, 4, 5, 2))
    win_t = jnp.transpose(cache_nsa_win, (0, 1, 3, 4, 5, 2))
    outs = [[] for _ in range(16)]
    for l in range(depth):
        prm = _layer_params(l, params)
        proj = _in_proj(xp, prm['norm_pre_mix'], prm['w_in'])
        ps = _in_proj(xs, prm['norm_pre_mix'], prm['w_in'])
        pp = proj.reshape(b, t, N_PROJ)
        row3 = lambda c0, c1: ps[:, None, c0:c1]

        qa, ka, va, logf_p = _fox_prep(proj, prm['fox_bf_row'], b, t)
        oa_p = _fox_flash(qa, ka, va, b, t)
        oa_s, logf_s = _fox_step_t(page_table, l, fox_t, logf_t, row3(C_FQ, C_FQ + 3 * w),
                                   row3(C_SMALL, C_SMALL + 128), prm['fox_bf_row'])

        ob_p, h_p = _lru_prompt(proj, prm, b, t)
        ob_s, h_s = _lru_step(ps[:, C_LX:C_LX + w], ps[:, C_LG:C_LG + w],
                              jnp.moveaxis(state_rglru_conv[l], 1, 0), state_rglru_h[l], prm)

        cmp = _nsa_cmp(proj, prm['nsa_wcmp'], b, t)
        oc_p = _nsa_prompt(proj, cmp, b, t)
        oc_s = _nsa_step_t(page_table, l, nsa_t, win_t, row3(C_NQ, C_NQ + w), row3(C_NKV, C_NKV + w),
                           row3(C_NWIN, C_NWIN + 256), row3(C_SMALL, C_SMALL + 128),
                           jnp.tile(params['nsa_w_cmp'][l], (1, PAGE_SIZE // NSA_BLOCK)))

        od_p, s_p = _gdn_prompt(proj, prm, b, t)
        od_s, s_s = _gdn_step(row3(C_DQKV, C_DQKV + 3 * w), row3(C_DG, C_DG + w),
                              row3(C_SMALL, C_SMALL + 128), state_gdn_conv[l], state_gdn_S[l], prm)

        xp = _out_proj(oa_p, ob_p, oc_p, od_p, xp, prm['w_out'], prm['norm_post_mix'])
        xp = _mlp(xp, prm['norm_pre_mlp'], prm['w_up'], prm['w_down'], prm['norm_post_mlp'])
        flat = lambda a: a.reshape(ns, w)
        xs = _out_proj(flat(oa_s), ob_s, flat(oc_s), flat(od_s), xs, prm['w_out'], prm['norm_post_mix'])
        xs = _mlp(xs, prm['norm_pre_mlp'], prm['w_up'], prm['w_down'], prm['norm_post_mlp'])

        new_win_s = ps[:, C_NWIN:C_NWIN + 256].reshape(ns, 1, 2, KVH_NSA, HEAD_DIM)
        win_all = jnp.concatenate([cache_nsa_win[l], new_win_s], axis=1)
        layer_out = [
            pp[:, :, C_FK:C_FK + 2 * w].reshape(b, t, 2, N_HEADS, HEAD_DIM),
            ps[:, C_FK:C_FK + 2 * w].reshape(ns, 1, 2, N_HEADS, HEAD_DIM),
            logf_p[:, :N_HEADS].reshape(b, t, N_HEADS),
            logf_s[:, :, :N_HEADS],
            pp[:, :, C_NKV:C_NKV + w].reshape(b, t, 4, KVH_NSA, HEAD_DIM),
            ps[:, C_NKV:C_NKV + w].reshape(ns, 1, 4, KVH_NSA, HEAD_DIM),
            pp[:, t - min(NSA_WINDOW, t):, C_NWIN:C_NWIN + 256].reshape(b, min(NSA_WINDOW, t), 2, KVH_NSA, HEAD_DIM),
            win_all[:, -NSA_WINDOW:],
            pp[:, t - (CONV_W - 1):, C_LX:C_LX + w],
            jnp.concatenate([state_rglru_conv[l][:, 1:], ps[:, None, C_LX:C_LX + w]], axis=1),
            h_p[:, 0],
            h_s,
            pp[:, t - (CONV_W - 1):, C_DQKV:C_DQKV + 3 * w],
            jnp.concatenate([state_gdn_conv[l][:, 1:], ps[:, None, C_DQKV:C_DQKV + 3 * w]], axis=1),
            s_p,
            s_s,
        ]
        for i, a in enumerate(layer_out):
            outs[i].append(a)
    y_prompt = xp.reshape(b, t, d)
    y_sample = xs.reshape(ns, 1, d)
    return (y_prompt, y_sample) + tuple(jnp.stack(o) for o in outs)
```
